```python
import math
import jax, jax.numpy as jnp
from jax import lax
import numpy as np

D_MODEL = 1024
BATCH = 8
SEQ = 4096
DEPTH = 2

HEAD_DIM = 64
N_MLSTM_HEADS = 4
N_DIL_HEADS = 6
N_SB_HEADS = 6
N_HEADS = N_MLSTM_HEADS + N_DIL_HEADS + N_SB_HEADS
D_MIX = N_HEADS * HEAD_DIM
D_ML = N_MLSTM_HEADS * HEAD_DIM
D_DIL = N_DIL_HEADS * HEAD_DIM
D_SB = N_SB_HEADS * HEAD_DIM
CONV_WIDTH = 4
MLSTM_CHUNK = 128
FGATE_BIAS_RANGE = (3.0, 6.0)
DIL_PATTERNS = ((128, 1), (512, 4), (2048, 16))
DIL_BLOCK = 128
ALIBI_MAX_BIAS = 8.0
SB_BLOCK = 128
N_EXPERTS = 16
N_GROUPS = 4
EXPERTS_PER_GROUP = N_EXPERTS // N_GROUPS
TOP_K = 2
D_EXPERT = 512
N_MOD = 6
EPS = 1e-6

OFF_ML_X = 0
OFF_ML_O = OFF_ML_X + D_ML
OFF_ML_I = OFF_ML_O + D_ML
OFF_ML_F = OFF_ML_I + N_MLSTM_HEADS
OFF_DIL = OFF_ML_F + N_MLSTM_HEADS
OFF_SB = OFF_DIL + 3 * D_DIL
D_IN_PROJ = OFF_SB + 3 * D_SB

kernel_name = 'hybrid_mlstm_dilated_stickbreaking_moe'


def rms_norm(x):
    xf = x.astype(jnp.float32)
    return xf * lax.rsqrt(jnp.mean(xf * xf, axis=-1, keepdims=True) + EPS)


def split_heads(t, n_heads):
    b, s, _ = t.shape
    return t.reshape(b, s, n_heads, HEAD_DIM).transpose(0, 2, 1, 3).astype(jnp.float32)


def mlstm_chunkwise(q, k, v, ig, lf):
    b, h, s, dh = q.shape
    nc = s // MLSTM_CHUNK

    def chunks(t):
        return jnp.moveaxis(t.reshape(b, h, nc, MLSTM_CHUNK, *t.shape[3:]), 2, 0)

    causal = jnp.tril(jnp.ones((MLSTM_CHUNK, MLSTM_CHUNK), dtype=bool))

    def step(carry, xs):
        c_st, n_st, m_st = carry
        qc, kc, vc, igc, lfc = xs
        cum = jnp.cumsum(lfc, axis=-1)
        log_d = jnp.where(causal, cum[..., :, None] - cum[..., None, :] + igc[..., None, :], -jnp.inf)
        log_inter = cum + m_st[..., None]
        m_t = jnp.maximum(log_inter, jnp.max(log_d, axis=-1))
        dmat = jnp.exp(log_d - m_t[..., None])
        inter = jnp.exp(log_inter - m_t)
        sw = jnp.einsum('bhtd,bhsd->bhts', qc, kc) * dmat
        num = inter[..., None] * jnp.einsum('bhtd,bhde->bhte', qc, c_st) + jnp.einsum('bhts,bhse->bhte', sw, vc)
        den = inter * jnp.einsum('bhtd,bhd->bht', qc, n_st) + jnp.sum(sw, axis=-1)
        h_out = num / jnp.maximum(jnp.abs(den), jnp.exp(-m_t))[..., None]
        last = cum[..., -1]
        log_w = last[..., None] - cum + igc
        m_new = jnp.maximum(last + m_st, jnp.max(log_w, axis=-1))
        w = jnp.exp(log_w - m_new[..., None])
        decay = jnp.exp(last + m_st - m_new)
        c_new = decay[..., None, None] * c_st + jnp.einsum('bhs,bhsd,bhse->bhde', w, kc, vc)
        n_new = decay[..., None] * n_st + jnp.einsum('bhs,bhsd->bhd', w, kc)
        return (c_new, n_new, m_new), h_out

    init = (jnp.zeros((b, h, dh, dh), jnp.float32), jnp.zeros((b, h, dh), jnp.float32),
            jnp.zeros((b, h), jnp.float32))
    _, hs = lax.scan(step, init, (chunks(q), chunks(k), chunks(v), chunks(ig), chunks(lf)))
    return jnp.moveaxis(hs, 0, 2).reshape(b, h, s, dh)


def mlstm_mixer(u, conv_w, conv_b, w_mq, w_mk, w_mv, gate_bias):
    xm = u[..., OFF_ML_X:OFF_ML_X + D_ML]
    xc = lax.conv_general_dilated(xm, conv_w[:, None, :].astype(xm.dtype), (1,), [(CONV_WIDTH - 1, 0)],
                                  dimension_numbers=('NWC', 'WIO', 'NWC'), feature_group_count=D_ML)
    xc = jax.nn.silu(xc + conv_b)
    xc_h = split_heads(xc, N_MLSTM_HEADS)
    q = jnp.einsum('bhsd,hde->bhse', xc_h, w_mq).astype(jnp.float32)
    k = (jnp.einsum('bhsd,hde->bhse', xc_h, w_mk) * HEAD_DIM ** -0.5).astype(jnp.float32)
    v = jnp.einsum('bhsd,hde->bhse', split_heads(xm, N_MLSTM_HEADS), w_mv).astype(jnp.float32)
    ig = (u[..., OFF_ML_I:OFF_ML_I + N_MLSTM_HEADS] + gate_bias[0]).astype(jnp.float32).transpose(0, 2, 1)
    lf = jax.nn.log_sigmoid((u[..., OFF_ML_F:OFF_ML_F + N_MLSTM_HEADS] + gate_bias[1]).astype(jnp.float32)).transpose(0, 2, 1)
    o = jax.nn.sigmoid(split_heads(u[..., OFF_ML_O:OFF_ML_O + D_ML], N_MLSTM_HEADS))
    return o * mlstm_chunkwise(q, k, v, ig, lf)


def dilated_pattern(q, k, v, slopes, window, dilation):
    b, h, s, dh = q.shape
    length = s // dilation
    span = window // dilation
    bq = math.gcd(length, DIL_BLOCK)
    nblk = length // bq

    def by_residue(t):
        return t.reshape(b, h, length, dilation, dh).transpose(0, 1, 3, 2, 4)

    pad = ((0, 0), (0, 0), (0, 0), (span, 0), (0, 0))
    qb = by_residue(q).reshape(b, h, dilation, nblk, bq, dh)
    kp = jnp.pad(by_residue(k), pad)
    vp = jnp.pad(by_residue(v), pad)
    key_idx = jnp.arange(nblk)[:, None] * bq + jnp.arange(bq + span)[None, :]
    kb = kp[:, :, :, key_idx, :]
    vb = vp[:, :, :, key_idx, :]
    steps = jnp.arange(bq)[:, None] - jnp.arange(bq + span)[None, :] + span
    key_pos = key_idx[:, None, :] - span
    valid = (steps >= 0) & (steps <= span) & (key_pos >= 0)
    scores = jnp.einsum('bhrnqd,bhrnkd->bhrnqk', qb, kb) * HEAD_DIM ** -0.5
    scores = scores - slopes[None, :, None, None, None, None] * (steps * dilation).astype(jnp.float32)
    scores = jnp.where(valid, scores, -jnp.inf)
    mx = jnp.max(scores, axis=-1, keepdims=True)
    p = jnp.exp(scores - mx)
    den = jnp.sum(p, axis=-1)
    out = jnp.einsum('bhrnqk,bhrnkd->bhrnqd', p, vb) / den[..., None]
    lse = mx[..., 0] + jnp.log(den)
    out = out.reshape(b, h, dilation, length, dh).transpose(0, 1, 3, 2, 4).reshape(b, h, s, dh)
    lse = lse.reshape(b, h, dilation, length).transpose(0, 1, 3, 2).reshape(b, h, s)
    return out, lse


def dilated_mixer(u):
    q = split_heads(u[..., OFF_DIL:OFF_DIL + D_DIL], N_DIL_HEADS)
    k = split_heads(u[..., OFF_DIL + D_DIL:OFF_DIL + 2 * D_DIL], N_DIL_HEADS)
    v = split_heads(u[..., OFF_DIL + 2 * D_DIL:OFF_DIL + 3 * D_DIL], N_DIL_HEADS)
    slopes = jnp.exp2(-ALIBI_MAX_BIAS * jnp.arange(1, N_DIL_HEADS + 1, dtype=jnp.float32) / N_DIL_HEADS)
    outs, lses = [], []
    for window, dilation in DIL_PATTERNS:
        o, l = dilated_pattern(q, k, v, slopes, window, dilation)
        outs.append(o)
        lses.append(l)
    weights = jax.nn.softmax(jnp.stack(lses), axis=0)
    return jnp.einsum('pbhs,pbhsd->bhsd', weights, jnp.stack(outs))


def stick_breaking_mixer(u):
    q = split_heads(u[..., OFF_SB:OFF_SB + D_SB], N_SB_HEADS)
    k = split_heads(u[..., OFF_SB + D_SB:OFF_SB + 2 * D_SB], N_SB_HEADS)
    v = split_heads(u[..., OFF_SB + 2 * D_SB:OFF_SB + 3 * D_SB], N_SB_HEADS)
    b, h, s, dh = q.shape
    nblk = s // SB_BLOCK
    qb = jnp.moveaxis(q.reshape(b, h, nblk, SB_BLOCK, dh), 2, 0)
    key_pos = jnp.arange(s)

    def block(args):
        qblk, bi = args
        z = jnp.einsum('bhqd,bhkd->bhqk', qblk, k) * HEAD_DIM ** -0.5
        qpos = bi * SB_BLOCK + jnp.arange(SB_BLOCK)
        causal = key_pos[None, :] < qpos[:, None]
        log_keep = jnp.where(causal, jax.nn.log_sigmoid(-z), 0.0)
        log_rest = lax.cumsum(log_keep, axis=3, reverse=True) - log_keep
        a = jnp.where(causal, jnp.exp(jax.nn.log_sigmoid(z) + log_rest), 0.0)
        return jnp.einsum('bhqk,bhkd->bhqd', a, v)

    ob = lax.map(block, (qb, jnp.arange(nblk)))
    return jnp.moveaxis(ob, 0, 2).reshape(b, h, s, dh)


def routed_moe(h, w_router, router_bias, w_gate_e, w_up_e, w_down_e):
    b, s, d = h.shape
    t = h.reshape(b * s, d)
    scores = jax.nn.sigmoid((t @ w_router).astype(jnp.float32))
    grouped = (scores + router_bias).reshape(-1, N_GROUPS, EXPERTS_PER_GROUP)
    group_score = jnp.sum(lax.top_k(grouped, TOP_K)[0], axis=-1)
    _, g_sel = lax.top_k(group_score, 1)
    in_group = jnp.take_along_axis(grouped, g_sel[:, :, None], axis=1)[:, 0]
    _, e_loc = lax.top_k(in_group, TOP_K)
    e_idx = g_sel * EXPERTS_PER_GROUP + e_loc
    w = jnp.take_along_axis(scores, e_idx, axis=1)
    w = w / jnp.sum(w, axis=-1, keepdims=True)
    combine = jnp.einsum('tk,tke->te', w, jax.nn.one_hot(e_idx, N_EXPERTS, dtype=jnp.float32))
    y = jnp.zeros((b * s, d), jnp.float32)
    for e in range(N_EXPERTS):
        he = jax.nn.silu(t @ w_gate_e[e]) * (t @ w_up_e[e])
        y = y + combine[:, e:e + 1] * (he @ w_down_e[e])
    return y.reshape(b, s, d)


def setup_inputs(seed: int = 0) -> dict:
    key = jax.random.key(seed)
    ks = jax.random.split(key, 20)
    f32 = jnp.float32

    def nrm(k, shape, scale):
        return jax.random.normal(k, shape, f32) * scale

    fgate = jnp.linspace(FGATE_BIAS_RANGE[0], FGATE_BIAS_RANGE[1], N_MLSTM_HEADS, dtype=f32)
    gate_bias = jnp.stack([nrm(ks[8], (DEPTH, N_MLSTM_HEADS), 0.1),
                           fgate[None, :] + nrm(ks[9], (DEPTH, N_MLSTM_HEADS), 0.1)], axis=1)
    return {
        'x': nrm(ks[0], (BATCH, SEQ, D_MODEL), 1.0),
        'c': nrm(ks[1], (BATCH, D_MODEL), 1.0),
        'w_in': nrm(ks[2], (DEPTH, D_MODEL, D_IN_PROJ), D_MODEL ** -0.5),
        'conv_w': nrm(ks[3], (DEPTH, CONV_WIDTH, D_ML), CONV_WIDTH ** -0.5),
        'conv_b': nrm(ks[4], (DEPTH, D_ML), 0.02),
        'w_mq': nrm(ks[5], (DEPTH, N_MLSTM_HEADS, HEAD_DIM, HEAD_DIM), HEAD_DIM ** -0.5),
        'w_mk': nrm(ks[6], (DEPTH, N_MLSTM_HEADS, HEAD_DIM, HEAD_DIM), HEAD_DIM ** -0.5),
        'w_mv': nrm(ks[7], (DEPTH, N_MLSTM_HEADS, HEAD_DIM, HEAD_DIM), HEAD_DIM ** -0.5),
        'gate_bias': gate_bias,
        'g_head': 1.0 + nrm(ks[10], (DEPTH, D_MIX), 0.02),
        'w_out': nrm(ks[11], (DEPTH, D_MIX, D_MODEL), D_MIX ** -0.5),
        'w_ada': nrm(ks[12], (DEPTH, D_MODEL, N_MOD * D_MODEL), 0.5 * D_MODEL ** -0.5),
        'b_ada': nrm(ks[13], (DEPTH, N_MOD * D_MODEL), 0.02),
        'w_router': nrm(ks[14], (D_MODEL, N_EXPERTS), D_MODEL ** -0.5),
        'router_bias': nrm(ks[15], (N_EXPERTS,), 0.01),
        'w_gate_e': nrm(ks[16], (DEPTH, N_EXPERTS, D_MODEL, D_EXPERT), D_MODEL ** -0.5),
        'w_up_e': nrm(ks[17], (DEPTH, N_EXPERTS, D_MODEL, D_EXPERT), D_MODEL ** -0.5),
        'w_down_e': nrm(ks[18], (DEPTH, N_EXPERTS, D_EXPERT, D_MODEL), D_EXPERT ** -0.5),
        'g_final': 1.0 + nrm(ks[19], (D_MODEL,), 0.02),
    }


def reference(x, c, w_in, conv_w, conv_b, w_mq, w_mk, w_mv, gate_bias, g_head, w_out,
              w_ada, b_ada, w_router, router_bias, w_gate_e, w_up_e, w_down_e, g_final):
    b, s, _ = x.shape
    c_act = jax.nn.silu(c.astype(jnp.float32))
    for l in range(DEPTH):
        mod = c_act @ w_ada[l] + b_ada[l]
        shift1, scale1, gate1, shift2, scale2, gate2 = [m[:, None, :] for m in jnp.split(mod, N_MOD, axis=-1)]
        h = rms_norm(x) * (1.0 + scale1) + shift1
        u = h @ w_in[l]
        y_ml = mlstm_mixer(u, conv_w[l], conv_b[l], w_mq[l], w_mk[l], w_mv[l], gate_bias[l])
        y_dil = dilated_mixer(u)
        y_sb = stick_breaking_mixer(u)
        y = jnp.concatenate([y_ml, y_dil, y_sb], axis=1)
        y = rms_norm(y) * g_head[l].reshape(N_HEADS, 1, HEAD_DIM)
        y = y.transpose(0, 2, 1, 3).reshape(b, s, D_MIX) @ w_out[l]
        x = x + gate1 * y
        h = rms_norm(x) * (1.0 + scale2) + shift2
        x = x + gate2 * routed_moe(h, w_router, router_bias, w_gate_e[l], w_up_e[l], w_down_e[l])
    return rms_norm(x) * g_final
```

```python
import functools
import math

import jax
import jax.numpy as jnp
from jax import lax
from jax.experimental import pallas as pl
from jax.experimental.pallas import tpu as pltpu

HEAD_DIM = 64
N_ML = 4
N_DIL = 6
N_SB = 6
D_ML = N_ML * HEAD_DIM
D_DIL = N_DIL * HEAD_DIM
D_SB = N_SB * HEAD_DIM
CONV_WIDTH = 4
CHUNK = 128
DIL_PATTERNS = ((128, 1), (512, 4), (2048, 16))
ALIBI_MAX_BIAS = 8.0
N_EXPERTS = 16
N_GROUPS = 4
EPG = N_EXPERTS // N_GROUPS
EPS = 1e-6
LANES = 128
ML_COLS = 2 * D_ML + LANES
SB_DEAD_LOG = -104.0
VMEM_LIMIT = 56 * 1024 * 1024

_NT = (((1,), (1,)), ((), ()))
_TN = (((0,), (0,)), ((), ()))

f32 = jnp.float32
bf16 = jnp.bfloat16


def _dot(a, b):
    return jnp.dot(a, b, preferred_element_type=f32)


def _dot_nt(a, b):
    return lax.dot_general(a, b, _NT, preferred_element_type=f32)


def _split_bf16(x):
    hi = x.astype(bf16)
    lo = (x - hi.astype(f32)).astype(bf16)
    return hi, lo


def _iota(shape, dim):
    return lax.broadcasted_iota(jnp.int32, shape, dim)


def _rms(x):
    return x * lax.rsqrt(jnp.mean(x * x, axis=-1, keepdims=True) + EPS)


def _cparams(sem):
    return pltpu.CompilerParams(dimension_semantics=sem, vmem_limit_bytes=VMEM_LIMIT)


def _ada_kernel(c_ref, w_ref, b_ref, o_ref):
    c = c_ref[...]
    c_act = (c * lax.logistic(c)).astype(bf16)
    o_ref[...] = _dot(c_act, w_ref[...].astype(bf16)) + b_ref[...]


def _ada(c, w_ada, b_ada):
    depth, d, nd = w_ada.shape
    b = c.shape[0]
    nblk = nd // d
    return pl.pallas_call(
        _ada_kernel,
        out_shape=jax.ShapeDtypeStruct((depth, b, nd), f32),
        grid=(depth, nblk),
        in_specs=[pl.BlockSpec((b, d), lambda l, j: (0, 0)),
                  pl.BlockSpec((None, d, d), lambda l, j: (l, 0, j)),
                  pl.BlockSpec((None, 1, d), lambda l, j: (l, 0, j))],
        out_specs=pl.BlockSpec((None, b, d), lambda l, j: (l, 0, j)),
        compiler_params=_cparams(("parallel", "parallel")),
        name="ada_mod",
    )(c, w_ada, b_ada.reshape(depth, 1, nd))


def _inproj_kernel(x_ref, mod_ref, wml_ref, wdil_ref, wsb_ref, wgt_ref,
                   uml_ref, udil_ref, usb_ref, grow_ref):
    x = x_ref[...]
    mod = mod_ref[...]
    h = (_rms(x) * (1.0 + mod[1:2, :]) + mod[0:1, :]).astype(bf16)
    uml_ref[...] = _dot(h, wml_ref[...])
    udil_ref[...] = _dot(h, wdil_ref[...]).astype(bf16)
    usb_ref[...] = _dot(h, wsb_ref[...]).astype(bf16)
    grow_ref[...] = _dot_nt(wgt_ref[...], h)


def _inproj(x, mod, w_ml, w_dil, w_sb, w_gt, tm):
    b, s, d = x.shape
    full = lambda a: pl.BlockSpec(a.shape, lambda i, j: (0,) * a.ndim)
    return pl.pallas_call(
        _inproj_kernel,
        out_shape=(jax.ShapeDtypeStruct((b, s, ML_COLS), f32),
                   jax.ShapeDtypeStruct((b, s, 3 * D_DIL), bf16),
                   jax.ShapeDtypeStruct((b, s, 3 * D_SB), bf16),
                   jax.ShapeDtypeStruct((b, 2 * N_ML, s), f32)),
        grid=(b, s // tm),
        in_specs=[pl.BlockSpec((None, tm, d), lambda i, j: (i, j, 0)),
                  pl.BlockSpec((None, 6, d), lambda i, j: (i, 0, 0)),
                  full(w_ml), full(w_dil), full(w_sb), full(w_gt)],
        out_specs=(pl.BlockSpec((None, tm, ML_COLS), lambda i, j: (i, j, 0)),
                   pl.BlockSpec((None, tm, 3 * D_DIL), lambda i, j: (i, j, 0)),
                   pl.BlockSpec((None, tm, 3 * D_SB), lambda i, j: (i, j, 0)),
                   pl.BlockSpec((None, 2 * N_ML, tm), lambda i, j: (i, 0, j))),
        compiler_params=_cparams(("parallel", "parallel")),
        name="norm_inproj",
    )(x, mod, w_ml, w_dil, w_sb, w_gt)


def _mlstm_kernel(u_ref, grow_ref, convw_ref, convb_ref, wq_ref, wk_ref, wv_ref, gbc_ref, gbr_ref,
                  y_ref, prev_ref, c_ref, n_ref, m_ref):
    L = CHUNK

    @pl.when(pl.program_id(1) == 0)
    def _():
        prev_ref[...] = jnp.zeros_like(prev_ref)
        c_ref[...] = jnp.zeros_like(c_ref)
        n_ref[...] = jnp.zeros_like(n_ref)
        m_ref[...] = jnp.zeros_like(m_ref)

    u = u_ref[...]
    xm = u[:, :D_ML]
    og = u[:, D_ML:2 * D_ML]
    gcol = u[:, 2 * D_ML:] + gbc_ref[...]
    grow = grow_ref[...] + gbr_ref[...]

    prev = prev_ref[...]
    row = _iota((L, D_ML), 0)
    convw = convw_ref[...]
    xc = xm * convw[CONV_WIDTH - 1:CONV_WIDTH, :]
    for sft in range(1, CONV_WIDTH):
        shifted = jnp.where(row < sft, pltpu.roll(prev, sft, 0), pltpu.roll(xm, sft, 0))
        xc = xc + shifted * convw[CONV_WIDTH - 1 - sft:CONV_WIDTH - sft, :]
    prev_ref[...] = xm
    xc = xc + convb_ref[...]
    xc = xc * lax.logistic(xc)
    xcb = xc.astype(bf16)
    q = _dot(xcb, wq_ref[...])
    k = _dot(xcb, wk_ref[...])
    v = _dot(xm.astype(bf16), wv_ref[...])
    kb = k.astype(bf16)
    vb = v.astype(bf16)

    def logsig(t):
        return jnp.minimum(t, 0.0) - jnp.log1p(jnp.exp(-jnp.abs(t)))

    lf_col = logsig(gcol)
    lf_row = logsig(grow)
    ltri = (_iota((L, L), 1) <= _iota((L, L), 0)).astype(bf16)
    utri = (_iota((L, L), 0) <= _iota((L, L), 1)).astype(bf16)
    ch, cl = _split_bf16(lf_col)
    cum_col = _dot(ltri, ch) + _dot(ltri, cl)
    rh, rl = _split_bf16(lf_row)
    cum_row = _dot(rh, utri) + _dot(rl, utri)

    causal = _iota((L, L), 1) <= _iota((L, L), 0)
    lane = _iota((1, D_ML), 1)
    c_bd = c_ref[...]
    n_st = n_ref[...]
    m_st = m_ref[...]
    c_bdb = c_bd.astype(bf16)

    hmix = jnp.zeros((L, D_ML), f32)
    wfull = jnp.zeros((L, D_ML), f32)
    decay_l = jnp.zeros((1, D_ML), f32)
    mnew_l = jnp.zeros((1, D_ML), f32)
    for hd in range(N_ML):
        mh = (lane >= hd * HEAD_DIM) & (lane < (hd + 1) * HEAD_DIM)
        cumc = cum_col[:, N_ML + hd:N_ML + hd + 1]
        cumr = cum_row[N_ML + hd:N_ML + hd + 1, :]
        igr = grow[hd:hd + 1, :]
        igc = gcol[:, hd:hd + 1]
        m_prev = m_st[:, hd * HEAD_DIM:hd * HEAD_DIM + 1]
        log_d = jnp.where(causal, cumc - cumr + igr, -jnp.inf)
        log_inter = cumc + m_prev
        m_t = jnp.maximum(log_inter, jnp.max(log_d, axis=-1, keepdims=True))
        dmat = jnp.exp(log_d - m_t)
        inter = jnp.exp(log_inter - m_t)
        qh = jnp.where(mh, q, 0.0)
        qhb = qh.astype(bf16)
        sw = _dot_nt(qhb, kb) * dmat
        num = inter * _dot(qhb, c_bdb) + _dot(sw.astype(bf16), vb)
        den = inter * jnp.sum(qh * n_st, axis=-1, keepdims=True) + jnp.sum(sw, axis=-1, keepdims=True)
        hout = num / jnp.maximum(jnp.abs(den), jnp.exp(-m_t))
        hmix = jnp.where(mh, hout, hmix)
        last = cumc[L - 1:L, :]
        log_w = last - cumc + igc
        m_new = jnp.maximum(last + m_prev, jnp.max(log_w, axis=0, keepdims=True))
        wfull = jnp.where(mh, jnp.exp(log_w - m_new), wfull)
        decay_l = jnp.where(mh, jnp.exp(last + m_prev - m_new), decay_l)
        mnew_l = jnp.where(mh, m_new, mnew_l)

    kw = k * wfull
    upd = lax.dot_general(kw.astype(bf16), vb, _TN, preferred_element_type=f32)
    same_head = (_iota((D_ML, D_ML), 0) // HEAD_DIM) == (_iota((D_ML, D_ML), 1) // HEAD_DIM)
    c_ref[...] = decay_l * c_bd + jnp.where(same_head, upd, 0.0)
    n_ref[...] = decay_l * n_st + jnp.sum(kw, axis=0, keepdims=True)
    m_ref[...] = mnew_l
    y_ref[...] = lax.logistic(og) * hmix


def _mlstm(u_ml, g_row, conv_w, conv_b, wq_bd, wk_bd, wv_bd, gb_col, gb_row):
    b, s, _ = u_ml.shape
    full = lambda a: pl.BlockSpec(a.shape, lambda i, j: (0,) * a.ndim)
    return pl.pallas_call(
        _mlstm_kernel,
        out_shape=jax.ShapeDtypeStruct((b, s, D_ML), f32),
        grid=(b, s // CHUNK),
        in_specs=[pl.BlockSpec((None, CHUNK, ML_COLS), lambda i, j: (i, j, 0)),
                  pl.BlockSpec((None, 2 * N_ML, CHUNK), lambda i, j: (i, 0, j)),
                  full(conv_w), full(conv_b), full(wq_bd), full(wk_bd), full(wv_bd), full(gb_col), full(gb_row)],
        out_specs=pl.BlockSpec((None, CHUNK, D_ML), lambda i, j: (i, j, 0)),
        scratch_shapes=[pltpu.VMEM((CHUNK, D_ML), f32), pltpu.VMEM((D_ML, D_ML), f32),
                        pltpu.VMEM((1, D_ML), f32), pltpu.VMEM((1, D_ML), f32)],
        compiler_params=_cparams(("parallel", "arbitrary")),
        name="mlstm",
    )(u_ml, g_row, conv_w, conv_b, wq_bd, wk_bd, wv_bd, gb_col, gb_row)


def _dil_kernel(q_ref, kp_ref, k_ref, vp_ref, v_ref, slope_ref, o_ref, lse_ref, kbuf, vbuf, *, nblk):
    L = CHUNK
    first = pl.program_id(2) == 0
    kbuf[0:L, :] = kp_ref[...]
    kbuf[L:, :] = k_ref[...]
    vbuf[0:L, :] = vp_ref[...]
    vbuf[L:, :] = v_ref[...]
    row = _iota((L, L), 0)
    col = _iota((L, L), 1)
    step_prev = (row - col + L).astype(f32)
    step_cur = (row - col).astype(f32)
    valid_cur = col <= row
    lane = _iota((1, LANES), 1)
    m0 = lane < HEAD_DIM

    def body(blk, carry):
        r0 = pl.multiple_of(blk * L, L)
        valid_prev = (col >= row) & jnp.logical_not(first & (blk == 0))
        for hp in range(N_DIL // 2):
            cs = slice(hp * LANES, (hp + 1) * LANES)
            q = q_ref[pl.ds(r0, L), cs]
            kp = kbuf[pl.ds(r0, L), cs]
            kc = kbuf[pl.ds(r0 + L, L), cs]
            vp = vbuf[pl.ds(r0, L), cs]
            vc = vbuf[pl.ds(r0 + L, L), cs]
            outs, lses = [], []
            for j in range(2):
                mj = m0 if j == 0 else jnp.logical_not(m0)
                qj = jnp.where(mj, q, jnp.zeros_like(q))
                slope = slope_ref[2 * hp + j][0:1, :]
                sp = jnp.where(valid_prev, _dot_nt(qj, kp) - slope * step_prev, -jnp.inf)
                sc = jnp.where(valid_cur, _dot_nt(qj, kc) - slope * step_cur, -jnp.inf)
                mx = jnp.maximum(jnp.max(sp, axis=-1, keepdims=True), jnp.max(sc, axis=-1, keepdims=True))
                pp = jnp.exp(sp - mx)
                pc = jnp.exp(sc - mx)
                den = jnp.sum(pp, axis=-1, keepdims=True) + jnp.sum(pc, axis=-1, keepdims=True)
                outs.append((_dot(pp.astype(bf16), vp) + _dot(pc.astype(bf16), vc)) / den)
                lses.append(mx + jnp.log(den))
            o_ref[pl.ds(r0, L), cs] = jnp.where(m0, outs[0], outs[1]).astype(o_ref.dtype)
            lse_ref[pl.ds(r0, L), cs] = jnp.where(m0, lses[0], lses[1])
        return carry

    lax.fori_loop(0, nblk, body, 0)


def _dilated(u_dil, slopes_d, dilation):
    b, s, _ = u_dil.shape
    length = s // dilation
    tq = min(4 * CHUNK, length)
    nblk = tq // CHUNK
    qkv = u_dil.reshape(b, length, dilation * 3 * D_DIL)
    blk_per_tile = tq // CHUNK
    cur = lambda off: pl.BlockSpec((None, tq, D_DIL), lambda i, r, n: (i, n, 3 * r + off))
    prv = lambda off: pl.BlockSpec((None, CHUNK, D_DIL),
                                   lambda i, r, n: (i, jnp.maximum(n * blk_per_tile - 1, 0), 3 * r + off))
    out, lse = pl.pallas_call(
        functools.partial(_dil_kernel, nblk=nblk),
        out_shape=(jax.ShapeDtypeStruct((b, length, dilation * D_DIL), bf16),
                   jax.ShapeDtypeStruct((b, length, dilation * D_DIL), f32)),
        grid=(b, dilation, length // tq),
        in_specs=[cur(0), prv(1), cur(1), prv(2), cur(2),
                  pl.BlockSpec(slopes_d.shape, lambda i, r, n: (0, 0, 0))],
        out_specs=(pl.BlockSpec((None, tq, D_DIL), lambda i, r, n: (i, n, r)),
                   pl.BlockSpec((None, tq, D_DIL), lambda i, r, n: (i, n, r))),
        scratch_shapes=[pltpu.VMEM((tq + CHUNK, D_DIL), bf16), pltpu.VMEM((tq + CHUNK, D_DIL), bf16)],
        compiler_params=_cparams(("parallel", "parallel", "arbitrary")),
        name=f"dilated_d{dilation}",
    )(qkv, qkv, qkv, qkv, qkv, slopes_d)
    return out.reshape(b, s, D_DIL), lse.reshape(b, s, D_DIL)


def _sb_kernel(q_ref, k_ref, v_ref, o_ref):
    L = CHUNK
    qi = pl.program_id(2)
    q = q_ref[...]
    row = _iota((L, L), 0)
    col = _iota((L, L), 1)
    strict = col < row
    lane = _iota((1, LANES), 1)
    m0 = lane < HEAD_DIM
    q0 = jnp.where(m0, q, jnp.zeros_like(q))
    q1 = jnp.where(m0, jnp.zeros_like(q), q)
    csum = jnp.concatenate([(_iota((L, L), 0) > _iota((L, L), 1)).astype(bf16), jnp.ones((L, L), bf16)], axis=1)

    def head(qj, kblk, vblk, valid, rsum, acc):
        z = _dot_nt(qj, kblk)
        lk_raw = -(jnp.maximum(z, 0.0) + jnp.log1p(jnp.exp(-jnp.abs(z))))
        lk = jnp.where(valid, lk_raw, 0.0)
        hi, lo = _split_bf16(lk)
        t = _dot(hi, csum) + _dot(lo, csum)
        a = jnp.where(valid, jnp.exp(z + lk_raw + rsum + t[:, :L]), 0.0)
        return rsum + t[:, L:], acc + _dot(a.astype(bf16), vblk)

    def cond(c):
        return c[0]

    def body(c):
        _, kb, r0, a0, r1, a1 = c
        k0 = pl.multiple_of(kb * L, L)
        kblk = k_ref[pl.ds(k0, L), :]
        vblk = v_ref[pl.ds(k0, L), :]
        valid = strict | (kb < qi)
        r0, a0 = head(q0, kblk, vblk, valid, r0, a0)
        r1, a1 = head(q1, kblk, vblk, valid, r1, a1)
        alive = jnp.maximum(jnp.max(r0), jnp.max(r1)) > SB_DEAD_LOG
        return (jnp.logical_and(kb > 0, alive), kb - 1, r0, a0, r1, a1)

    z0 = jnp.zeros((L, L), f32)
    _, _, _, a0, _, a1 = lax.while_loop(cond, body, (qi >= 0, qi, z0, z0, z0, z0))
    o_ref[...] = jnp.where(m0, a0, a1).astype(o_ref.dtype)


def _stickbreak(u_sb):
    b, s, _ = u_sb.shape
    npair = N_SB // 2
    return pl.pallas_call(
        _sb_kernel,
        out_shape=jax.ShapeDtypeStruct((b, s, D_SB), bf16),
        grid=(b, npair, s // CHUNK),
        in_specs=[pl.BlockSpec((None, CHUNK, LANES), lambda i, h, n: (i, n, h)),
                  pl.BlockSpec((None, s, LANES), lambda i, h, n: (i, 0, npair + h)),
                  pl.BlockSpec((None, s, LANES), lambda i, h, n: (i, 0, 2 * npair + h))],
        out_specs=pl.BlockSpec((None, CHUNK, LANES), lambda i, h, n: (i, n, h)),
        compiler_params=_cparams(("parallel", "parallel", "arbitrary")),
        name="stickbreak",
    )(u_sb, u_sb, u_sb)


def _post_kernel(x_ref, yml_ref, d0_ref, d1_ref, d2_ref, l0_ref, l1_ref, l2_ref, ysb_ref,
                 mod_ref, gh_ref, wout_ref, wr_hi_ref, wr_lo_ref, rb_ref,
                 xo_ref, h_ref, comb_ref):
    mod = mod_ref[...]
    l0, l1, l2 = l0_ref[...], l1_ref[...], l2_ref[...]
    mx = jnp.maximum(jnp.maximum(l0, l1), l2)
    e0, e1, e2 = jnp.exp(l0 - mx), jnp.exp(l1 - mx), jnp.exp(l2 - mx)
    ydil = (e0 * d0_ref[...].astype(f32) + e1 * d1_ref[...].astype(f32) + e2 * d2_ref[...].astype(f32)) / (e0 + e1 + e2)
    y = jnp.concatenate([yml_ref[...], ydil, ysb_ref[...].astype(f32)], axis=1)
    pair = ((_iota((LANES, LANES), 0) // HEAD_DIM) == (_iota((LANES, LANES), 1) // HEAD_DIM)).astype(bf16)
    cols = []
    for cblk in range(y.shape[1] // LANES):
        yb = y[:, cblk * LANES:(cblk + 1) * LANES]
        hi, lo = _split_bf16(yb * yb)
        ms = (_dot(hi, pair) + _dot(lo, pair)) * (1.0 / HEAD_DIM)
        cols.append(yb * lax.rsqrt(ms + EPS))
    yn = (jnp.concatenate(cols, axis=1) * gh_ref[...]).astype(bf16)
    xn = x_ref[...] + mod[2:3, :] * _dot(yn, wout_ref[...])
    xo_ref[...] = xn
    h2 = _rms(xn) * (1.0 + mod[4:5, :]) + mod[3:4, :]
    hh, hl = _split_bf16(h2)
    h_ref[...] = hh
    logits = _dot(hh, wr_hi_ref[...]) + _dot(hl, wr_hi_ref[...]) + _dot(hh, wr_lo_ref[...])
    lane = _iota((1, LANES), 1)
    glane = lane < N_GROUPS
    sc = [lax.logistic(logits[:, j * LANES:(j + 1) * LANES]) for j in range(EPG)]
    bs = [jnp.where(glane, sc[j] + rb_ref[:, j * LANES:(j + 1) * LANES], -jnp.inf) for j in range(EPG)]
    hi01, lo01 = jnp.maximum(bs[0], bs[1]), jnp.minimum(bs[0], bs[1])
    hi23, lo23 = jnp.maximum(bs[2], bs[3]), jnp.minimum(bs[2], bs[3])
    top1 = jnp.maximum(hi01, hi23)
    top2 = jnp.maximum(jnp.minimum(hi01, hi23), jnp.maximum(lo01, lo23))
    gscore = jnp.where(glane, top1 + top2, -jnp.inf)
    gmax = jnp.max(gscore, axis=-1, keepdims=True)
    gidx = jnp.min(jnp.where(gscore == gmax, lane, LANES), axis=-1, keepdims=True)
    gsel = lane == gidx
    chosen = []
    for j in range(EPG):
        rank = jnp.zeros(bs[j].shape, jnp.int32)
        for jj in range(EPG):
            if jj != j:
                ahead = (bs[jj] > bs[j]) | ((bs[jj] == bs[j]) & (jj < j))
                rank = rank + ahead.astype(jnp.int32)
        chosen.append(jnp.where(gsel & (rank < 2), sc[j], 0.0))
    tot = jnp.sum(chosen[0] + chosen[1] + chosen[2] + chosen[3], axis=-1, keepdims=True)
    for j in range(EPG):
        comb_ref[:, j * LANES:(j + 1) * LANES] = chosen[j] / tot


def _post(x, y_ml, dil, y_sb, mod, g_head, w_out, wr_hi, wr_lo, rb, tm):
    b, s, d = x.shape
    (d0, l0), (d1, l1), (d2, l2) = dil
    tok = lambda w: pl.BlockSpec((None, tm, w), lambda i, j: (i, j, 0))
    full = lambda a: pl.BlockSpec(a.shape, lambda i, j: (0,) * a.ndim)
    return pl.pallas_call(
        _post_kernel,
        out_shape=(jax.ShapeDtypeStruct((b, s, d), f32),
                   jax.ShapeDtypeStruct((b, s, d), bf16),
                   jax.ShapeDtypeStruct((b, s, EPG * LANES), f32)),
        grid=(b, s // tm),
        in_specs=[tok(d), tok(D_ML), tok(D_DIL), tok(D_DIL), tok(D_DIL), tok(D_DIL), tok(D_DIL), tok(D_DIL),
                  tok(D_SB), pl.BlockSpec((None, 6, d), lambda i, j: (i, 0, 0)),
                  full(g_head), full(w_out), full(wr_hi), full(wr_lo), full(rb)],
        out_specs=(tok(d), tok(d), tok(EPG * LANES)),
        compiler_params=_cparams(("parallel", "parallel")),
        name="merge_outproj_router",
    )(x, y_ml, d0, d1, d2, l0, l1, l2, y_sb, mod, g_head, w_out, wr_hi, wr_lo, rb)


def _moe_kernel(x_ref, h_ref, comb_ref, mod_ref, wg_ref, wu_ref, wd_ref, gf_ref, o_ref, acc_ref, *, final):
    e = pl.program_id(2)

    @pl.when(e == 0)
    def _():
        acc_ref[...] = jnp.zeros_like(acc_ref)

    h = h_ref[...]
    comb = comb_ref[...]
    lane = _iota((1, comb.shape[1]), 1)
    ce = jnp.sum(jnp.where(lane == (e // N_GROUPS) * LANES + e % N_GROUPS, comb, 0.0), axis=-1, keepdims=True)
    g = _dot(h, wg_ref[...])
    he = g * lax.logistic(g) * _dot(h, wu_ref[...]) * ce
    acc_ref[...] += _dot(he.astype(bf16), wd_ref[...])

    @pl.when(e == N_EXPERTS - 1)
    def _():
        xn = x_ref[...] + mod_ref[...][5:6, :] * acc_ref[...]
        if final:
            xn = _rms(xn) * gf_ref[...]
        o_ref[...] = xn


def _moe(x, h, comb, mod, wg, wu, wd, g_final, tm, final):
    b, s, d = x.shape
    de = wg.shape[2]
    expert = lambda i, j, e: ((e % N_GROUPS) * EPG + e // N_GROUPS, 0, 0)
    return pl.pallas_call(
        functools.partial(_moe_kernel, final=final),
        out_shape=jax.ShapeDtypeStruct((b, s, d), f32),
        grid=(b, s // tm, N_EXPERTS),
        in_specs=[pl.BlockSpec((None, tm, d), lambda i, j, e: (i, j, 0)),
                  pl.BlockSpec((None, tm, d), lambda i, j, e: (i, j, 0)),
                  pl.BlockSpec((None, tm, EPG * LANES), lambda i, j, e: (i, j, 0)),
                  pl.BlockSpec((None, 6, d), lambda i, j, e: (i, 0, 0)),
                  pl.BlockSpec((None, d, de), expert),
                  pl.BlockSpec((None, d, de), expert),
                  pl.BlockSpec((None, de, d), expert),
                  pl.BlockSpec((1, d), lambda i, j, e: (0, 0))],
        out_specs=pl.BlockSpec((None, tm, d), lambda i, j, e: (i, j, 0)),
        scratch_shapes=[pltpu.VMEM((tm, d), f32)],
        compiler_params=_cparams(("parallel", "parallel", "arbitrary")),
        name="moe_final" if final else "moe",
    )(x, h, comb, mod, wg, wu, wd, g_final)


def _block_diag(w):
    n, a, c = w.shape
    out = jnp.zeros((n * a, n * c), w.dtype)
    for i in range(n):
        out = out.at[i * a:(i + 1) * a, i * c:(i + 1) * c].set(w[i])
    return out


def kernel(x, c, w_in, conv_w, conv_b, w_mq, w_mk, w_mv, gate_bias, g_head, w_out, w_ada, b_ada,
           w_router, router_bias, w_gate_e, w_up_e, w_down_e, g_final):
    b, s, d = x.shape
    depth = w_in.shape[0]
    qscale = HEAD_DIM ** -0.5
    tm = min(512, s)

    mod_all = _ada(c, w_ada, b_ada).reshape(depth, b, 6, d)

    wr = jnp.zeros((d, EPG, LANES), f32).at[:, :, :N_GROUPS].set(
        w_router.reshape(d, N_GROUPS, EPG).transpose(0, 2, 1)).reshape(d, EPG * LANES)
    wr_hi = wr.astype(bf16)
    wr_lo = (wr - wr_hi.astype(f32)).astype(bf16)
    rb = jnp.zeros((1, EPG, LANES), f32).at[0, :, :N_GROUPS].set(
        router_bias.reshape(N_GROUPS, EPG).T).reshape(1, EPG * LANES)

    slope = jnp.exp2(-ALIBI_MAX_BIAS * jnp.arange(1, N_DIL + 1, dtype=f32) / N_DIL)
    g_final2 = g_final.reshape(1, d)

    off_o = D_ML
    off_i = 2 * D_ML
    off_f = off_i + N_ML
    off_dil = off_f + N_ML
    off_sb = off_dil + 3 * D_DIL

    for l in range(depth):
        w = w_in[l]
        w_ml = jnp.concatenate([w[:, :off_i], w[:, off_i:off_dil],
                                jnp.zeros((d, LANES - 2 * N_ML), f32)], axis=1).astype(bf16)
        w_gt = w[:, off_i:off_dil].T.astype(bf16)
        wdil = w[:, off_dil:off_sb]
        w_dil = jnp.concatenate([wdil[:, :D_DIL] * qscale, wdil[:, D_DIL:]], axis=1).astype(bf16)
        wsb = w[:, off_sb:]
        w_sb = jnp.concatenate([wsb[:, :D_SB] * qscale, wsb[:, D_SB:]], axis=1).astype(bf16)
        mod = mod_all[l]

        u_ml, u_dil, u_sb, g_row = _inproj(x, mod, w_ml, w_dil, w_sb, w_gt, tm)

        gb = gate_bias[l]
        gb_col = jnp.zeros((1, LANES), f32).at[0, :2 * N_ML].set(gb.reshape(-1))
        gb_row = gb.reshape(2 * N_ML, 1)
        y_ml = _mlstm(u_ml, g_row, conv_w[l], conv_b[l].reshape(1, D_ML),
                      _block_diag(w_mq[l]).astype(bf16), (_block_diag(w_mk[l]) * qscale).astype(bf16),
                      _block_diag(w_mv[l]).astype(bf16), gb_col, gb_row)

        dil = []
        for _, dilation in DIL_PATTERNS:
            slopes_d = jnp.broadcast_to((slope * dilation)[:, None, None], (N_DIL, 8, LANES))
            dil.append(_dilated(u_dil, slopes_d, dilation))

        y_sb = _stickbreak(u_sb)

        x, h2, comb = _post(x, y_ml, dil, y_sb, mod, g_head[l].reshape(1, d), w_out[l].astype(bf16),
                            wr_hi, wr_lo, rb, tm)
        x = _moe(x, h2, comb, mod, w_gate_e[l].astype(bf16), w_up_e[l].astype(bf16), w_down_e[l].astype(bf16),
                 g_final2, min(1024, s), final=(l == depth - 1))
    return x
```

```python
import functools

import jax
import jax.numpy as jnp
from jax import lax
from jax.experimental import pallas as pl
from jax.experimental.pallas import tpu as pltpu

HEAD_DIM = 64
N_ML = 4
N_DIL = 6
N_SB = 6
D_ML = N_ML * HEAD_DIM
D_DIL = N_DIL * HEAD_DIM
D_SB = N_SB * HEAD_DIM
CONV_WIDTH = 4
CHUNK = 128
DILATIONS = (1, 4, 16)
DIL_SPAN = 128
ALIBI_MAX_BIAS = 8.0
N_EXPERTS = 16
N_GROUPS = 4
EPG = N_EXPERTS // N_GROUPS
EPS = 1e-6
LANES = 128
ML_COLS = 2 * D_ML + LANES
SB_DEAD_LOG = -104.0
SB_NO_KEYS = -1e30
SB_QBLOCKS = 4
SB_GROUP = 2
DIL_UNITS = 4
VMEM_LIMIT = 56 * 1024 * 1024

_NT = (((1,), (1,)), ((), ()))
_TN = (((0,), (0,)), ((), ()))

f32 = jnp.float32
bf16 = jnp.bfloat16


def _dot(a, b):
    return jnp.dot(a, b, preferred_element_type=f32)


def _dot_nt(a, b):
    return lax.dot_general(a, b, _NT, preferred_element_type=f32)


def _split_bf16(x):
    hi = x.astype(bf16)
    lo = (x - hi.astype(f32)).astype(bf16)
    return hi, lo


def _iota(shape, dim):
    return lax.broadcasted_iota(jnp.int32, shape, dim)


def _rms(x):
    return x * lax.rsqrt(jnp.mean(x * x, axis=-1, keepdims=True) + EPS)


def _cparams(sem):
    return pltpu.CompilerParams(dimension_semantics=sem, vmem_limit_bytes=VMEM_LIMIT)


def _ada_kernel(c_ref, w_ref, b_ref, o_ref):
    c = c_ref[...]
    c_act = (c * lax.logistic(c)).astype(bf16)
    o_ref[...] = _dot(c_act, w_ref[...].astype(bf16)) + b_ref[...]


def _ada(c, w_ada, b_ada):
    depth, d, nd = w_ada.shape
    b = c.shape[0]
    nblk = nd // d
    return pl.pallas_call(
        _ada_kernel,
        out_shape=jax.ShapeDtypeStruct((depth, b, nd), f32),
        grid=(depth, nblk),
        in_specs=[pl.BlockSpec((b, d), lambda l, j: (0, 0)),
                  pl.BlockSpec((None, d, d), lambda l, j: (l, 0, j)),
                  pl.BlockSpec((None, 1, d), lambda l, j: (l, 0, j))],
        out_specs=pl.BlockSpec((None, b, d), lambda l, j: (l, 0, j)),
        compiler_params=_cparams(("parallel", "parallel")),
        name="ada_mod",
    )(c, w_ada, b_ada.reshape(depth, 1, nd))


def _inproj_kernel(x_ref, mod_ref, wml_ref, wdil_ref, wsb_ref, wgt_ref,
                   uml_ref, udil_ref, usb_ref, grow_ref):
    x = x_ref[...]
    mod = mod_ref[...]
    h = (_rms(x) * (1.0 + mod[1:2, :]) + mod[0:1, :]).astype(bf16)
    uml_ref[...] = _dot(h, wml_ref[...])
    udil_ref[...] = _dot(h, wdil_ref[...])
    usb_ref[...] = _dot(h, wsb_ref[...]).astype(bf16)
    grow_ref[...] = _dot_nt(wgt_ref[...], h)


def _inproj(x, mod, w_ml, w_dil, w_sb, w_gt, tm):
    b, s, d = x.shape
    full = lambda a: pl.BlockSpec(a.shape, lambda i, j: (0,) * a.ndim)
    return pl.pallas_call(
        _inproj_kernel,
        out_shape=(jax.ShapeDtypeStruct((b, s, ML_COLS), f32),
                   jax.ShapeDtypeStruct((b, s, 3 * D_DIL), f32),
                   jax.ShapeDtypeStruct((b, s, 3 * D_SB), bf16),
                   jax.ShapeDtypeStruct((b, 2 * N_ML, s), f32)),
        grid=(b, s // tm),
        in_specs=[pl.BlockSpec((None, tm, d), lambda i, j: (i, j, 0)),
                  pl.BlockSpec((None, 6, d), lambda i, j: (i, 0, 0)),
                  full(w_ml), full(w_dil), full(w_sb), full(w_gt)],
        out_specs=(pl.BlockSpec((None, tm, ML_COLS), lambda i, j: (i, j, 0)),
                   pl.BlockSpec((None, tm, 3 * D_DIL), lambda i, j: (i, j, 0)),
                   pl.BlockSpec((None, tm, 3 * D_SB), lambda i, j: (i, j, 0)),
                   pl.BlockSpec((None, 2 * N_ML, tm), lambda i, j: (i, 0, j))),
        compiler_params=_cparams(("parallel", "parallel")),
        name="norm_inproj",
    )(x, mod, w_ml, w_dil, w_sb, w_gt)


def _mlstm_kernel(u_ref, grow_ref, convw_ref, convb_ref, wq_ref, wk_ref, wv_ref, gbc_ref, gbr_ref,
                  y_ref, prev_ref, c_ref, n_ref, m_ref):
    L = CHUNK

    @pl.when(pl.program_id(1) == 0)
    def _():
        prev_ref[...] = jnp.zeros_like(prev_ref)
        c_ref[...] = jnp.zeros_like(c_ref)
        n_ref[...] = jnp.zeros_like(n_ref)
        m_ref[...] = jnp.zeros_like(m_ref)

    u = u_ref[...]
    xm = u[:, :D_ML]
    og = u[:, D_ML:2 * D_ML]
    gcol = u[:, 2 * D_ML:] + gbc_ref[...]
    grow = grow_ref[...] + gbr_ref[...]

    prev = prev_ref[...]
    row = _iota((L, D_ML), 0)
    convw = convw_ref[...]
    xc = xm * convw[CONV_WIDTH - 1:CONV_WIDTH, :]
    for sft in range(1, CONV_WIDTH):
        shifted = jnp.where(row < sft, pltpu.roll(prev, sft, 0), pltpu.roll(xm, sft, 0))
        xc = xc + shifted * convw[CONV_WIDTH - 1 - sft:CONV_WIDTH - sft, :]
    prev_ref[...] = xm
    xc = xc + convb_ref[...]
    xc = xc * lax.logistic(xc)
    xcb = xc.astype(bf16)
    q = _dot(xcb, wq_ref[...])
    k = _dot(xcb, wk_ref[...])
    v = _dot(xm.astype(bf16), wv_ref[...])
    kb = k.astype(bf16)
    vb = v.astype(bf16)

    def logsig(t):
        return jnp.minimum(t, 0.0) - jnp.log1p(jnp.exp(-jnp.abs(t)))

    lf_col = logsig(gcol)
    lf_row = logsig(grow)
    ltri = (_iota((L, L), 1) <= _iota((L, L), 0)).astype(bf16)
    utri = (_iota((L, L), 0) <= _iota((L, L), 1)).astype(bf16)
    ch, cl = _split_bf16(lf_col)
    cum_col = _dot(ltri, ch) + _dot(ltri, cl)
    rh, rl = _split_bf16(lf_row)
    cum_row = _dot(rh, utri) + _dot(rl, utri)

    causal = _iota((L, L), 1) <= _iota((L, L), 0)
    lane = _iota((1, D_ML), 1)
    c_bd = c_ref[...]
    n_st = n_ref[...]
    m_st = m_ref[...]
    c_bdb = c_bd.astype(bf16)

    heads = range(N_ML)
    mhs = [(lane >= hd * HEAD_DIM) & (lane < (hd + 1) * HEAD_DIM) for hd in heads]
    qhs = [jnp.where(mhs[hd], q, 0.0) for hd in heads]
    qhb = [qh.astype(bf16) for qh in qhs]
    qk = [_dot_nt(qhb[hd], kb) for hd in heads]
    qc = [_dot(qhb[hd], c_bdb) for hd in heads]
    m_ts, inters, sws = [], [], []
    wfull = jnp.zeros((L, D_ML), f32)
    decay_l = jnp.zeros((1, D_ML), f32)
    mnew_l = jnp.zeros((1, D_ML), f32)
    for hd in heads:
        cumc = cum_col[:, N_ML + hd:N_ML + hd + 1]
        cumr = cum_row[N_ML + hd:N_ML + hd + 1, :]
        igr = grow[hd:hd + 1, :]
        igc = gcol[:, hd:hd + 1]
        m_prev = m_st[:, hd * HEAD_DIM:hd * HEAD_DIM + 1]
        log_d = jnp.where(causal, cumc - cumr + igr, -jnp.inf)
        log_inter = cumc + m_prev
        m_t = jnp.maximum(log_inter, jnp.max(log_d, axis=-1, keepdims=True))
        m_ts.append(m_t)
        inters.append(jnp.exp(log_inter - m_t))
        sws.append(qk[hd] * jnp.exp(log_d - m_t))
        last = cumc[L - 1:L, :]
        log_w = last - cumc + igc
        m_new = jnp.maximum(last + m_prev, jnp.max(log_w, axis=0, keepdims=True))
        wfull = jnp.where(mhs[hd], jnp.exp(log_w - m_new), wfull)
        decay_l = jnp.where(mhs[hd], jnp.exp(last + m_prev - m_new), decay_l)
        mnew_l = jnp.where(mhs[hd], m_new, mnew_l)
    swv = [_dot(sws[hd].astype(bf16), vb) for hd in heads]
    hmix = jnp.zeros((L, D_ML), f32)
    for hd in heads:
        num = inters[hd] * qc[hd] + swv[hd]
        den = (inters[hd] * jnp.sum(qhs[hd] * n_st, axis=-1, keepdims=True)
               + jnp.sum(sws[hd], axis=-1, keepdims=True))
        hout = num / jnp.maximum(jnp.abs(den), jnp.exp(-m_ts[hd]))
        hmix = jnp.where(mhs[hd], hout, hmix)

    kw = k * wfull
    upd = lax.dot_general(kw.astype(bf16), vb, _TN, preferred_element_type=f32)
    same_head = (_iota((D_ML, D_ML), 0) // HEAD_DIM) == (_iota((D_ML, D_ML), 1) // HEAD_DIM)
    c_ref[...] = decay_l * c_bd + jnp.where(same_head, upd, 0.0)
    n_ref[...] = decay_l * n_st + jnp.sum(kw, axis=0, keepdims=True)
    m_ref[...] = mnew_l
    y_ref[...] = lax.logistic(og) * hmix


def _mlstm(u_ml, g_row, conv_w, conv_b, wq_bd, wk_bd, wv_bd, gb_col, gb_row):
    b, s, _ = u_ml.shape
    full = lambda a: pl.BlockSpec(a.shape, lambda i, j: (0,) * a.ndim)
    return pl.pallas_call(
        _mlstm_kernel,
        out_shape=jax.ShapeDtypeStruct((b, s, D_ML), f32),
        grid=(b, s // CHUNK),
        in_specs=[pl.BlockSpec((None, CHUNK, ML_COLS), lambda i, j: (i, j, 0)),
                  pl.BlockSpec((None, 2 * N_ML, CHUNK), lambda i, j: (i, 0, j)),
                  full(conv_w), full(conv_b), full(wq_bd), full(wk_bd), full(wv_bd), full(gb_col), full(gb_row)],
        out_specs=pl.BlockSpec((None, CHUNK, D_ML), lambda i, j: (i, j, 0)),
        scratch_shapes=[pltpu.VMEM((CHUNK, D_ML), f32), pltpu.VMEM((D_ML, D_ML), f32),
                        pltpu.VMEM((1, D_ML), f32), pltpu.VMEM((1, D_ML), f32)],
        compiler_params=_cparams(("parallel", "arbitrary")),
        name="mlstm",
    )(u_ml, g_row, conv_w, conv_b, wq_bd, wk_bd, wv_bd, gb_col, gb_row)


def _rows(start, size, stride):
    return pl.ds(start, size) if stride == 1 else pl.ds(start, size, stride=stride)


def _dil_kernel(q_ref, k_ref, v_ref, slope_ref, o_ref, *scratch, seq, tile):
    npat = len(DILATIONS)
    kd = scratch[0:npat]
    vd = scratch[npat:2 * npat]
    op = scratch[2 * npat:3 * npat]
    ls = scratch[3 * npat:4 * npat]
    bias_ref = scratch[4 * npat]
    L = CHUNK
    ti = pl.program_id(2)
    lane = _iota((1, LANES), 1)
    m0 = lane < HEAD_DIM

    @pl.when(ti == 0)
    def _():
        row = _iota((L, L), 0)
        col = _iota((L, L), 1)
        for p, d in enumerate(DILATIONS):
            per = seq // d
            for r in range(d):
                kd[p][r * per:(r + 1) * per, :] = k_ref[_rows(r, per, d), :].astype(bf16)
                vd[p][r * per:(r + 1) * per, :] = v_ref[_rows(r, per, d), :].astype(bf16)
            for j in range(2):
                sl = slope_ref[j][0:1, :] * float(d)
                bias_ref[p, j, 0] = jnp.where(col >= row, -(sl * (row - col + DIL_SPAN).astype(f32)), -jnp.inf)
                bias_ref[p, j, 1] = jnp.full((L, L), -jnp.inf, f32)
                bias_ref[p, j, 2] = jnp.where(col <= row, -(sl * (row - col).astype(f32)), -jnp.inf)

    nunits = tile // L
    for p, d in enumerate(DILATIONS):
        per = seq // d
        blocks_per_res = tile // d // L

        def body(it, carry, p=p, d=d, per=per, blocks_per_res=blocks_per_res):
            units = []
            for c in range(DIL_UNITS):
                u = it * DIL_UNITS + c
                r = u // blocks_per_res
                blk = u % blocks_per_res
                start = blk * (L * d) + r
                q = q_ref[_rows(start, L, d), :].astype(bf16)
                m_first = (ti * tile) // d + blk * L
                base = r * per + m_first
                has_prev = m_first > 0
                cur0 = pl.multiple_of(base, L)
                prev0 = pl.multiple_of(jnp.where(has_prev, base - L, base), L)
                units.append(dict(start=start, q=q, pk=jnp.where(has_prev, 0, 1),
                                  kc=kd[p][pl.ds(cur0, L), :], vc=vd[p][pl.ds(cur0, L), :],
                                  kp=kd[p][pl.ds(prev0, L), :], vp=vd[p][pl.ds(prev0, L), :]))
            chains = [(un, j) for un in units for j in range(2)]
            scores = []
            for un, j in chains:
                qj = jnp.where(m0 if j == 0 else jnp.logical_not(m0), un["q"], jnp.zeros_like(un["q"]))
                scores.append((_dot_nt(qj, un["kp"]) + bias_ref[p, j, un["pk"]],
                               _dot_nt(qj, un["kc"]) + bias_ref[p, j, 2]))
            probs = []
            for sp, sc in scores:
                mx = jnp.max(jnp.maximum(sp, sc), axis=-1, keepdims=True)
                pp = jnp.exp(sp - mx)
                pc = jnp.exp(sc - mx)
                den = jnp.sum(pp + pc, axis=-1, keepdims=True)
                probs.append((pp.astype(bf16), pc.astype(bf16), den, mx + jnp.log(den)))
            outs = [(_dot(pp, un["vp"]) + _dot(pc, un["vc"])) / den
                    for (un, j), (pp, pc, den, _) in zip(chains, probs)]
            for c, un in enumerate(units):
                op[p][_rows(un["start"], L, d), :] = jnp.where(m0, outs[2 * c], outs[2 * c + 1])
                ls[p][_rows(un["start"], L, d), :] = jnp.where(m0, probs[2 * c][3], probs[2 * c + 1][3])
            return carry

        lax.fori_loop(0, nunits // DIL_UNITS, body, 0)

    l0, l1, l2 = ls[0][...], ls[1][...], ls[2][...]
    mx = jnp.maximum(jnp.maximum(l0, l1), l2)
    e0, e1, e2 = jnp.exp(l0 - mx), jnp.exp(l1 - mx), jnp.exp(l2 - mx)
    o_ref[...] = ((e0 * op[0][...] + e1 * op[1][...] + e2 * op[2][...]) / (e0 + e1 + e2)).astype(o_ref.dtype)


def _dilated(u_dil, slopes):
    b, s, _ = u_dil.shape
    npair = N_DIL // 2
    tile = DILATIONS[-1] * CHUNK
    assert s % tile == 0 and (tile // CHUNK) % DIL_UNITS == 0
    seq_buf = lambda dt: pltpu.VMEM((s, LANES), dt)
    tile_buf = pltpu.VMEM((tile, LANES), f32)
    npat = len(DILATIONS)
    return pl.pallas_call(
        functools.partial(_dil_kernel, seq=s, tile=tile),
        out_shape=jax.ShapeDtypeStruct((b, s, D_DIL), bf16),
        grid=(b, npair, s // tile),
        in_specs=[pl.BlockSpec((None, tile, LANES), lambda i, h, n: (i, n, h)),
                  pl.BlockSpec((None, s, LANES), lambda i, h, n: (i, 0, npair + h)),
                  pl.BlockSpec((None, s, LANES), lambda i, h, n: (i, 0, 2 * npair + h)),
                  pl.BlockSpec((None, 2, 8, LANES), lambda i, h, n: (h, 0, 0, 0))],
        out_specs=pl.BlockSpec((None, tile, LANES), lambda i, h, n: (i, n, h)),
        scratch_shapes=([seq_buf(bf16)] * (2 * npat) + [tile_buf] * (2 * npat)
                        + [pltpu.VMEM((npat, 2, 3, CHUNK, CHUNK), f32)]),
        compiler_params=_cparams(("parallel", "parallel", "arbitrary")),
        name="dilated",
    )(u_dil, u_dil, u_dil, slopes)


def _sb_kernel(q_ref, k_ref, v_ref, o_ref, qm_ref, r_ref, acc_ref, *, nq):
    L = CHUNK
    tile0 = pl.program_id(1) * nq
    row = _iota((L, L), 0)
    col = _iota((L, L), 1)
    strict = col < row
    lane = _iota((1, LANES), 1)
    m0 = lane < HEAD_DIM
    npair = N_SB // 2
    csum = jnp.concatenate([(_iota((L, L), 0) > _iota((L, L), 1)).astype(bf16), jnp.ones((L, L), bf16)], axis=1)

    for qb in range(nq):
        for hp in range(npair):
            q = q_ref[qb * L:(qb + 1) * L, hp * LANES:(hp + 1) * LANES]
            qm_ref[(qb * npair + hp) * 2] = jnp.where(m0, q, jnp.zeros_like(q))
            qm_ref[(qb * npair + hp) * 2 + 1] = jnp.where(m0, jnp.zeros_like(q), q)

    def sweep(it, diagonal):
        rmax = None
        for g0 in range(0, nq, SB_GROUP):
            chains = []
            for qb in range(g0, g0 + SB_GROUP):
                qi = tile0 + qb
                k0 = pl.multiple_of(jnp.maximum(qi - it, 0) * L, L)
                out_of_keys = jnp.where(qi - it <= 0, SB_NO_KEYS, 0.0)
                for hp in range(npair):
                    kblk = k_ref[pl.ds(k0, L), hp * LANES:(hp + 1) * LANES]
                    vblk = v_ref[pl.ds(k0, L), hp * LANES:(hp + 1) * LANES]
                    for j in range(2):
                        chains.append(((qb * npair + hp) * 2 + j, kblk, vblk, out_of_keys))
            zs = [_dot_nt(qm_ref[c], kblk) for c, kblk, _, _ in chains]
            zl, lkb = [], []
            for z in zs:
                lk = -(jnp.maximum(z, 0.0) + jnp.log(1.0 + jnp.exp(-jnp.abs(z))))
                zl.append(z + lk)
                lkb.append((jnp.where(strict, lk, 0.0) if diagonal else lk).astype(bf16))
            ts = [_dot(x, csum) for x in lkb]
            avs = []
            for (c, _, _, out_of_keys), s, t in zip(chains, zl, ts):
                if diagonal:
                    a = jnp.where(strict, jnp.exp(s + t[:, :L]), 0.0)
                    rs = t[:, L:] + out_of_keys
                else:
                    a = jnp.exp(s + r_ref[c] + t[:, :L])
                    rs = r_ref[c] + t[:, L:] + out_of_keys
                avs.append(a.astype(bf16))
                r_ref[c] = rs
                rmax = rs if rmax is None else jnp.maximum(rmax, rs)
            for (c, _, vblk, _), a in zip(chains, avs):
                if diagonal:
                    acc_ref[c] = _dot(a, vblk)
                else:
                    acc_ref[c] += _dot(a, vblk)
        return jnp.max(rmax) > SB_DEAD_LOG

    alive0 = sweep(0, True)

    def body(carry):
        _, it = carry
        return sweep(it, False), it + 1

    lax.while_loop(lambda carry: carry[0], body, (alive0, jnp.int32(1)))

    for qb in range(nq):
        for hp in range(npair):
            c = (qb * npair + hp) * 2
            o_ref[qb * L:(qb + 1) * L, hp * LANES:(hp + 1) * LANES] = (
                jnp.where(m0, acc_ref[c], acc_ref[c + 1]).astype(o_ref.dtype))


def _stickbreak(u_sb):
    b, s, _ = u_sb.shape
    nq = SB_QBLOCKS
    tq = nq * CHUNK
    nchain = nq * N_SB
    return pl.pallas_call(
        functools.partial(_sb_kernel, nq=nq),
        out_shape=jax.ShapeDtypeStruct((b, s, D_SB), bf16),
        grid=(b, s // tq),
        in_specs=[pl.BlockSpec((None, tq, D_SB), lambda i, n: (i, n, 0)),
                  pl.BlockSpec((None, s, D_SB), lambda i, n: (i, 0, 1)),
                  pl.BlockSpec((None, s, D_SB), lambda i, n: (i, 0, 2))],
        out_specs=pl.BlockSpec((None, tq, D_SB), lambda i, n: (i, n, 0)),
        scratch_shapes=[pltpu.VMEM((nchain, CHUNK, LANES), bf16),
                        pltpu.VMEM((nchain, CHUNK, CHUNK), f32),
                        pltpu.VMEM((nchain, CHUNK, LANES), f32)],
        compiler_params=_cparams(("parallel", "arbitrary")),
        name="stickbreak",
    )(u_sb, u_sb, u_sb)


def _post_kernel(x_ref, yml_ref, ydil_ref, ysb_ref, mod_ref, gh_ref, wout_ref, wr_hi_ref, wr_lo_ref, rb_ref,
                 xo_ref, h_ref, comb_ref):
    mod = mod_ref[...]
    y = jnp.concatenate([yml_ref[...], ydil_ref[...].astype(f32), ysb_ref[...].astype(f32)], axis=1)
    pair = ((_iota((LANES, LANES), 0) // HEAD_DIM) == (_iota((LANES, LANES), 1) // HEAD_DIM)).astype(bf16)
    cols = []
    for cblk in range(y.shape[1] // LANES):
        yb = y[:, cblk * LANES:(cblk + 1) * LANES]
        hi, lo = _split_bf16(yb * yb)
        ms = (_dot(hi, pair) + _dot(lo, pair)) * (1.0 / HEAD_DIM)
        cols.append(yb * lax.rsqrt(ms + EPS))
    yn = (jnp.concatenate(cols, axis=1) * gh_ref[...]).astype(bf16)
    xn = x_ref[...] + mod[2:3, :] * _dot(yn, wout_ref[...])
    xo_ref[...] = xn
    h2 = _rms(xn) * (1.0 + mod[4:5, :]) + mod[3:4, :]
    hh, hl = _split_bf16(h2)
    h_ref[...] = hh
    logits = _dot(hh, wr_hi_ref[...]) + _dot(hl, wr_hi_ref[...]) + _dot(hh, wr_lo_ref[...])
    lane = _iota((1, LANES), 1)
    glane = lane < N_GROUPS
    sc = [lax.logistic(logits[:, j * LANES:(j + 1) * LANES]) for j in range(EPG)]
    bs = [jnp.where(glane, sc[j] + rb_ref[:, j * LANES:(j + 1) * LANES], -jnp.inf) for j in range(EPG)]
    hi01, lo01 = jnp.maximum(bs[0], bs[1]), jnp.minimum(bs[0], bs[1])
    hi23, lo23 = jnp.maximum(bs[2], bs[3]), jnp.minimum(bs[2], bs[3])
    top1 = jnp.maximum(hi01, hi23)
    top2 = jnp.maximum(jnp.minimum(hi01, hi23), jnp.maximum(lo01, lo23))
    gscore = jnp.where(glane, top1 + top2, -jnp.inf)
    gmax = jnp.max(gscore, axis=-1, keepdims=True)
    gidx = jnp.min(jnp.where(gscore == gmax, lane, LANES), axis=-1, keepdims=True)
    gsel = lane == gidx
    chosen = []
    for j in range(EPG):
        rank = jnp.zeros(bs[j].shape, jnp.int32)
        for jj in range(EPG):
            if jj != j:
                ahead = (bs[jj] >= bs[j]) if jj < j else (bs[jj] > bs[j])
                rank = rank + ahead.astype(jnp.int32)
        chosen.append(jnp.where(gsel & (rank < 2), sc[j], 0.0))
    tot = jnp.sum(chosen[0] + chosen[1] + chosen[2] + chosen[3], axis=-1, keepdims=True)
    for j in range(EPG):
        comb_ref[:, j * LANES:(j + 1) * LANES] = chosen[j] / tot


def _post(x, y_ml, y_dil, y_sb, mod, g_head, w_out, wr_hi, wr_lo, rb, tm):
    b, s, d = x.shape
    tok = lambda w: pl.BlockSpec((None, tm, w), lambda i, j: (i, j, 0))
    full = lambda a: pl.BlockSpec(a.shape, lambda i, j: (0,) * a.ndim)
    return pl.pallas_call(
        _post_kernel,
        out_shape=(jax.ShapeDtypeStruct((b, s, d), f32),
                   jax.ShapeDtypeStruct((b, s, d), bf16),
                   jax.ShapeDtypeStruct((b, s, EPG * LANES), f32)),
        grid=(b, s // tm),
        in_specs=[tok(d), tok(D_ML), tok(D_DIL), tok(D_SB), pl.BlockSpec((None, 6, d), lambda i, j: (i, 0, 0)),
                  full(g_head), full(w_out), full(wr_hi), full(wr_lo), full(rb)],
        out_specs=(tok(d), tok(d), tok(EPG * LANES)),
        compiler_params=_cparams(("parallel", "parallel")),
        name="merge_outproj_router",
    )(x, y_ml, y_dil, y_sb, mod, g_head, w_out, wr_hi, wr_lo, rb)


def _moe_kernel(x_ref, h_ref, comb_ref, mod_ref, wg_ref, wu_ref, wd_ref, gf_ref, o_ref, acc_ref, *, final):
    e = pl.program_id(2)

    @pl.when(e == 0)
    def _():
        acc_ref[...] = jnp.zeros_like(acc_ref)

    h = h_ref[...]
    comb = comb_ref[...]
    lane = _iota((1, comb.shape[1]), 1)
    ce = jnp.sum(jnp.where(lane == (e // N_GROUPS) * LANES + e % N_GROUPS, comb, 0.0), axis=-1, keepdims=True)
    g = _dot(h, wg_ref[...])
    he = g * lax.logistic(g) * _dot(h, wu_ref[...]) * ce
    acc_ref[...] += _dot(he.astype(bf16), wd_ref[...])

    @pl.when(e == N_EXPERTS - 1)
    def _():
        xn = x_ref[...] + mod_ref[...][5:6, :] * acc_ref[...]
        if final:
            xn = _rms(xn) * gf_ref[...]
        o_ref[...] = xn


def _moe(x, h, comb, mod, wg, wu, wd, g_final, tm, final):
    b, s, d = x.shape
    de = wg.shape[2]
    expert = lambda i, j, e: ((e % N_GROUPS) * EPG + e // N_GROUPS, 0, 0)
    return pl.pallas_call(
        functools.partial(_moe_kernel, final=final),
        out_shape=jax.ShapeDtypeStruct((b, s, d), f32),
        grid=(b, s // tm, N_EXPERTS),
        in_specs=[pl.BlockSpec((None, tm, d), lambda i, j, e: (i, j, 0)),
                  pl.BlockSpec((None, tm, d), lambda i, j, e: (i, j, 0)),
                  pl.BlockSpec((None, tm, EPG * LANES), lambda i, j, e: (i, j, 0)),
                  pl.BlockSpec((None, 6, d), lambda i, j, e: (i, 0, 0)),
                  pl.BlockSpec((None, d, de), expert),
                  pl.BlockSpec((None, d, de), expert),
                  pl.BlockSpec((None, de, d), expert),
                  pl.BlockSpec((1, d), lambda i, j, e: (0, 0))],
        out_specs=pl.BlockSpec((None, tm, d), lambda i, j, e: (i, j, 0)),
        scratch_shapes=[pltpu.VMEM((tm, d), f32)],
        compiler_params=_cparams(("parallel", "parallel", "arbitrary")),
        name="moe_final" if final else "moe",
    )(x, h, comb, mod, wg, wu, wd, g_final)


def _block_diag(w):
    n, a, c = w.shape
    out = jnp.zeros((n * a, n * c), w.dtype)
    for i in range(n):
        out = out.at[i * a:(i + 1) * a, i * c:(i + 1) * c].set(w[i])
    return out


def kernel(x, c, w_in, conv_w, conv_b, w_mq, w_mk, w_mv, gate_bias, g_head, w_out, w_ada, b_ada,
           w_router, router_bias, w_gate_e, w_up_e, w_down_e, g_final):
    b, s, d = x.shape
    depth = w_in.shape[0]
    qscale = HEAD_DIM ** -0.5
    tm = min(512, s)

    mod_all = _ada(c, w_ada, b_ada).reshape(depth, b, 6, d)

    wr = jnp.zeros((d, EPG, LANES), f32).at[:, :, :N_GROUPS].set(
        w_router.reshape(d, N_GROUPS, EPG).transpose(0, 2, 1)).reshape(d, EPG * LANES)
    wr_hi = wr.astype(bf16)
    wr_lo = (wr - wr_hi.astype(f32)).astype(bf16)
    rb = jnp.zeros((1, EPG, LANES), f32).at[0, :, :N_GROUPS].set(
        router_bias.reshape(N_GROUPS, EPG).T).reshape(1, EPG * LANES)

    slope = jnp.exp2(-ALIBI_MAX_BIAS * jnp.arange(1, N_DIL + 1, dtype=f32) / N_DIL)
    slopes = jnp.broadcast_to(slope.reshape(N_DIL // 2, 2, 1, 1), (N_DIL // 2, 2, 8, LANES))
    g_final2 = g_final.reshape(1, d)

    off_i = 2 * D_ML
    off_dil = off_i + 2 * N_ML
    off_sb = off_dil + 3 * D_DIL

    for l in range(depth):
        w = w_in[l]
        w_ml = jnp.concatenate([w[:, :off_i], w[:, off_i:off_dil],
                                jnp.zeros((d, LANES - 2 * N_ML), f32)], axis=1).astype(bf16)
        w_gt = w[:, off_i:off_dil].T.astype(bf16)
        wdil = w[:, off_dil:off_sb]
        w_dil = jnp.concatenate([wdil[:, :D_DIL] * qscale, wdil[:, D_DIL:]], axis=1).astype(bf16)
        wsb = w[:, off_sb:]
        w_sb = jnp.concatenate([wsb[:, :D_SB] * qscale, wsb[:, D_SB:]], axis=1).astype(bf16)
        mod = mod_all[l]

        u_ml, u_dil, u_sb, g_row = _inproj(x, mod, w_ml, w_dil, w_sb, w_gt, tm)

        gb = gate_bias[l]
        gb_col = jnp.zeros((1, LANES), f32).at[0, :2 * N_ML].set(gb.reshape(-1))
        gb_row = gb.reshape(2 * N_ML, 1)
        y_ml = _mlstm(u_ml, g_row, conv_w[l], conv_b[l].reshape(1, D_ML),
                      _block_diag(w_mq[l]).astype(bf16), (_block_diag(w_mk[l]) * qscale).astype(bf16),
                      _block_diag(w_mv[l]).astype(bf16), gb_col, gb_row)
        y_dil = _dilated(u_dil, slopes)
        y_sb = _stickbreak(u_sb)

        x, h2, comb = _post(x, y_ml, y_dil, y_sb, mod, g_head[l].reshape(1, d), w_out[l].astype(bf16),
                            wr_hi, wr_lo, rb, tm)
        x = _moe(x, h2, comb, mod, w_gate_e[l].astype(bf16), w_up_e[l].astype(bf16), w_down_e[l].astype(bf16),
                 g_final2, min(1024, s), final=(l == depth - 1))
    return x
```

```python
import functools

import jax
import jax.numpy as jnp
from jax import lax
from jax.experimental import pallas as pl
from jax.experimental.pallas import tpu as pltpu

HEAD_DIM = 64
N_ML = 4
N_DIL = 6
N_SB = 6
D_ML = N_ML * HEAD_DIM
D_DIL = N_DIL * HEAD_DIM
D_SB = N_SB * HEAD_DIM
CONV_WIDTH = 4
CHUNK = 128
DILATIONS = (1, 4, 16)
DIL_SPAN = 128
ALIBI_MAX_BIAS = 8.0
N_EXPERTS = 16
N_GROUPS = 4
EPG = N_EXPERTS // N_GROUPS
EPS = 1e-6
LANES = 128
ML_COLS = 2 * D_ML + LANES
SB_DEAD_LOG = -104.0
SB_NO_KEYS = -1e30
SB_QBLOCKS = 4
SB_GROUP = 2
DIL_UNITS = 4
MOE_ROWS = 512
VMEM_LIMIT = 56 * 1024 * 1024

_NT = (((1,), (1,)), ((), ()))
_TN = (((0,), (0,)), ((), ()))

f32 = jnp.float32
bf16 = jnp.bfloat16


def _dot(a, b):
    return jnp.dot(a, b, preferred_element_type=f32)


def _dot_nt(a, b):
    return lax.dot_general(a, b, _NT, preferred_element_type=f32)


def _split_bf16(x):
    hi = x.astype(bf16)
    lo = (x - hi.astype(f32)).astype(bf16)
    return hi, lo


def _iota(shape, dim):
    return lax.broadcasted_iota(jnp.int32, shape, dim)


def _rms(x):
    return x * lax.rsqrt(jnp.mean(x * x, axis=-1, keepdims=True) + EPS)


def _cparams(sem):
    return pltpu.CompilerParams(dimension_semantics=sem, vmem_limit_bytes=VMEM_LIMIT)


def _ada_kernel(c_ref, w_ref, b_ref, o_ref):
    c = c_ref[...]
    c_act = (c * lax.logistic(c)).astype(bf16)
    o_ref[...] = _dot(c_act, w_ref[...].astype(bf16)) + b_ref[...]


def _ada(c, w_ada, b_ada):
    depth, d, nd = w_ada.shape
    b = c.shape[0]
    nblk = nd // d
    return pl.pallas_call(
        _ada_kernel,
        out_shape=jax.ShapeDtypeStruct((depth, b, nd), f32),
        grid=(depth, nblk),
        in_specs=[pl.BlockSpec((b, d), lambda l, j: (0, 0)),
                  pl.BlockSpec((None, d, d), lambda l, j: (l, 0, j)),
                  pl.BlockSpec((None, 1, d), lambda l, j: (l, 0, j))],
        out_specs=pl.BlockSpec((None, b, d), lambda l, j: (l, 0, j)),
        compiler_params=_cparams(("parallel", "parallel")),
        name="ada_mod",
    )(c, w_ada, b_ada.reshape(depth, 1, nd))


def _inproj_kernel(x_ref, mod_ref, wml_ref, wdil_ref, wsb_ref, wgt_ref,
                   uml_ref, udil_ref, usb_ref, grow_ref):
    x = x_ref[...]
    mod = mod_ref[...]
    h = (_rms(x) * (1.0 + mod[1:2, :]) + mod[0:1, :]).astype(bf16)
    uml_ref[...] = _dot(h, wml_ref[...])
    udil_ref[...] = _dot(h, wdil_ref[...])
    usb_ref[...] = _dot(h, wsb_ref[...]).astype(bf16)
    grow_ref[...] = _dot_nt(wgt_ref[...], h)


def _inproj(x, mod, w_ml, w_dil, w_sb, w_gt, tm):
    b, s, d = x.shape
    full = lambda a: pl.BlockSpec(a.shape, lambda i, j: (0,) * a.ndim)
    return pl.pallas_call(
        _inproj_kernel,
        out_shape=(jax.ShapeDtypeStruct((b, s, ML_COLS), f32),
                   jax.ShapeDtypeStruct((b, s, 3 * D_DIL), f32),
                   jax.ShapeDtypeStruct((b, s, 3 * D_SB), bf16),
                   jax.ShapeDtypeStruct((b, 2 * N_ML, s), f32)),
        grid=(b, s // tm),
        in_specs=[pl.BlockSpec((None, tm, d), lambda i, j: (i, j, 0)),
                  pl.BlockSpec((None, 6, d), lambda i, j: (i, 0, 0)),
                  full(w_ml), full(w_dil), full(w_sb), full(w_gt)],
        out_specs=(pl.BlockSpec((None, tm, ML_COLS), lambda i, j: (i, j, 0)),
                   pl.BlockSpec((None, tm, 3 * D_DIL), lambda i, j: (i, j, 0)),
                   pl.BlockSpec((None, tm, 3 * D_SB), lambda i, j: (i, j, 0)),
                   pl.BlockSpec((None, 2 * N_ML, tm), lambda i, j: (i, 0, j))),
        compiler_params=_cparams(("parallel", "parallel")),
        name="norm_inproj",
    )(x, mod, w_ml, w_dil, w_sb, w_gt)


def _mlstm_kernel(u_ref, grow_ref, convw_ref, convb_ref, wq_ref, wk_ref, wv_ref, gbc_ref, gbr_ref,
                  y_ref, prev_ref, c_ref, n_ref, m_ref):
    L = CHUNK

    @pl.when(pl.program_id(1) == 0)
    def _():
        prev_ref[...] = jnp.zeros_like(prev_ref)
        c_ref[...] = jnp.zeros_like(c_ref)
        n_ref[...] = jnp.zeros_like(n_ref)
        m_ref[...] = jnp.zeros_like(m_ref)

    u = u_ref[...]
    xm = u[:, :D_ML]
    og = u[:, D_ML:2 * D_ML]
    gcol = u[:, 2 * D_ML:] + gbc_ref[...]
    grow = grow_ref[...] + gbr_ref[...]

    prev = prev_ref[...]
    row = _iota((L, D_ML), 0)
    convw = convw_ref[...]
    xc = xm * convw[CONV_WIDTH - 1:CONV_WIDTH, :]
    for sft in range(1, CONV_WIDTH):
        shifted = jnp.where(row < sft, pltpu.roll(prev, sft, 0), pltpu.roll(xm, sft, 0))
        xc = xc + shifted * convw[CONV_WIDTH - 1 - sft:CONV_WIDTH - sft, :]
    prev_ref[...] = xm
    xc = xc + convb_ref[...]
    xc = xc * lax.logistic(xc)
    xcb = xc.astype(bf16)
    q = _dot(xcb, wq_ref[...])
    k = _dot(xcb, wk_ref[...])
    v = _dot(xm.astype(bf16), wv_ref[...])
    kb = k.astype(bf16)
    vb = v.astype(bf16)

    def logsig(t):
        return jnp.minimum(t, 0.0) - jnp.log1p(jnp.exp(-jnp.abs(t)))

    lf_col = logsig(gcol)
    lf_row = logsig(grow)
    ltri = (_iota((L, L), 1) <= _iota((L, L), 0)).astype(bf16)
    utri = (_iota((L, L), 0) <= _iota((L, L), 1)).astype(bf16)
    ch, cl = _split_bf16(lf_col)
    cum_col = _dot(ltri, ch) + _dot(ltri, cl)
    rh, rl = _split_bf16(lf_row)
    cum_row = _dot(rh, utri) + _dot(rl, utri)

    causal = _iota((L, L), 1) <= _iota((L, L), 0)
    lane = _iota((1, D_ML), 1)
    c_bd = c_ref[...]
    n_st = n_ref[...]
    m_st = m_ref[...]
    c_bdb = c_bd.astype(bf16)

    heads = range(N_ML)
    mhs = [(lane >= hd * HEAD_DIM) & (lane < (hd + 1) * HEAD_DIM) for hd in heads]
    qhs = [jnp.where(mhs[hd], q, 0.0) for hd in heads]
    qhb = [qh.astype(bf16) for qh in qhs]
    qk = [_dot_nt(qhb[hd], kb) for hd in heads]
    qc = [_dot(qhb[hd], c_bdb) for hd in heads]
    m_ts, inters, sws = [], [], []
    wfull = jnp.zeros((L, D_ML), f32)
    decay_l = jnp.zeros((1, D_ML), f32)
    mnew_l = jnp.zeros((1, D_ML), f32)
    for hd in heads:
        cumc = cum_col[:, N_ML + hd:N_ML + hd + 1]
        cumr = cum_row[N_ML + hd:N_ML + hd + 1, :]
        igr = grow[hd:hd + 1, :]
        igc = gcol[:, hd:hd + 1]
        m_prev = m_st[:, hd * HEAD_DIM:hd * HEAD_DIM + 1]
        log_d = jnp.where(causal, cumc - cumr + igr, -jnp.inf)
        log_inter = cumc + m_prev
        m_t = jnp.maximum(log_inter, jnp.max(log_d, axis=-1, keepdims=True))
        m_ts.append(m_t)
        inters.append(jnp.exp(log_inter - m_t))
        sws.append(qk[hd] * jnp.exp(log_d - m_t))
        last = cumc[L - 1:L, :]
        log_w = last - cumc + igc
        m_new = jnp.maximum(last + m_prev, jnp.max(log_w, axis=0, keepdims=True))
        wfull = jnp.where(mhs[hd], jnp.exp(log_w - m_new), wfull)
        decay_l = jnp.where(mhs[hd], jnp.exp(last + m_prev - m_new), decay_l)
        mnew_l = jnp.where(mhs[hd], m_new, mnew_l)
    swv = [_dot(sws[hd].astype(bf16), vb) for hd in heads]
    hmix = jnp.zeros((L, D_ML), f32)
    for hd in heads:
        num = inters[hd] * qc[hd] + swv[hd]
        den = (inters[hd] * jnp.sum(qhs[hd] * n_st, axis=-1, keepdims=True)
               + jnp.sum(sws[hd], axis=-1, keepdims=True))
        hout = num / jnp.maximum(jnp.abs(den), jnp.exp(-m_ts[hd]))
        hmix = jnp.where(mhs[hd], hout, hmix)

    kw = k * wfull
    upd = lax.dot_general(kw.astype(bf16), vb, _TN, preferred_element_type=f32)
    same_head = (_iota((D_ML, D_ML), 0) // HEAD_DIM) == (_iota((D_ML, D_ML), 1) // HEAD_DIM)
    c_ref[...] = decay_l * c_bd + jnp.where(same_head, upd, 0.0)
    n_ref[...] = decay_l * n_st + jnp.sum(kw, axis=0, keepdims=True)
    m_ref[...] = mnew_l
    y_ref[...] = lax.logistic(og) * hmix


def _mlstm(u_ml, g_row, conv_w, conv_b, wq_bd, wk_bd, wv_bd, gb_col, gb_row):
    b, s, _ = u_ml.shape
    full = lambda a: pl.BlockSpec(a.shape, lambda i, j: (0,) * a.ndim)
    return pl.pallas_call(
        _mlstm_kernel,
        out_shape=jax.ShapeDtypeStruct((b, s, D_ML), f32),
        grid=(b, s // CHUNK),
        in_specs=[pl.BlockSpec((None, CHUNK, ML_COLS), lambda i, j: (i, j, 0)),
                  pl.BlockSpec((None, 2 * N_ML, CHUNK), lambda i, j: (i, 0, j)),
                  full(conv_w), full(conv_b), full(wq_bd), full(wk_bd), full(wv_bd), full(gb_col), full(gb_row)],
        out_specs=pl.BlockSpec((None, CHUNK, D_ML), lambda i, j: (i, j, 0)),
        scratch_shapes=[pltpu.VMEM((CHUNK, D_ML), f32), pltpu.VMEM((D_ML, D_ML), f32),
                        pltpu.VMEM((1, D_ML), f32), pltpu.VMEM((1, D_ML), f32)],
        compiler_params=_cparams(("parallel", "arbitrary")),
        name="mlstm",
    )(u_ml, g_row, conv_w, conv_b, wq_bd, wk_bd, wv_bd, gb_col, gb_row)


def _rows(start, size, stride):
    return pl.ds(start, size) if stride == 1 else pl.ds(start, size, stride=stride)


def _dil_kernel(q_ref, k_ref, v_ref, slope_ref, o_ref, *scratch, seq, tile):
    npat = len(DILATIONS)
    kd = scratch[0:npat]
    vd = scratch[npat:2 * npat]
    op = scratch[2 * npat:3 * npat]
    ls = scratch[3 * npat:4 * npat]
    bias_ref = scratch[4 * npat]
    L = CHUNK
    ti = pl.program_id(2)
    lane = _iota((1, LANES), 1)
    m0 = lane < HEAD_DIM

    @pl.when(ti == 0)
    def _():
        row = _iota((L, L), 0)
        col = _iota((L, L), 1)
        for p, d in enumerate(DILATIONS):
            per = seq // d
            for r in range(d):
                kd[p][r * per:(r + 1) * per, :] = k_ref[_rows(r, per, d), :].astype(bf16)
                vd[p][r * per:(r + 1) * per, :] = v_ref[_rows(r, per, d), :].astype(bf16)
            for j in range(2):
                sl = slope_ref[j][0:1, :] * float(d)
                bias_ref[p, j, 0] = jnp.where(col >= row, -(sl * (row - col + DIL_SPAN).astype(f32)), -jnp.inf)
                bias_ref[p, j, 1] = jnp.full((L, L), -jnp.inf, f32)
                bias_ref[p, j, 2] = jnp.where(col <= row, -(sl * (row - col).astype(f32)), -jnp.inf)

    nunits = tile // L
    for p, d in enumerate(DILATIONS):
        per = seq // d
        blocks_per_res = tile // d // L

        def body(it, carry, p=p, d=d, per=per, blocks_per_res=blocks_per_res):
            units = []
            for c in range(DIL_UNITS):
                u = it * DIL_UNITS + c
                r = u // blocks_per_res
                blk = u % blocks_per_res
                start = blk * (L * d) + r
                q = q_ref[_rows(start, L, d), :].astype(bf16)
                m_first = (ti * tile) // d + blk * L
                base = r * per + m_first
                has_prev = m_first > 0
                cur0 = pl.multiple_of(base, L)
                prev0 = pl.multiple_of(jnp.where(has_prev, base - L, base), L)
                units.append(dict(start=start, q=q, pk=jnp.where(has_prev, 0, 1),
                                  kc=kd[p][pl.ds(cur0, L), :], vc=vd[p][pl.ds(cur0, L), :],
                                  kp=kd[p][pl.ds(prev0, L), :], vp=vd[p][pl.ds(prev0, L), :]))
            chains = [(un, j) for un in units for j in range(2)]
            scores = []
            for un, j in chains:
                qj = jnp.where(m0 if j == 0 else jnp.logical_not(m0), un["q"], jnp.zeros_like(un["q"]))
                scores.append((_dot_nt(qj, un["kp"]) + bias_ref[p, j, un["pk"]],
                               _dot_nt(qj, un["kc"]) + bias_ref[p, j, 2]))
            probs = []
            for sp, sc in scores:
                mx = jnp.max(jnp.maximum(sp, sc), axis=-1, keepdims=True)
                pp = jnp.exp(sp - mx)
                pc = jnp.exp(sc - mx)
                den = jnp.sum(pp + pc, axis=-1, keepdims=True)
                probs.append((pp.astype(bf16), pc.astype(bf16), den, mx + jnp.log(den)))
            outs = [(_dot(pp, un["vp"]) + _dot(pc, un["vc"])) / den
                    for (un, j), (pp, pc, den, _) in zip(chains, probs)]
            for c, un in enumerate(units):
                op[p][_rows(un["start"], L, d), :] = jnp.where(m0, outs[2 * c], outs[2 * c + 1])
                ls[p][_rows(un["start"], L, d), :] = jnp.where(m0, probs[2 * c][3], probs[2 * c + 1][3])
            return carry

        lax.fori_loop(0, nunits // DIL_UNITS, body, 0)

    l0, l1, l2 = ls[0][...], ls[1][...], ls[2][...]
    mx = jnp.maximum(jnp.maximum(l0, l1), l2)
    e0, e1, e2 = jnp.exp(l0 - mx), jnp.exp(l1 - mx), jnp.exp(l2 - mx)
    o_ref[...] = ((e0 * op[0][...] + e1 * op[1][...] + e2 * op[2][...]) / (e0 + e1 + e2)).astype(o_ref.dtype)


def _dilated(u_dil, slopes):
    b, s, _ = u_dil.shape
    npair = N_DIL // 2
    tile = DILATIONS[-1] * CHUNK
    assert s % tile == 0 and (tile // CHUNK) % DIL_UNITS == 0
    seq_buf = lambda dt: pltpu.VMEM((s, LANES), dt)
    tile_buf = pltpu.VMEM((tile, LANES), f32)
    npat = len(DILATIONS)
    return pl.pallas_call(
        functools.partial(_dil_kernel, seq=s, tile=tile),
        out_shape=jax.ShapeDtypeStruct((b, s, D_DIL), bf16),
        grid=(b, npair, s // tile),
        in_specs=[pl.BlockSpec((None, tile, LANES), lambda i, h, n: (i, n, h)),
                  pl.BlockSpec((None, s, LANES), lambda i, h, n: (i, 0, npair + h)),
                  pl.BlockSpec((None, s, LANES), lambda i, h, n: (i, 0, 2 * npair + h)),
                  pl.BlockSpec((None, 2, 8, LANES), lambda i, h, n: (h, 0, 0, 0))],
        out_specs=pl.BlockSpec((None, tile, LANES), lambda i, h, n: (i, n, h)),
        scratch_shapes=([seq_buf(bf16)] * (2 * npat) + [tile_buf] * (2 * npat)
                        + [pltpu.VMEM((npat, 2, 3, CHUNK, CHUNK), f32)]),
        compiler_params=_cparams(("parallel", "parallel", "arbitrary")),
        name="dilated",
    )(u_dil, u_dil, u_dil, slopes)


def _sb_kernel(q_ref, k_ref, v_ref, o_ref, qm_ref, r_ref, acc_ref, *, nq):
    L = CHUNK
    tile0 = pl.program_id(1) * nq
    row = _iota((L, L), 0)
    col = _iota((L, L), 1)
    strict = col < row
    lane = _iota((1, LANES), 1)
    m0 = lane < HEAD_DIM
    npair = N_SB // 2
    csum = jnp.concatenate([(_iota((L, L), 0) > _iota((L, L), 1)).astype(bf16), jnp.ones((L, L), bf16)], axis=1)

    for qb in range(nq):
        for hp in range(npair):
            q = q_ref[qb * L:(qb + 1) * L, hp * LANES:(hp + 1) * LANES]
            qm_ref[(qb * npair + hp) * 2] = jnp.where(m0, q, jnp.zeros_like(q))
            qm_ref[(qb * npair + hp) * 2 + 1] = jnp.where(m0, jnp.zeros_like(q), q)

    def sweep(it, diagonal):
        rmax = None
        for g0 in range(0, nq, SB_GROUP):
            chains = []
            for qb in range(g0, g0 + SB_GROUP):
                qi = tile0 + qb
                k0 = pl.multiple_of(jnp.maximum(qi - it, 0) * L, L)
                out_of_keys = jnp.where(qi - it <= 0, SB_NO_KEYS, 0.0)
                for hp in range(npair):
                    kblk = k_ref[pl.ds(k0, L), hp * LANES:(hp + 1) * LANES]
                    vblk = v_ref[pl.ds(k0, L), hp * LANES:(hp + 1) * LANES]
                    for j in range(2):
                        chains.append(((qb * npair + hp) * 2 + j, kblk, vblk, out_of_keys))
            zs = [_dot_nt(qm_ref[c], kblk) for c, kblk, _, _ in chains]
            zl, lkb = [], []
            for z in zs:
                lk = -(jnp.maximum(z, 0.0) + jnp.log(1.0 + jnp.exp(-jnp.abs(z))))
                zl.append(z + lk)
                lkb.append((jnp.where(strict, lk, 0.0) if diagonal else lk).astype(bf16))
            ts = [_dot(x, csum) for x in lkb]
            avs = []
            for (c, _, _, out_of_keys), s, t in zip(chains, zl, ts):
                if diagonal:
                    a = jnp.where(strict, jnp.exp(s + t[:, :L]), 0.0)
                    rs = t[:, L:] + out_of_keys
                else:
                    a = jnp.exp(s + r_ref[c] + t[:, :L])
                    rs = r_ref[c] + t[:, L:] + out_of_keys
                avs.append(a.astype(bf16))
                r_ref[c] = rs
                rmax = rs if rmax is None else jnp.maximum(rmax, rs)
            for (c, _, vblk, _), a in zip(chains, avs):
                if diagonal:
                    acc_ref[c] = _dot(a, vblk)
                else:
                    acc_ref[c] += _dot(a, vblk)
        return jnp.max(rmax) > SB_DEAD_LOG

    alive0 = sweep(0, True)

    def body(carry):
        _, it = carry
        return sweep(it, False), it + 1

    lax.while_loop(lambda carry: carry[0], body, (alive0, jnp.int32(1)))

    for qb in range(nq):
        for hp in range(npair):
            c = (qb * npair + hp) * 2
            o_ref[qb * L:(qb + 1) * L, hp * LANES:(hp + 1) * LANES] = (
                jnp.where(m0, acc_ref[c], acc_ref[c + 1]).astype(o_ref.dtype))


def _stickbreak(u_sb):
    b, s, _ = u_sb.shape
    nq = SB_QBLOCKS
    tq = nq * CHUNK
    nchain = nq * N_SB
    return pl.pallas_call(
        functools.partial(_sb_kernel, nq=nq),
        out_shape=jax.ShapeDtypeStruct((b, s, D_SB), bf16),
        grid=(b, s // tq),
        in_specs=[pl.BlockSpec((None, tq, D_SB), lambda i, n: (i, n, 0)),
                  pl.BlockSpec((None, s, D_SB), lambda i, n: (i, 0, 1)),
                  pl.BlockSpec((None, s, D_SB), lambda i, n: (i, 0, 2))],
        out_specs=pl.BlockSpec((None, tq, D_SB), lambda i, n: (i, n, 0)),
        scratch_shapes=[pltpu.VMEM((nchain, CHUNK, LANES), bf16),
                        pltpu.VMEM((nchain, CHUNK, CHUNK), f32),
                        pltpu.VMEM((nchain, CHUNK, LANES), f32)],
        compiler_params=_cparams(("parallel", "arbitrary")),
        name="stickbreak",
    )(u_sb, u_sb, u_sb)


def _post_kernel(x_ref, yml_ref, ydil_ref, ysb_ref, mod_ref, gh_ref, wout_ref, wr_hi_ref, wr_lo_ref, rb_ref,
                 xo_ref, h_ref, selt_ref, w_ref):
    mod = mod_ref[...]
    y = jnp.concatenate([yml_ref[...], ydil_ref[...].astype(f32), ysb_ref[...].astype(f32)], axis=1)
    pair = ((_iota((LANES, LANES), 0) // HEAD_DIM) == (_iota((LANES, LANES), 1) // HEAD_DIM)).astype(bf16)
    cols = []
    for cblk in range(y.shape[1] // LANES):
        yb = y[:, cblk * LANES:(cblk + 1) * LANES]
        hi, lo = _split_bf16(yb * yb)
        ms = (_dot(hi, pair) + _dot(lo, pair)) * (1.0 / HEAD_DIM)
        cols.append(yb * lax.rsqrt(ms + EPS))
    yn = (jnp.concatenate(cols, axis=1) * gh_ref[...]).astype(bf16)
    xn = x_ref[...] + mod[2:3, :] * _dot(yn, wout_ref[...])
    xo_ref[...] = xn
    h2 = _rms(xn) * (1.0 + mod[4:5, :]) + mod[3:4, :]
    hh, hl = _split_bf16(h2)
    h_ref[...] = h2
    logits = _dot(hh, wr_hi_ref[...]) + _dot(hl, wr_hi_ref[...]) + _dot(hh, wr_lo_ref[...])
    lane = _iota((1, LANES), 1)
    glane = lane < N_GROUPS
    sc = [lax.logistic(logits[:, j * LANES:(j + 1) * LANES]) for j in range(EPG)]
    bs = [jnp.where(glane, sc[j] + rb_ref[:, j * LANES:(j + 1) * LANES], -jnp.inf) for j in range(EPG)]
    hi01, lo01 = jnp.maximum(bs[0], bs[1]), jnp.minimum(bs[0], bs[1])
    hi23, lo23 = jnp.maximum(bs[2], bs[3]), jnp.minimum(bs[2], bs[3])
    top1 = jnp.maximum(hi01, hi23)
    top2 = jnp.maximum(jnp.minimum(hi01, hi23), jnp.maximum(lo01, lo23))
    gscore = jnp.where(glane, top1 + top2, -jnp.inf)
    gmax = jnp.max(gscore, axis=-1, keepdims=True)
    gidx = jnp.min(jnp.where(gscore == gmax, lane, LANES), axis=-1, keepdims=True)
    gsel = lane == gidx
    picks, vals = [], []
    for j in range(EPG):
        rank = jnp.zeros(bs[j].shape, jnp.int32)
        for jj in range(EPG):
            if jj != j:
                ahead = (bs[jj] >= bs[j]) if jj < j else (bs[jj] > bs[j])
                rank = rank + ahead.astype(jnp.int32)
        picks.append(gsel & (rank < 2))
        vals.append(jnp.sum(jnp.where(gsel, sc[j], 0.0), axis=-1, keepdims=True))
    erow = _iota((N_EXPERTS, LANES), 0)
    gcol = _iota((N_EXPERTS, LANES), 1)
    sel_t = jnp.zeros((N_EXPERTS, x_ref.shape[0]), f32)
    for j in range(EPG):
        place = ((erow == gcol * EPG + j) & (gcol < N_GROUPS)).astype(bf16)
        sel_t = sel_t + _dot_nt(place, picks[j].astype(bf16))
    selt_ref[...] = sel_t.astype(selt_ref.dtype)
    seen = jnp.zeros_like(vals[0])
    w_lo = jnp.zeros_like(vals[0])
    w_hi = jnp.zeros_like(vals[0])
    for j in range(EPG):
        picked = jnp.sum(picks[j].astype(f32), axis=-1, keepdims=True) > 0.0
        w_lo = w_lo + jnp.where(picked & (seen == 0.0), vals[j], 0.0)
        w_hi = w_hi + jnp.where(picked & (seen == 1.0), vals[j], 0.0)
        seen = seen + picked.astype(f32)
    tot = w_lo + w_hi
    w_ref[...] = jnp.where(lane == 0, w_lo / tot, jnp.where(lane == 1, w_hi / tot, 0.0))


def _post(x, y_ml, y_dil, y_sb, mod, g_head, w_out, wr_hi, wr_lo, rb, tm):
    b, s, d = x.shape
    tok = lambda w: pl.BlockSpec((None, tm, w), lambda i, j: (i, j, 0))
    full = lambda a: pl.BlockSpec(a.shape, lambda i, j: (0,) * a.ndim)
    return pl.pallas_call(
        _post_kernel,
        out_shape=(jax.ShapeDtypeStruct((b, s, d), f32),
                   jax.ShapeDtypeStruct((b, s, d), f32),
                   jax.ShapeDtypeStruct((N_EXPERTS, b * s), bf16),
                   jax.ShapeDtypeStruct((b, s, LANES), f32)),
        grid=(b, s // tm),
        in_specs=[tok(d), tok(D_ML), tok(D_DIL), tok(D_SB), pl.BlockSpec((None, 6, d), lambda i, j: (i, 0, 0)),
                  full(g_head), full(w_out), full(wr_hi), full(wr_lo), full(rb)],
        out_specs=(tok(d), tok(d),
                   pl.BlockSpec((N_EXPERTS, tm), lambda i, j: (0, i * (s // tm) + j)), tok(LANES)),
        compiler_params=_cparams(("parallel", "parallel")),
        name="merge_outproj_router",
    )(x, y_ml, y_dil, y_sb, mod, g_head, w_out, wr_hi, wr_lo, rb)


def _plan_kernel(selt_ref, pos0_ref, pos1_ref, tiles_ref, before_ref, cnt_ref, run_ref, off_ref, *, tp, ntile_pad):
    phase = pl.program_id(0)
    step = pl.program_id(1)
    sel = selt_ref[...]
    ones = jnp.ones((tp, LANES), bf16)

    @pl.when((phase == 0) & (step == 0))
    def _():
        cnt_ref[...] = jnp.zeros_like(cnt_ref)
        before_ref[...] = (_iota((tp, tp), 0) < _iota((tp, tp), 1)).astype(bf16)
        tiles_ref[...] = jnp.zeros_like(tiles_ref)

    @pl.when(phase == 0)
    def _():
        cnt_ref[...] += _dot(sel, ones)
        pos0_ref[...] = jnp.zeros_like(pos0_ref)
        pos1_ref[...] = jnp.zeros_like(pos1_ref)

    @pl.when((phase == 1) & (step == 0))
    def _():
        cnt = cnt_ref[...]
        padded = jnp.floor((cnt + (MOE_ROWS - 1.0)) * (1.0 / MOE_ROWS)) * MOE_ROWS
        rows, acc = [], jnp.zeros((1, LANES), f32)
        for e in range(N_EXPERTS):
            rows.append(acc)
            acc = acc + padded[e:e + 1, :]
        off = jnp.concatenate(rows, axis=0)
        off_ref[...] = off
        run_ref[...] = jnp.zeros_like(run_ref)
        start = _iota((1, ntile_pad), 1).astype(f32) * MOE_ROWS
        ends = (off + padded)[:, 0:1]
        owner = jnp.sum((start >= ends).astype(f32), axis=0, keepdims=True)
        tiles_ref[0:1, :] = jnp.minimum(owner, N_EXPERTS - 1.0).astype(jnp.int32)
        tiles_ref[1:2, :] = (start < acc[:, 0:1]).astype(jnp.int32)

    @pl.when(phase == 1)
    def _():
        pos = _dot(sel, before_ref[...]) + (off_ref[...] + run_ref[...])[:, 0:1]
        picked = sel > 0
        pos0_ref[...] = jnp.min(jnp.where(picked, pos, 1e9), axis=0, keepdims=True).astype(jnp.int32)
        pos1_ref[...] = jnp.max(jnp.where(picked, pos, -1.0), axis=0, keepdims=True).astype(jnp.int32)
        run_ref[...] += _dot(sel, ones)


def _plan(sel_t, tp, ntiles):
    ne, t = sel_t.shape
    ntile_pad = -(-ntiles // LANES) * LANES
    pos0, pos1, tiles = pl.pallas_call(
        functools.partial(_plan_kernel, tp=tp, ntile_pad=ntile_pad),
        out_shape=(jax.ShapeDtypeStruct((1, t), jnp.int32), jax.ShapeDtypeStruct((1, t), jnp.int32),
                   jax.ShapeDtypeStruct((8, ntile_pad), jnp.int32)),
        grid=(2, t // tp),
        in_specs=[pl.BlockSpec((ne, tp), lambda p, i: (0, i))],
        out_specs=(pl.BlockSpec((1, tp), lambda p, i: (0, i * p)), pl.BlockSpec((1, tp), lambda p, i: (0, i * p)),
                   pl.BlockSpec((8, ntile_pad), lambda p, i: (0, 0))),
        scratch_shapes=[pltpu.VMEM((tp, tp), bf16), pltpu.VMEM((ne, LANES), f32),
                        pltpu.VMEM((ne, LANES), f32), pltpu.VMEM((ne, LANES), f32)],
        compiler_params=_cparams(("arbitrary", "arbitrary")),
        name="moe_plan",
    )(sel_t)
    return pos0.reshape(t), pos1.reshape(t), tiles[0, :ntiles], tiles[1, :ntiles]


def _dispatch_kernel(pos0_ref, pos1_ref, h_hbm, zero_hbm, xs_hbm, sem, *, td):
    del zero_hbm
    base = pl.program_id(0) * td

    def issue(r, carry):
        src = h_hbm.at[pl.ds(base + r, 1)]
        pltpu.make_async_copy(src, xs_hbm.at[pl.ds(pos0_ref[r], 1)], sem.at[0]).start()
        pltpu.make_async_copy(src, xs_hbm.at[pl.ds(pos1_ref[r], 1)], sem.at[1]).start()
        return carry

    lax.fori_loop(0, td, issue, 0, unroll=8)

    def drain(r, carry):
        src = h_hbm.at[pl.ds(base + r, 1)]
        pltpu.make_async_copy(src, xs_hbm.at[pl.ds(pos0_ref[r], 1)], sem.at[0]).wait()
        pltpu.make_async_copy(src, xs_hbm.at[pl.ds(pos1_ref[r], 1)], sem.at[1]).wait()
        return carry

    lax.fori_loop(0, td, drain, 0, unroll=8)


def _dispatch(h_rows, pos0, pos1, nrows, td):
    t, w = h_rows.shape
    smem = lambda: pl.BlockSpec((td,), lambda i: (i,), memory_space=pltpu.SMEM)
    anyspace = lambda: pl.BlockSpec(memory_space=pl.ANY)
    return pl.pallas_call(
        functools.partial(_dispatch_kernel, td=td),
        out_shape=jax.ShapeDtypeStruct((nrows, w), h_rows.dtype),
        grid=(t // td,),
        in_specs=[smem(), smem(), anyspace(), anyspace()],
        out_specs=anyspace(),
        scratch_shapes=[pltpu.SemaphoreType.DMA((2,))],
        input_output_aliases={3: 0},
        compiler_params=_cparams(("arbitrary",)),
        name="moe_dispatch",
    )(pos0, pos1, h_rows, jnp.zeros((nrows, w), h_rows.dtype))


def _experts_kernel(owner_ref, used_ref, xs_ref, wg_ref, wu_ref, wd_ref, ys_ref):
    tile = pl.program_id(0)

    @pl.when(used_ref[tile] > 0)
    def _():
        xb = xs_ref[...].astype(bf16)
        g = _dot(xb, wg_ref[...].astype(bf16))
        he = g * lax.logistic(g) * _dot(xb, wu_ref[...].astype(bf16))
        ys_ref[...] = _dot(he.astype(bf16), wd_ref[...].astype(bf16))

    @pl.when(used_ref[tile] == 0)
    def _():
        ys_ref[...] = jnp.zeros_like(ys_ref)


def _experts(xs, owner, used, wg, wu, wd):
    nrows, w = xs.shape
    _, d, de = wg.shape
    return pl.pallas_call(
        _experts_kernel,
        out_shape=jax.ShapeDtypeStruct((nrows, w), xs.dtype),
        grid_spec=pltpu.PrefetchScalarGridSpec(
            num_scalar_prefetch=2,
            grid=(nrows // MOE_ROWS,),
            in_specs=[pl.BlockSpec((MOE_ROWS, w), lambda g, own, use: (g, 0)),
                      pl.BlockSpec((None, d, de), lambda g, own, use: (own[g], 0, 0)),
                      pl.BlockSpec((None, d, de), lambda g, own, use: (own[g], 0, 0)),
                      pl.BlockSpec((None, de, d), lambda g, own, use: (own[g], 0, 0))],
            out_specs=pl.BlockSpec((MOE_ROWS, w), lambda g, own, use: (g, 0))),
        compiler_params=_cparams(("arbitrary",)),
        name="moe_experts",
    )(owner, used, xs, wg, wu, wd)


def _combine_kernel(pos0_ref, pos1_ref, npos0_ref, npos1_ref, ys_hbm, x_ref, w_ref, mod_ref, gf_ref, o_ref,
                    buf, sem, *, tc, final):
    step = pl.program_id(0)
    nstep = pl.num_programs(0)
    slot = step % 2

    def rows(p0_ref, p1_ref, slot_, start):
        def one(r, carry):
            for k, p_ref in enumerate((p0_ref, p1_ref)):
                cp = pltpu.make_async_copy(ys_hbm.at[pl.ds(p_ref[r], 1)], buf.at[slot_, k, pl.ds(r, 1)],
                                           sem.at[slot_, k])
                cp.start() if start else cp.wait()
            return carry
        lax.fori_loop(0, tc, one, 0, unroll=8)

    @pl.when(step == 0)
    def _():
        rows(pos0_ref, pos1_ref, 0, True)

    @pl.when(step + 1 < nstep)
    def _():
        rows(npos0_ref, npos1_ref, 1 - slot, True)

    rows(pos0_ref, pos1_ref, slot, False)
    w = w_ref[...]
    y = w[:, 0:1] * buf[slot, 0] + w[:, 1:2] * buf[slot, 1]
    xn = x_ref[...] + mod_ref[...][5:6, :] * y
    if final:
        xn = _rms(xn) * gf_ref[...]
    o_ref[...] = xn


def _combine(ys, pos0, pos1, x, w_pair, mod, g_final, tc, final):
    b, s, d = x.shape
    t = b * s
    nstep = t // tc
    per_batch = s // tc
    cur = lambda: pl.BlockSpec((tc,), lambda i: (i,), memory_space=pltpu.SMEM)
    nxt = lambda: pl.BlockSpec((tc,), lambda i: (jnp.minimum(i + 1, nstep - 1),), memory_space=pltpu.SMEM)
    out = pl.pallas_call(
        functools.partial(_combine_kernel, tc=tc, final=final),
        out_shape=jax.ShapeDtypeStruct((t, d), f32),
        grid=(nstep,),
        in_specs=[cur(), cur(), nxt(), nxt(), pl.BlockSpec(memory_space=pl.ANY),
                  pl.BlockSpec((tc, d), lambda i: (i, 0)),
                  pl.BlockSpec((tc, LANES), lambda i: (i, 0)),
                  pl.BlockSpec((None, 6, d), lambda i: (i // per_batch, 0, 0)),
                  pl.BlockSpec((1, d), lambda i: (0, 0))],
        out_specs=pl.BlockSpec((tc, d), lambda i: (i, 0)),
        scratch_shapes=[pltpu.VMEM((2, 2, tc, ys.shape[1]), ys.dtype), pltpu.SemaphoreType.DMA((2, 2))],
        compiler_params=_cparams(("arbitrary",)),
        name="moe_combine_final" if final else "moe_combine",
    )(pos0, pos1, pos0, pos1, ys, x.reshape(t, d), w_pair.reshape(t, LANES), mod, g_final)
    return out.reshape(b, s, d)


def _moe(x, h_rows, sel_t, w_pair, mod, wg, wu, wd, g_final, final):
    b, s, d = x.shape
    t = b * s
    nrows = 2 * t + N_EXPERTS * MOE_ROWS
    pos0, pos1, owner, used = _plan(sel_t, min(1024, t), nrows // MOE_ROWS)
    xs = _dispatch(h_rows.reshape(t, d), pos0, pos1, nrows, min(1024, t))
    ys = _experts(xs, owner, used, wg, wu, wd)
    return _combine(ys, pos0, pos1, x, w_pair, mod, g_final, min(512, s), final)


def _block_diag(w):
    n, a, c = w.shape
    out = jnp.zeros((n * a, n * c), w.dtype)
    for i in range(n):
        out = out.at[i * a:(i + 1) * a, i * c:(i + 1) * c].set(w[i])
    return out


def kernel(x, c, w_in, conv_w, conv_b, w_mq, w_mk, w_mv, gate_bias, g_head, w_out, w_ada, b_ada,
           w_router, router_bias, w_gate_e, w_up_e, w_down_e, g_final):
    b, s, d = x.shape
    depth = w_in.shape[0]
    qscale = HEAD_DIM ** -0.5
    tm = min(512, s)

    mod_all = _ada(c, w_ada, b_ada).reshape(depth, b, 6, d)

    wr = jnp.zeros((d, EPG, LANES), f32).at[:, :, :N_GROUPS].set(
        w_router.reshape(d, N_GROUPS, EPG).transpose(0, 2, 1)).reshape(d, EPG * LANES)
    wr_hi = wr.astype(bf16)
    wr_lo = (wr - wr_hi.astype(f32)).astype(bf16)
    rb = jnp.zeros((1, EPG, LANES), f32).at[0, :, :N_GROUPS].set(
        router_bias.reshape(N_GROUPS, EPG).T).reshape(1, EPG * LANES)

    slope = jnp.exp2(-ALIBI_MAX_BIAS * jnp.arange(1, N_DIL + 1, dtype=f32) / N_DIL)
    slopes = jnp.broadcast_to(slope.reshape(N_DIL // 2, 2, 1, 1), (N_DIL // 2, 2, 8, LANES))
    g_final2 = g_final.reshape(1, d)

    off_i = 2 * D_ML
    off_dil = off_i + 2 * N_ML
    off_sb = off_dil + 3 * D_DIL

    for l in range(depth):
        w = w_in[l]
        w_ml = jnp.concatenate([w[:, :off_i], w[:, off_i:off_dil],
                                jnp.zeros((d, LANES - 2 * N_ML), f32)], axis=1).astype(bf16)
        w_gt = w[:, off_i:off_dil].T.astype(bf16)
        wdil = w[:, off_dil:off_sb]
        w_dil = jnp.concatenate([wdil[:, :D_DIL] * qscale, wdil[:, D_DIL:]], axis=1).astype(bf16)
        wsb = w[:, off_sb:]
        w_sb = jnp.concatenate([wsb[:, :D_SB] * qscale, wsb[:, D_SB:]], axis=1).astype(bf16)
        mod = mod_all[l]

        u_ml, u_dil, u_sb, g_row = _inproj(x, mod, w_ml, w_dil, w_sb, w_gt, tm)

        gb = gate_bias[l]
        gb_col = jnp.zeros((1, LANES), f32).at[0, :2 * N_ML].set(gb.reshape(-1))
        gb_row = gb.reshape(2 * N_ML, 1)
        y_ml = _mlstm(u_ml, g_row, conv_w[l], conv_b[l].reshape(1, D_ML),
                      _block_diag(w_mq[l]).astype(bf16), (_block_diag(w_mk[l]) * qscale).astype(bf16),
                      _block_diag(w_mv[l]).astype(bf16), gb_col, gb_row)
        y_dil = _dilated(u_dil, slopes)
        y_sb = _stickbreak(u_sb)

        x, h_rows, sel_t, w_pair = _post(x, y_ml, y_dil, y_sb, mod, g_head[l].reshape(1, d),
                                           w_out[l].astype(bf16), wr_hi, wr_lo, rb, tm)
        x = _moe(x, h_rows, sel_t, w_pair, mod, w_gate_e[l], w_up_e[l], w_down_e[l], g_final2,
                 final=(l == depth - 1))
    return x
```

```python
import functools

import jax
import jax.numpy as jnp
from jax import lax
from jax.experimental import pallas as pl
from jax.experimental.pallas import tpu as pltpu

HEAD_DIM = 64
N_ML = 4
N_DIL = 6
N_SB = 6
D_ML = N_ML * HEAD_DIM
D_DIL = N_DIL * HEAD_DIM
D_SB = N_SB * HEAD_DIM
CONV_WIDTH = 4
CHUNK = 128
DILATIONS = (1, 4, 16)
DIL_SPAN = 128
ALIBI_MAX_BIAS = 8.0
N_EXPERTS = 16
N_GROUPS = 4
EPG = N_EXPERTS // N_GROUPS
EPS = 1e-6
LANES = 128
ML_COLS = 2 * D_ML + LANES
SB_DEAD_LOG = -104.0
SB_NO_KEYS = -1e30
SB_QBLOCKS = 4
SB_GROUP = 2
DIL_UNITS = 4
MOE_ROWS = 512
VMEM_LIMIT = 56 * 1024 * 1024

_NT = (((1,), (1,)), ((), ()))
_TN = (((0,), (0,)), ((), ()))

f32 = jnp.float32
bf16 = jnp.bfloat16


def _dot(a, b):
    return jnp.dot(a, b, preferred_element_type=f32)


def _dot_nt(a, b):
    return lax.dot_general(a, b, _NT, preferred_element_type=f32)


def _split_bf16(x):
    hi = x.astype(bf16)
    lo = (x - hi.astype(f32)).astype(bf16)
    return hi, lo


def _iota(shape, dim):
    return lax.broadcasted_iota(jnp.int32, shape, dim)


def _rms(x):
    return x * lax.rsqrt(jnp.mean(x * x, axis=-1, keepdims=True) + EPS)


def _cparams(sem):
    return pltpu.CompilerParams(dimension_semantics=sem, vmem_limit_bytes=VMEM_LIMIT)


def _ada_kernel(c_ref, w_ref, b_ref, o_ref):
    c = c_ref[...]
    c_act = (c * lax.logistic(c)).astype(bf16)
    o_ref[...] = _dot(c_act, w_ref[...].astype(bf16)) + b_ref[...]


def _ada(c, w_ada, b_ada):
    depth, d, nd = w_ada.shape
    b = c.shape[0]
    nblk = nd // d
    return pl.pallas_call(
        _ada_kernel,
        out_shape=jax.ShapeDtypeStruct((depth, b, nd), f32),
        grid=(depth, nblk),
        in_specs=[pl.BlockSpec((b, d), lambda l, j: (0, 0)),
                  pl.BlockSpec((None, d, d), lambda l, j: (l, 0, j)),
                  pl.BlockSpec((None, 1, d), lambda l, j: (l, 0, j))],
        out_specs=pl.BlockSpec((None, b, d), lambda l, j: (l, 0, j)),
        compiler_params=_cparams(("parallel", "parallel")),
        name="ada_mod",
    )(c, w_ada, b_ada.reshape(depth, 1, nd))


def _inproj_kernel(x_ref, mod_ref, wml_ref, wdil_ref, wsb_ref, wgt_ref,
                   uml_ref, udil_ref, usb_ref, grow_ref):
    x = x_ref[...]
    mod = mod_ref[...]
    h = (_rms(x) * (1.0 + mod[1:2, :]) + mod[0:1, :]).astype(bf16)
    uml_ref[...] = _dot(h, wml_ref[...])
    udil_ref[...] = _dot(h, wdil_ref[...])
    usb_ref[...] = _dot(h, wsb_ref[...]).astype(bf16)
    grow_ref[...] = _dot_nt(wgt_ref[...], h)


def _inproj(x, mod, w_ml, w_dil, w_sb, w_gt, tm):
    b, s, d = x.shape
    full = lambda a: pl.BlockSpec(a.shape, lambda i, j: (0,) * a.ndim)
    return pl.pallas_call(
        _inproj_kernel,
        out_shape=(jax.ShapeDtypeStruct((b, s, ML_COLS), f32),
                   jax.ShapeDtypeStruct((b, s, 3 * D_DIL), f32),
                   jax.ShapeDtypeStruct((b, s, 3 * D_SB), bf16),
                   jax.ShapeDtypeStruct((b, 2 * N_ML, s), f32)),
        grid=(b, s // tm),
        in_specs=[pl.BlockSpec((None, tm, d), lambda i, j: (i, j, 0)),
                  pl.BlockSpec((None, 6, d), lambda i, j: (i, 0, 0)),
                  full(w_ml), full(w_dil), full(w_sb), full(w_gt)],
        out_specs=(pl.BlockSpec((None, tm, ML_COLS), lambda i, j: (i, j, 0)),
                   pl.BlockSpec((None, tm, 3 * D_DIL), lambda i, j: (i, j, 0)),
                   pl.BlockSpec((None, tm, 3 * D_SB), lambda i, j: (i, j, 0)),
                   pl.BlockSpec((None, 2 * N_ML, tm), lambda i, j: (i, 0, j))),
        compiler_params=_cparams(("parallel", "parallel")),
        name="norm_inproj",
    )(x, mod, w_ml, w_dil, w_sb, w_gt)


def _mlstm_kernel(u_ref, grow_ref, convw_ref, convb_ref, wq_ref, wk_ref, wv_ref, gbc_ref, gbr_ref,
                  y_ref, prev_ref, c_ref, n_ref, m_ref):
    L = CHUNK

    @pl.when(pl.program_id(1) == 0)
    def _():
        prev_ref[...] = jnp.zeros_like(prev_ref)
        c_ref[...] = jnp.zeros_like(c_ref)
        n_ref[...] = jnp.zeros_like(n_ref)
        m_ref[...] = jnp.zeros_like(m_ref)

    u = u_ref[...]
    xm = u[:, :D_ML]
    og = u[:, D_ML:2 * D_ML]
    gcol = u[:, 2 * D_ML:] + gbc_ref[...]
    grow = grow_ref[...] + gbr_ref[...]

    prev = prev_ref[...]
    row = _iota((L, D_ML), 0)
    convw = convw_ref[...]
    xc = xm * convw[CONV_WIDTH - 1:CONV_WIDTH, :]
    for sft in range(1, CONV_WIDTH):
        shifted = jnp.where(row < sft, pltpu.roll(prev, sft, 0), pltpu.roll(xm, sft, 0))
        xc = xc + shifted * convw[CONV_WIDTH - 1 - sft:CONV_WIDTH - sft, :]
    prev_ref[...] = xm
    xc = xc + convb_ref[...]
    xc = xc * lax.logistic(xc)
    xcb = xc.astype(bf16)
    q = _dot(xcb, wq_ref[...])
    k = _dot(xcb, wk_ref[...])
    v = _dot(xm.astype(bf16), wv_ref[...])
    kb = k.astype(bf16)
    vb = v.astype(bf16)

    def logsig(t):
        return jnp.minimum(t, 0.0) - jnp.log1p(jnp.exp(-jnp.abs(t)))

    lf_col = logsig(gcol)
    lf_row = logsig(grow)
    ltri = (_iota((L, L), 1) <= _iota((L, L), 0)).astype(bf16)
    utri = (_iota((L, L), 0) <= _iota((L, L), 1)).astype(bf16)
    ch, cl = _split_bf16(lf_col)
    cum_col = _dot(ltri, ch) + _dot(ltri, cl)
    rh, rl = _split_bf16(lf_row)
    cum_row = _dot(rh, utri) + _dot(rl, utri)

    causal = _iota((L, L), 1) <= _iota((L, L), 0)
    lane = _iota((1, D_ML), 1)
    c_bd = c_ref[...]
    n_st = n_ref[...]
    m_st = m_ref[...]
    c_bdb = c_bd.astype(bf16)

    heads = range(N_ML)
    mhs = [(lane >= hd * HEAD_DIM) & (lane < (hd + 1) * HEAD_DIM) for hd in heads]
    qhs = [jnp.where(mhs[hd], q, 0.0) for hd in heads]
    qhb = [qh.astype(bf16) for qh in qhs]
    qk = [_dot_nt(qhb[hd], kb) for hd in heads]
    qc = [_dot(qhb[hd], c_bdb) for hd in heads]
    m_ts, inters, sws = [], [], []
    wfull = jnp.zeros((L, D_ML), f32)
    decay_l = jnp.zeros((1, D_ML), f32)
    mnew_l = jnp.zeros((1, D_ML), f32)
    for hd in heads:
        cumc = cum_col[:, N_ML + hd:N_ML + hd + 1]
        cumr = cum_row[N_ML + hd:N_ML + hd + 1, :]
        igr = grow[hd:hd + 1, :]
        igc = gcol[:, hd:hd + 1]
        m_prev = m_st[:, hd * HEAD_DIM:hd * HEAD_DIM + 1]
        log_d = jnp.where(causal, cumc - cumr + igr, -jnp.inf)
        log_inter = cumc + m_prev
        m_t = jnp.maximum(log_inter, jnp.max(log_d, axis=-1, keepdims=True))
        m_ts.append(m_t)
        inters.append(jnp.exp(log_inter - m_t))
        sws.append(qk[hd] * jnp.exp(log_d - m_t))
        last = cumc[L - 1:L, :]
        log_w = last - cumc + igc
        m_new = jnp.maximum(last + m_prev, jnp.max(log_w, axis=0, keepdims=True))
        wfull = jnp.where(mhs[hd], jnp.exp(log_w - m_new), wfull)
        decay_l = jnp.where(mhs[hd], jnp.exp(last + m_prev - m_new), decay_l)
        mnew_l = jnp.where(mhs[hd], m_new, mnew_l)
    swv = [_dot(sws[hd].astype(bf16), vb) for hd in heads]
    hmix = jnp.zeros((L, D_ML), f32)
    for hd in heads:
        num = inters[hd] * qc[hd] + swv[hd]
        den = (inters[hd] * jnp.sum(qhs[hd] * n_st, axis=-1, keepdims=True)
               + jnp.sum(sws[hd], axis=-1, keepdims=True))
        hout = num / jnp.maximum(jnp.abs(den), jnp.exp(-m_ts[hd]))
        hmix = jnp.where(mhs[hd], hout, hmix)

    kw = k * wfull
    upd = lax.dot_general(kw.astype(bf16), vb, _TN, preferred_element_type=f32)
    same_head = (_iota((D_ML, D_ML), 0) // HEAD_DIM) == (_iota((D_ML, D_ML), 1) // HEAD_DIM)
    c_ref[...] = decay_l * c_bd + jnp.where(same_head, upd, 0.0)
    n_ref[...] = decay_l * n_st + jnp.sum(kw, axis=0, keepdims=True)
    m_ref[...] = mnew_l
    y_ref[...] = lax.logistic(og) * hmix


def _mlstm(u_ml, g_row, conv_w, conv_b, wq_bd, wk_bd, wv_bd, gb_col, gb_row):
    b, s, _ = u_ml.shape
    full = lambda a: pl.BlockSpec(a.shape, lambda i, j: (0,) * a.ndim)
    return pl.pallas_call(
        _mlstm_kernel,
        out_shape=jax.ShapeDtypeStruct((b, s, D_ML), f32),
        grid=(b, s // CHUNK),
        in_specs=[pl.BlockSpec((None, CHUNK, ML_COLS), lambda i, j: (i, j, 0)),
                  pl.BlockSpec((None, 2 * N_ML, CHUNK), lambda i, j: (i, 0, j)),
                  full(conv_w), full(conv_b), full(wq_bd), full(wk_bd), full(wv_bd), full(gb_col), full(gb_row)],
        out_specs=pl.BlockSpec((None, CHUNK, D_ML), lambda i, j: (i, j, 0)),
        scratch_shapes=[pltpu.VMEM((CHUNK, D_ML), f32), pltpu.VMEM((D_ML, D_ML), f32),
                        pltpu.VMEM((1, D_ML), f32), pltpu.VMEM((1, D_ML), f32)],
        compiler_params=_cparams(("parallel", "arbitrary")),
        name="mlstm",
    )(u_ml, g_row, conv_w, conv_b, wq_bd, wk_bd, wv_bd, gb_col, gb_row)


def _rows(start, size, stride):
    return pl.ds(start, size) if stride == 1 else pl.ds(start, size, stride=stride)


def _dil_kernel(q_ref, k_ref, v_ref, slope_ref, o_ref, *scratch, seq, tile):
    npat = len(DILATIONS)
    kd = scratch[0:npat]
    vd = scratch[npat:2 * npat]
    op = scratch[2 * npat:3 * npat]
    ls = scratch[3 * npat:4 * npat]
    bias_ref = scratch[4 * npat]
    L = CHUNK
    ti = pl.program_id(2)
    lane = _iota((1, LANES), 1)
    m0 = lane < HEAD_DIM

    @pl.when(ti == 0)
    def _():
        row = _iota((L, L), 0)
        col = _iota((L, L), 1)
        for p, d in enumerate(DILATIONS):
            per = seq // d
            for r in range(d):
                kd[p][r * per:(r + 1) * per, :] = k_ref[_rows(r, per, d), :].astype(bf16)
                vd[p][r * per:(r + 1) * per, :] = v_ref[_rows(r, per, d), :].astype(bf16)
            for j in range(2):
                sl = slope_ref[j][0:1, :] * float(d)
                bias_ref[p, j, 0] = jnp.where(col >= row, -(sl * (row - col + DIL_SPAN).astype(f32)), -jnp.inf)
                bias_ref[p, j, 1] = jnp.full((L, L), -jnp.inf, f32)
                bias_ref[p, j, 2] = jnp.where(col <= row, -(sl * (row - col).astype(f32)), -jnp.inf)

    nunits = tile // L
    for p, d in enumerate(DILATIONS):
        per = seq // d
        blocks_per_res = tile // d // L

        def body(it, carry, p=p, d=d, per=per, blocks_per_res=blocks_per_res):
            units = []
            for c in range(DIL_UNITS):
                u = it * DIL_UNITS + c
                r = u // blocks_per_res
                blk = u % blocks_per_res
                start = blk * (L * d) + r
                q = q_ref[_rows(start, L, d), :].astype(bf16)
                m_first = (ti * tile) // d + blk * L
                base = r * per + m_first
                has_prev = m_first > 0
                cur0 = pl.multiple_of(base, L)
                prev0 = pl.multiple_of(jnp.where(has_prev, base - L, base), L)
                units.append(dict(start=start, q=q, pk=jnp.where(has_prev, 0, 1),
                                  kc=kd[p][pl.ds(cur0, L), :], vc=vd[p][pl.ds(cur0, L), :],
                                  kp=kd[p][pl.ds(prev0, L), :], vp=vd[p][pl.ds(prev0, L), :]))
            chains = [(un, j) for un in units for j in range(2)]
            scores = []
            for un, j in chains:
                qj = jnp.where(m0 if j == 0 else jnp.logical_not(m0), un["q"], jnp.zeros_like(un["q"]))
                scores.append((_dot_nt(qj, un["kp"]) + bias_ref[p, j, un["pk"]],
                               _dot_nt(qj, un["kc"]) + bias_ref[p, j, 2]))
            probs = []
            for sp, sc in scores:
                mx = jnp.max(jnp.maximum(sp, sc), axis=-1, keepdims=True)
                pp = jnp.exp(sp - mx)
                pc = jnp.exp(sc - mx)
                den = jnp.sum(pp + pc, axis=-1, keepdims=True)
                probs.append((pp.astype(bf16), pc.astype(bf16), den, mx + jnp.log(den)))
            outs = [(_dot(pp, un["vp"]) + _dot(pc, un["vc"])) / den
                    for (un, j), (pp, pc, den, _) in zip(chains, probs)]
            for c, un in enumerate(units):
                op[p][_rows(un["start"], L, d), :] = jnp.where(m0, outs[2 * c], outs[2 * c + 1])
                ls[p][_rows(un["start"], L, d), :] = jnp.where(m0, probs[2 * c][3], probs[2 * c + 1][3])
            return carry

        lax.fori_loop(0, nunits // DIL_UNITS, body, 0)

    l0, l1, l2 = ls[0][...], ls[1][...], ls[2][...]
    mx = jnp.maximum(jnp.maximum(l0, l1), l2)
    e0, e1, e2 = jnp.exp(l0 - mx), jnp.exp(l1 - mx), jnp.exp(l2 - mx)
    o_ref[...] = ((e0 * op[0][...] + e1 * op[1][...] + e2 * op[2][...]) / (e0 + e1 + e2)).astype(o_ref.dtype)


def _dilated(u_dil, slopes):
    b, s, _ = u_dil.shape
    npair = N_DIL // 2
    tile = DILATIONS[-1] * CHUNK
    assert s % tile == 0 and (tile // CHUNK) % DIL_UNITS == 0
    seq_buf = lambda dt: pltpu.VMEM((s, LANES), dt)
    tile_buf = pltpu.VMEM((tile, LANES), f32)
    npat = len(DILATIONS)
    return pl.pallas_call(
        functools.partial(_dil_kernel, seq=s, tile=tile),
        out_shape=jax.ShapeDtypeStruct((b, s, D_DIL), bf16),
        grid=(b, npair, s // tile),
        in_specs=[pl.BlockSpec((None, tile, LANES), lambda i, h, n: (i, n, h)),
                  pl.BlockSpec((None, s, LANES), lambda i, h, n: (i, 0, npair + h)),
                  pl.BlockSpec((None, s, LANES), lambda i, h, n: (i, 0, 2 * npair + h)),
                  pl.BlockSpec((None, 2, 8, LANES), lambda i, h, n: (h, 0, 0, 0))],
        out_specs=pl.BlockSpec((None, tile, LANES), lambda i, h, n: (i, n, h)),
        scratch_shapes=([seq_buf(bf16)] * (2 * npat) + [tile_buf] * (2 * npat)
                        + [pltpu.VMEM((npat, 2, 3, CHUNK, CHUNK), f32)]),
        compiler_params=_cparams(("parallel", "parallel", "arbitrary")),
        name="dilated",
    )(u_dil, u_dil, u_dil, slopes)


def _sb_kernel(q_ref, k_ref, v_ref, o_ref, qm_ref, r_ref, acc_ref, *, nq):
    L = CHUNK
    tile0 = pl.program_id(1) * nq
    row = _iota((L, L), 0)
    col = _iota((L, L), 1)
    strict = col < row
    lane = _iota((1, LANES), 1)
    m0 = lane < HEAD_DIM
    npair = N_SB // 2
    csum = jnp.concatenate([(_iota((L, L), 0) > _iota((L, L), 1)).astype(bf16), jnp.ones((L, L), bf16)], axis=1)

    for qb in range(nq):
        for hp in range(npair):
            q = q_ref[qb * L:(qb + 1) * L, hp * LANES:(hp + 1) * LANES]
            qm_ref[(qb * npair + hp) * 2] = jnp.where(m0, q, jnp.zeros_like(q))
            qm_ref[(qb * npair + hp) * 2 + 1] = jnp.where(m0, jnp.zeros_like(q), q)

    def sweep(it, diagonal):
        rmax = None
        for g0 in range(0, nq, SB_GROUP):
            chains = []
            for qb in range(g0, g0 + SB_GROUP):
                qi = tile0 + qb
                k0 = pl.multiple_of(jnp.maximum(qi - it, 0) * L, L)
                out_of_keys = jnp.where(qi - it <= 0, SB_NO_KEYS, 0.0)
                for hp in range(npair):
                    kblk = k_ref[pl.ds(k0, L), hp * LANES:(hp + 1) * LANES]
                    vblk = v_ref[pl.ds(k0, L), hp * LANES:(hp + 1) * LANES]
                    for j in range(2):
                        chains.append(((qb * npair + hp) * 2 + j, kblk, vblk, out_of_keys))
            zs = [_dot_nt(qm_ref[c], kblk) for c, kblk, _, _ in chains]
            zl, lkb = [], []
            for z in zs:
                lk = -(jnp.maximum(z, 0.0) + jnp.log(1.0 + jnp.exp(-jnp.abs(z))))
                zl.append(z + lk)
                lkb.append((jnp.where(strict, lk, 0.0) if diagonal else lk).astype(bf16))
            ts = [_dot(x, csum) for x in lkb]
            avs = []
            for (c, _, _, out_of_keys), s, t in zip(chains, zl, ts):
                if diagonal:
                    a = jnp.where(strict, jnp.exp(s + t[:, :L]), 0.0)
                    rs = t[:, L:] + out_of_keys
                else:
                    a = jnp.exp(s + r_ref[c] + t[:, :L])
                    rs = r_ref[c] + t[:, L:] + out_of_keys
                avs.append(a.astype(bf16))
                r_ref[c] = rs
                rmax = rs if rmax is None else jnp.maximum(rmax, rs)
            for (c, _, vblk, _), a in zip(chains, avs):
                if diagonal:
                    acc_ref[c] = _dot(a, vblk)
                else:
                    acc_ref[c] += _dot(a, vblk)
        return jnp.max(rmax) > SB_DEAD_LOG

    alive0 = sweep(0, True)

    def body(carry):
        _, it = carry
        return sweep(it, False), it + 1

    lax.while_loop(lambda carry: carry[0], body, (alive0, jnp.int32(1)))

    for qb in range(nq):
        for hp in range(npair):
            c = (qb * npair + hp) * 2
            o_ref[qb * L:(qb + 1) * L, hp * LANES:(hp + 1) * LANES] = (
                jnp.where(m0, acc_ref[c], acc_ref[c + 1]).astype(o_ref.dtype))


def _stickbreak(u_sb):
    b, s, _ = u_sb.shape
    nq = SB_QBLOCKS
    tq = nq * CHUNK
    nchain = nq * N_SB
    return pl.pallas_call(
        functools.partial(_sb_kernel, nq=nq),
        out_shape=jax.ShapeDtypeStruct((b, s, D_SB), bf16),
        grid=(b, s // tq),
        in_specs=[pl.BlockSpec((None, tq, D_SB), lambda i, n: (i, n, 0)),
                  pl.BlockSpec((None, s, D_SB), lambda i, n: (i, 0, 1)),
                  pl.BlockSpec((None, s, D_SB), lambda i, n: (i, 0, 2))],
        out_specs=pl.BlockSpec((None, tq, D_SB), lambda i, n: (i, n, 0)),
        scratch_shapes=[pltpu.VMEM((nchain, CHUNK, LANES), bf16),
                        pltpu.VMEM((nchain, CHUNK, CHUNK), f32),
                        pltpu.VMEM((nchain, CHUNK, LANES), f32)],
        compiler_params=_cparams(("parallel", "arbitrary")),
        name="stickbreak",
    )(u_sb, u_sb, u_sb)


def _post_kernel(x_ref, yml_ref, ydil_ref, ysb_ref, mod_ref, gh_ref, wout_ref, wr_hi_ref, wr_lo_ref, rb_ref,
                 xo_ref, h_ref, selt_ref, w_ref):
    mod = mod_ref[...]
    y = jnp.concatenate([yml_ref[...], ydil_ref[...].astype(f32), ysb_ref[...].astype(f32)], axis=1)
    pair = ((_iota((LANES, LANES), 0) // HEAD_DIM) == (_iota((LANES, LANES), 1) // HEAD_DIM)).astype(bf16)
    cols = []
    for cblk in range(y.shape[1] // LANES):
        yb = y[:, cblk * LANES:(cblk + 1) * LANES]
        hi, lo = _split_bf16(yb * yb)
        ms = (_dot(hi, pair) + _dot(lo, pair)) * (1.0 / HEAD_DIM)
        cols.append(yb * lax.rsqrt(ms + EPS))
    yn = (jnp.concatenate(cols, axis=1) * gh_ref[...]).astype(bf16)
    xn = x_ref[...] + mod[2:3, :] * _dot(yn, wout_ref[...])
    xo_ref[...] = xn
    h2 = _rms(xn) * (1.0 + mod[4:5, :]) + mod[3:4, :]
    hh, hl = _split_bf16(h2)
    h_ref[...] = h2
    logits = _dot(hh, wr_hi_ref[...]) + _dot(hl, wr_hi_ref[...]) + _dot(hh, wr_lo_ref[...])
    lane = _iota((1, LANES), 1)
    glane = lane < N_GROUPS
    sc = [lax.logistic(logits[:, j * LANES:(j + 1) * LANES]) for j in range(EPG)]
    bs = [jnp.where(glane, sc[j] + rb_ref[:, j * LANES:(j + 1) * LANES], -jnp.inf) for j in range(EPG)]
    hi01, lo01 = jnp.maximum(bs[0], bs[1]), jnp.minimum(bs[0], bs[1])
    hi23, lo23 = jnp.maximum(bs[2], bs[3]), jnp.minimum(bs[2], bs[3])
    top1 = jnp.maximum(hi01, hi23)
    top2 = jnp.maximum(jnp.minimum(hi01, hi23), jnp.maximum(lo01, lo23))
    gscore = jnp.where(glane, top1 + top2, -jnp.inf)
    gmax = jnp.max(gscore, axis=-1, keepdims=True)
    gidx = jnp.min(jnp.where(gscore == gmax, lane, LANES), axis=-1, keepdims=True)
    gsel = lane == gidx
    picks, vals = [], []
    for j in range(EPG):
        rank = jnp.zeros(bs[j].shape, jnp.int32)
        for jj in range(EPG):
            if jj != j:
                ahead = (bs[jj] >= bs[j]) if jj < j else (bs[jj] > bs[j])
                rank = rank + ahead.astype(jnp.int32)
        picks.append(gsel & (rank < 2))
        vals.append(jnp.sum(jnp.where(gsel, sc[j], 0.0), axis=-1, keepdims=True))
    erow = _iota((N_EXPERTS, LANES), 0)
    gcol = _iota((N_EXPERTS, LANES), 1)
    sel_t = jnp.zeros((N_EXPERTS, x_ref.shape[0]), f32)
    for j in range(EPG):
        place = ((erow == gcol * EPG + j) & (gcol < N_GROUPS)).astype(bf16)
        sel_t = sel_t + _dot_nt(place, picks[j].astype(bf16))
    selt_ref[...] = sel_t.astype(selt_ref.dtype)
    seen = jnp.zeros_like(vals[0])
    w_lo = jnp.zeros_like(vals[0])
    w_hi = jnp.zeros_like(vals[0])
    for j in range(EPG):
        picked = jnp.sum(picks[j].astype(f32), axis=-1, keepdims=True) > 0.0
        w_lo = w_lo + jnp.where(picked & (seen == 0.0), vals[j], 0.0)
        w_hi = w_hi + jnp.where(picked & (seen == 1.0), vals[j], 0.0)
        seen = seen + picked.astype(f32)
    tot = w_lo + w_hi
    w_ref[...] = jnp.where(lane == 0, w_lo / tot, jnp.where(lane == 1, w_hi / tot, 0.0))


def _post(x, y_ml, y_dil, y_sb, mod, g_head, w_out, wr_hi, wr_lo, rb, tm):
    b, s, d = x.shape
    tok = lambda w: pl.BlockSpec((None, tm, w), lambda i, j: (i, j, 0))
    full = lambda a: pl.BlockSpec(a.shape, lambda i, j: (0,) * a.ndim)
    return pl.pallas_call(
        _post_kernel,
        out_shape=(jax.ShapeDtypeStruct((b, s, d), f32),
                   jax.ShapeDtypeStruct((b, s, d), f32),
                   jax.ShapeDtypeStruct((N_EXPERTS, b * s), bf16),
                   jax.ShapeDtypeStruct((b, s, LANES), f32)),
        grid=(b, s // tm),
        in_specs=[tok(d), tok(D_ML), tok(D_DIL), tok(D_SB), pl.BlockSpec((None, 6, d), lambda i, j: (i, 0, 0)),
                  full(g_head), full(w_out), full(wr_hi), full(wr_lo), full(rb)],
        out_specs=(tok(d), tok(d),
                   pl.BlockSpec((N_EXPERTS, tm), lambda i, j: (0, i * (s // tm) + j)), tok(LANES)),
        compiler_params=_cparams(("parallel", "parallel")),
        name="merge_outproj_router",
    )(x, y_ml, y_dil, y_sb, mod, g_head, w_out, wr_hi, wr_lo, rb)


def _plan_kernel(selt_ref, pos0_ref, pos1_ref, tiles_ref, before_ref, cnt_ref, run_ref, off_ref, *, tp, ntile_pad):
    phase = pl.program_id(0)
    step = pl.program_id(1)
    sel = selt_ref[...]
    ones = jnp.ones((tp, LANES), bf16)

    @pl.when((phase == 0) & (step == 0))
    def _():
        cnt_ref[...] = jnp.zeros_like(cnt_ref)
        before_ref[...] = (_iota((tp, tp), 0) < _iota((tp, tp), 1)).astype(bf16)
        tiles_ref[...] = jnp.zeros_like(tiles_ref)

    @pl.when(phase == 0)
    def _():
        cnt_ref[...] += _dot(sel, ones)
        pos0_ref[...] = jnp.zeros_like(pos0_ref)
        pos1_ref[...] = jnp.zeros_like(pos1_ref)

    @pl.when((phase == 1) & (step == 0))
    def _():
        cnt = cnt_ref[...]
        padded = jnp.floor((cnt + (MOE_ROWS - 1.0)) * (1.0 / MOE_ROWS)) * MOE_ROWS
        rows, acc = [], jnp.zeros((1, LANES), f32)
        for e in range(N_EXPERTS):
            rows.append(acc)
            acc = acc + padded[e:e + 1, :]
        off = jnp.concatenate(rows, axis=0)
        off_ref[...] = off
        run_ref[...] = jnp.zeros_like(run_ref)
        start = _iota((1, ntile_pad), 1).astype(f32) * MOE_ROWS
        ends = (off + padded)[:, 0:1]
        owner = jnp.sum((start >= ends).astype(f32), axis=0, keepdims=True)
        tiles_ref[0:1, :] = jnp.minimum(owner, N_EXPERTS - 1.0).astype(jnp.int32)
        tiles_ref[1:2, :] = (start < acc[:, 0:1]).astype(jnp.int32)

    @pl.when(phase == 1)
    def _():
        pos = _dot(sel, before_ref[...]) + (off_ref[...] + run_ref[...])[:, 0:1]
        picked = sel > 0
        pos0_ref[...] = jnp.min(jnp.where(picked, pos, 1e9), axis=0, keepdims=True).astype(jnp.int32)
        pos1_ref[...] = jnp.max(jnp.where(picked, pos, -1.0), axis=0, keepdims=True).astype(jnp.int32)
        run_ref[...] += _dot(sel, ones)


def _plan(sel_t, tp, ntiles):
    ne, t = sel_t.shape
    ntile_pad = -(-ntiles // LANES) * LANES
    pos0, pos1, tiles = pl.pallas_call(
        functools.partial(_plan_kernel, tp=tp, ntile_pad=ntile_pad),
        out_shape=(jax.ShapeDtypeStruct((1, t), jnp.int32), jax.ShapeDtypeStruct((1, t), jnp.int32),
                   jax.ShapeDtypeStruct((8, ntile_pad), jnp.int32)),
        grid=(2, t // tp),
        in_specs=[pl.BlockSpec((ne, tp), lambda p, i: (0, i))],
        out_specs=(pl.BlockSpec((1, tp), lambda p, i: (0, i * p)), pl.BlockSpec((1, tp), lambda p, i: (0, i * p)),
                   pl.BlockSpec((8, ntile_pad), lambda p, i: (0, 0))),
        scratch_shapes=[pltpu.VMEM((tp, tp), bf16), pltpu.VMEM((ne, LANES), f32),
                        pltpu.VMEM((ne, LANES), f32), pltpu.VMEM((ne, LANES), f32)],
        compiler_params=_cparams(("arbitrary", "arbitrary")),
        name="moe_plan",
    )(sel_t)
    return pos0.reshape(t), pos1.reshape(t), tiles[0, :ntiles], tiles[1, :ntiles]


def _dispatch_kernel(pos0_ref, pos1_ref, h_ref, zero_hbm, xs_hbm, sem, *, td):
    del zero_hbm

    def copies(r):
        src = h_ref.at[pl.ds(r, 1)]
        return (pltpu.make_async_copy(src, xs_hbm.at[pl.ds(pos0_ref[r], 1)], sem.at[0]),
                pltpu.make_async_copy(src, xs_hbm.at[pl.ds(pos1_ref[r], 1)], sem.at[1]))

    def issue(r, carry):
        first, second = copies(r)
        first.start(priority=0)
        second.start(priority=1)
        return carry

    lax.fori_loop(0, td, issue, 0, unroll=8)

    def drain(r, carry):
        first, second = copies(r)
        first.wait()
        second.wait()
        return carry

    lax.fori_loop(0, td, drain, 0, unroll=8)


def _dispatch(h_rows, pos0, pos1, nrows, td):
    t, w = h_rows.shape
    smem = lambda: pl.BlockSpec((td,), lambda i: (i,), memory_space=pltpu.SMEM)
    anyspace = lambda: pl.BlockSpec(memory_space=pl.ANY)
    return pl.pallas_call(
        functools.partial(_dispatch_kernel, td=td),
        out_shape=jax.ShapeDtypeStruct((nrows, w), h_rows.dtype),
        grid=(t // td,),
        in_specs=[smem(), smem(), pl.BlockSpec((td, w), lambda i: (i, 0)), anyspace()],
        out_specs=anyspace(),
        scratch_shapes=[pltpu.SemaphoreType.DMA((2,))],
        input_output_aliases={3: 0},
        compiler_params=_cparams(("arbitrary",)),
        name="moe_dispatch",
    )(pos0, pos1, h_rows, jnp.zeros((nrows, w), h_rows.dtype))


def _experts_kernel(owner_ref, used_ref, xs_ref, wg_ref, wu_ref, wd_ref, ys_ref):
    tile = pl.program_id(0)

    @pl.when(used_ref[tile] > 0)
    def _():
        xb = xs_ref[...].astype(bf16)
        g = _dot(xb, wg_ref[...].astype(bf16))
        he = g * lax.logistic(g) * _dot(xb, wu_ref[...].astype(bf16))
        ys_ref[...] = _dot(he.astype(bf16), wd_ref[...].astype(bf16))

    @pl.when(used_ref[tile] == 0)
    def _():
        ys_ref[...] = jnp.zeros_like(ys_ref)


def _experts(xs, owner, used, wg, wu, wd):
    nrows, w = xs.shape
    _, d, de = wg.shape
    return pl.pallas_call(
        _experts_kernel,
        out_shape=jax.ShapeDtypeStruct((nrows, w), xs.dtype),
        grid_spec=pltpu.PrefetchScalarGridSpec(
            num_scalar_prefetch=2,
            grid=(nrows // MOE_ROWS,),
            in_specs=[pl.BlockSpec((MOE_ROWS, w), lambda g, own, use: (g, 0)),
                      pl.BlockSpec((None, d, de), lambda g, own, use: (own[g], 0, 0)),
                      pl.BlockSpec((None, d, de), lambda g, own, use: (own[g], 0, 0)),
                      pl.BlockSpec((None, de, d), lambda g, own, use: (own[g], 0, 0))],
            out_specs=pl.BlockSpec((MOE_ROWS, w), lambda g, own, use: (g, 0))),
        compiler_params=_cparams(("arbitrary",)),
        name="moe_experts",
    )(owner, used, xs, wg, wu, wd)


def _combine_kernel(pos0_ref, pos1_ref, npos0_ref, npos1_ref, ys_hbm, x_ref, w_ref, mod_ref, gf_ref, o_ref,
                    buf, sem, *, tc, final):
    step = pl.program_id(0)
    nstep = pl.num_programs(0)
    slot = step % 2

    def rows(p0_ref, p1_ref, slot_, start):
        def one(r, carry):
            for k, p_ref in enumerate((p0_ref, p1_ref)):
                cp = pltpu.make_async_copy(ys_hbm.at[pl.ds(p_ref[r], 1)], buf.at[slot_, k, pl.ds(r, 1)],
                                           sem.at[slot_, k])
                cp.start(priority=k) if start else cp.wait()
            return carry
        lax.fori_loop(0, tc, one, 0, unroll=8)

    @pl.when(step == 0)
    def _():
        rows(pos0_ref, pos1_ref, 0, True)

    @pl.when(step + 1 < nstep)
    def _():
        rows(npos0_ref, npos1_ref, 1 - slot, True)

    rows(pos0_ref, pos1_ref, slot, False)
    w = w_ref[...]
    y = w[:, 0:1] * buf[slot, 0] + w[:, 1:2] * buf[slot, 1]
    xn = x_ref[...] + mod_ref[...][5:6, :] * y
    if final:
        xn = _rms(xn) * gf_ref[...]
    o_ref[...] = xn


def _combine(ys, pos0, pos1, x, w_pair, mod, g_final, tc, final):
    b, s, d = x.shape
    t = b * s
    nstep = t // tc
    per_batch = s // tc
    cur = lambda: pl.BlockSpec((tc,), lambda i: (i,), memory_space=pltpu.SMEM)
    nxt = lambda: pl.BlockSpec((tc,), lambda i: (jnp.minimum(i + 1, nstep - 1),), memory_space=pltpu.SMEM)
    out = pl.pallas_call(
        functools.partial(_combine_kernel, tc=tc, final=final),
        out_shape=jax.ShapeDtypeStruct((t, d), f32),
        grid=(nstep,),
        in_specs=[cur(), cur(), nxt(), nxt(), pl.BlockSpec(memory_space=pl.ANY),
                  pl.BlockSpec((tc, d), lambda i: (i, 0)),
                  pl.BlockSpec((tc, LANES), lambda i: (i, 0)),
                  pl.BlockSpec((None, 6, d), lambda i: (i // per_batch, 0, 0)),
                  pl.BlockSpec((1, d), lambda i: (0, 0))],
        out_specs=pl.BlockSpec((tc, d), lambda i: (i, 0)),
        scratch_shapes=[pltpu.VMEM((2, 2, tc, ys.shape[1]), ys.dtype), pltpu.SemaphoreType.DMA((2, 2))],
        compiler_params=_cparams(("arbitrary",)),
        name="moe_combine_final" if final else "moe_combine",
    )(pos0, pos1, pos0, pos1, ys, x.reshape(t, d), w_pair.reshape(t, LANES), mod, g_final)
    return out.reshape(b, s, d)


def _moe(x, h_rows, sel_t, w_pair, mod, wg, wu, wd, g_final, final):
    b, s, d = x.shape
    t = b * s
    nrows = 2 * t + N_EXPERTS * MOE_ROWS
    pos0, pos1, owner, used = _plan(sel_t, min(1024, t), nrows // MOE_ROWS)
    xs = _dispatch(h_rows.reshape(t, d), pos0, pos1, nrows, min(1024, t))
    ys = _experts(xs, owner, used, wg, wu, wd)
    return _combine(ys, pos0, pos1, x, w_pair, mod, g_final, min(512, s), final)


def _block_diag(w):
    n, a, c = w.shape
    out = jnp.zeros((n * a, n * c), w.dtype)
    for i in range(n):
        out = out.at[i * a:(i + 1) * a, i * c:(i + 1) * c].set(w[i])
    return out


def kernel(x, c, w_in, conv_w, conv_b, w_mq, w_mk, w_mv, gate_bias, g_head, w_out, w_ada, b_ada,
           w_router, router_bias, w_gate_e, w_up_e, w_down_e, g_final):
    b, s, d = x.shape
    depth = w_in.shape[0]
    qscale = HEAD_DIM ** -0.5
    tm = min(512, s)

    mod_all = _ada(c, w_ada, b_ada).reshape(depth, b, 6, d)

    wr = jnp.zeros((d, EPG, LANES), f32).at[:, :, :N_GROUPS].set(
        w_router.reshape(d, N_GROUPS, EPG).transpose(0, 2, 1)).reshape(d, EPG * LANES)
    wr_hi = wr.astype(bf16)
    wr_lo = (wr - wr_hi.astype(f32)).astype(bf16)
    rb = jnp.zeros((1, EPG, LANES), f32).at[0, :, :N_GROUPS].set(
        router_bias.reshape(N_GROUPS, EPG).T).reshape(1, EPG * LANES)

    slope = jnp.exp2(-ALIBI_MAX_BIAS * jnp.arange(1, N_DIL + 1, dtype=f32) / N_DIL)
    slopes = jnp.broadcast_to(slope.reshape(N_DIL // 2, 2, 1, 1), (N_DIL // 2, 2, 8, LANES))
    g_final2 = g_final.reshape(1, d)

    off_i = 2 * D_ML
    off_dil = off_i + 2 * N_ML
    off_sb = off_dil + 3 * D_DIL

    for l in range(depth):
        w = w_in[l]
        w_ml = jnp.concatenate([w[:, :off_i], w[:, off_i:off_dil],
                                jnp.zeros((d, LANES - 2 * N_ML), f32)], axis=1).astype(bf16)
        w_gt = w[:, off_i:off_dil].T.astype(bf16)
        wdil = w[:, off_dil:off_sb]
        w_dil = jnp.concatenate([wdil[:, :D_DIL] * qscale, wdil[:, D_DIL:]], axis=1).astype(bf16)
        wsb = w[:, off_sb:]
        w_sb = jnp.concatenate([wsb[:, :D_SB] * qscale, wsb[:, D_SB:]], axis=1).astype(bf16)
        mod = mod_all[l]

        u_ml, u_dil, u_sb, g_row = _inproj(x, mod, w_ml, w_dil, w_sb, w_gt, tm)

        gb = gate_bias[l]
        gb_col = jnp.zeros((1, LANES), f32).at[0, :2 * N_ML].set(gb.reshape(-1))
        gb_row = gb.reshape(2 * N_ML, 1)
        y_ml = _mlstm(u_ml, g_row, conv_w[l], conv_b[l].reshape(1, D_ML),
                      _block_diag(w_mq[l]).astype(bf16), (_block_diag(w_mk[l]) * qscale).astype(bf16),
                      _block_diag(w_mv[l]).astype(bf16), gb_col, gb_row)
        y_dil = _dilated(u_dil, slopes)
        y_sb = _stickbreak(u_sb)

        x, h_rows, sel_t, w_pair = _post(x, y_ml, y_dil, y_sb, mod, g_head[l].reshape(1, d),
                                           w_out[l].astype(bf16), wr_hi, wr_lo, rb, tm)
        x = _moe(x, h_rows, sel_t, w_pair, mod, w_gate_e[l], w_up_e[l], w_down_e[l], g_final2,
                 final=(l == depth - 1))
    return x
```

```python
import functools

import jax
import jax.numpy as jnp
from jax import lax
from jax.experimental import pallas as pl
from jax.experimental.pallas import tpu as pltpu

HEAD_DIM = 64
N_ML = 4
N_DIL = 6
N_SB = 6
D_ML = N_ML * HEAD_DIM
D_DIL = N_DIL * HEAD_DIM
D_SB = N_SB * HEAD_DIM
CONV_WIDTH = 4
CHUNK = 128
DILATIONS = (1, 4, 16)
DIL_SPAN = 128
ALIBI_MAX_BIAS = 8.0
N_EXPERTS = 16
N_GROUPS = 4
EPG = N_EXPERTS // N_GROUPS
EPS = 1e-6
LANES = 128
ML_COLS = 2 * D_ML + LANES
SB_DEAD_LOG = -104.0
SB_NO_KEYS = -1e30
SB_QBLOCKS = 4
SB_GROUP = 2
DIL_UNITS = 4
MOE_ROWS = 512
MOE_TOK = 256
MOE_CHUNK = 8
MOE_LOCAL = 2 * MOE_TOK + N_EXPERTS * MOE_CHUNK
VMEM_LIMIT = 56 * 1024 * 1024

_NT = (((1,), (1,)), ((), ()))
_TN = (((0,), (0,)), ((), ()))

f32 = jnp.float32
bf16 = jnp.bfloat16


def _dot(a, b):
    return jnp.dot(a, b, preferred_element_type=f32)


def _dot_nt(a, b):
    return lax.dot_general(a, b, _NT, preferred_element_type=f32)


def _split_bf16(x):
    hi = x.astype(bf16)
    lo = (x - hi.astype(f32)).astype(bf16)
    return hi, lo


def _iota(shape, dim):
    return lax.broadcasted_iota(jnp.int32, shape, dim)


def _rms(x):
    return x * lax.rsqrt(jnp.mean(x * x, axis=-1, keepdims=True) + EPS)


def _cparams(sem):
    return pltpu.CompilerParams(dimension_semantics=sem, vmem_limit_bytes=VMEM_LIMIT)


def _ada_kernel(c_ref, w_ref, b_ref, o_ref):
    c = c_ref[...]
    c_act = (c * lax.logistic(c)).astype(bf16)
    o_ref[...] = _dot(c_act, w_ref[...].astype(bf16)) + b_ref[...]


def _ada(c, w_ada, b_ada):
    depth, d, nd = w_ada.shape
    b = c.shape[0]
    nblk = nd // d
    return pl.pallas_call(
        _ada_kernel,
        out_shape=jax.ShapeDtypeStruct((depth, b, nd), f32),
        grid=(depth, nblk),
        in_specs=[pl.BlockSpec((b, d), lambda l, j: (0, 0)),
                  pl.BlockSpec((None, d, d), lambda l, j: (l, 0, j)),
                  pl.BlockSpec((None, 1, d), lambda l, j: (l, 0, j))],
        out_specs=pl.BlockSpec((None, b, d), lambda l, j: (l, 0, j)),
        compiler_params=_cparams(("parallel", "parallel")),
        name="ada_mod",
    )(c, w_ada, b_ada.reshape(depth, 1, nd))


def _inproj_kernel(x_ref, mod_ref, wml_ref, wdil_ref, wsb_ref, wgt_ref,
                   uml_ref, udil_ref, usb_ref, grow_ref):
    x = x_ref[...]
    mod = mod_ref[...]
    h = (_rms(x) * (1.0 + mod[1:2, :]) + mod[0:1, :]).astype(bf16)
    uml_ref[...] = _dot(h, wml_ref[...])
    udil_ref[...] = _dot(h, wdil_ref[...])
    usb_ref[...] = _dot(h, wsb_ref[...]).astype(bf16)
    grow_ref[...] = _dot_nt(wgt_ref[...], h)


def _inproj(x, mod, w_ml, w_dil, w_sb, w_gt, tm):
    b, s, d = x.shape
    full = lambda a: pl.BlockSpec(a.shape, lambda i, j: (0,) * a.ndim)
    return pl.pallas_call(
        _inproj_kernel,
        out_shape=(jax.ShapeDtypeStruct((b, s, ML_COLS), f32),
                   jax.ShapeDtypeStruct((b, s, 3 * D_DIL), f32),
                   jax.ShapeDtypeStruct((b, s, 3 * D_SB), bf16),
                   jax.ShapeDtypeStruct((b, 2 * N_ML, s), f32)),
        grid=(b, s // tm),
        in_specs=[pl.BlockSpec((None, tm, d), lambda i, j: (i, j, 0)),
                  pl.BlockSpec((None, 6, d), lambda i, j: (i, 0, 0)),
                  full(w_ml), full(w_dil), full(w_sb), full(w_gt)],
        out_specs=(pl.BlockSpec((None, tm, ML_COLS), lambda i, j: (i, j, 0)),
                   pl.BlockSpec((None, tm, 3 * D_DIL), lambda i, j: (i, j, 0)),
                   pl.BlockSpec((None, tm, 3 * D_SB), lambda i, j: (i, j, 0)),
                   pl.BlockSpec((None, 2 * N_ML, tm), lambda i, j: (i, 0, j))),
        compiler_params=_cparams(("parallel", "parallel")),
        name="norm_inproj",
    )(x, mod, w_ml, w_dil, w_sb, w_gt)


def _mlstm_kernel(u_ref, grow_ref, convw_ref, convb_ref, wq_ref, wk_ref, wv_ref, gbc_ref, gbr_ref,
                  y_ref, prev_ref, c_ref, n_ref, m_ref):
    L = CHUNK

    @pl.when(pl.program_id(1) == 0)
    def _():
        prev_ref[...] = jnp.zeros_like(prev_ref)
        c_ref[...] = jnp.zeros_like(c_ref)
        n_ref[...] = jnp.zeros_like(n_ref)
        m_ref[...] = jnp.zeros_like(m_ref)

    u = u_ref[...]
    xm = u[:, :D_ML]
    og = u[:, D_ML:2 * D_ML]
    gcol = u[:, 2 * D_ML:] + gbc_ref[...]
    grow = grow_ref[...] + gbr_ref[...]

    prev = prev_ref[...]
    row = _iota((L, D_ML), 0)
    convw = convw_ref[...]
    xc = xm * convw[CONV_WIDTH - 1:CONV_WIDTH, :]
    for sft in range(1, CONV_WIDTH):
        shifted = jnp.where(row < sft, pltpu.roll(prev, sft, 0), pltpu.roll(xm, sft, 0))
        xc = xc + shifted * convw[CONV_WIDTH - 1 - sft:CONV_WIDTH - sft, :]
    prev_ref[...] = xm
    xc = xc + convb_ref[...]
    xc = xc * lax.logistic(xc)
    xcb = xc.astype(bf16)
    q = _dot(xcb, wq_ref[...])
    k = _dot(xcb, wk_ref[...])
    v = _dot(xm.astype(bf16), wv_ref[...])
    kb = k.astype(bf16)
    vb = v.astype(bf16)

    def logsig(t):
        return jnp.minimum(t, 0.0) - jnp.log1p(jnp.exp(-jnp.abs(t)))

    lf_col = logsig(gcol)
    lf_row = logsig(grow)
    ltri = (_iota((L, L), 1) <= _iota((L, L), 0)).astype(bf16)
    utri = (_iota((L, L), 0) <= _iota((L, L), 1)).astype(bf16)
    ch, cl = _split_bf16(lf_col)
    cum_col = _dot(ltri, ch) + _dot(ltri, cl)
    rh, rl = _split_bf16(lf_row)
    cum_row = _dot(rh, utri) + _dot(rl, utri)

    causal = _iota((L, L), 1) <= _iota((L, L), 0)
    lane = _iota((1, D_ML), 1)
    c_bd = c_ref[...]
    n_st = n_ref[...]
    m_st = m_ref[...]
    c_bdb = c_bd.astype(bf16)

    heads = range(N_ML)
    mhs = [(lane >= hd * HEAD_DIM) & (lane < (hd + 1) * HEAD_DIM) for hd in heads]
    qhs = [jnp.where(mhs[hd], q, 0.0) for hd in heads]
    qhb = [qh.astype(bf16) for qh in qhs]
    qk = [_dot_nt(qhb[hd], kb) for hd in heads]
    qc = [_dot(qhb[hd], c_bdb) for hd in heads]
    m_ts, inters, sws = [], [], []
    wfull = jnp.zeros((L, D_ML), f32)
    decay_l = jnp.zeros((1, D_ML), f32)
    mnew_l = jnp.zeros((1, D_ML), f32)
    for hd in heads:
        cumc = cum_col[:, N_ML + hd:N_ML + hd + 1]
        cumr = cum_row[N_ML + hd:N_ML + hd + 1, :]
        igr = grow[hd:hd + 1, :]
        igc = gcol[:, hd:hd + 1]
        m_prev = m_st[:, hd * HEAD_DIM:hd * HEAD_DIM + 1]
        log_d = jnp.where(causal, cumc - cumr + igr, -jnp.inf)
        log_inter = cumc + m_prev
        m_t = jnp.maximum(log_inter, jnp.max(log_d, axis=-1, keepdims=True))
        m_ts.append(m_t)
        inters.append(jnp.exp(log_inter - m_t))
        sws.append(qk[hd] * jnp.exp(log_d - m_t))
        last = cumc[L - 1:L, :]
        log_w = last - cumc + igc
        m_new = jnp.maximum(last + m_prev, jnp.max(log_w, axis=0, keepdims=True))
        wfull = jnp.where(mhs[hd], jnp.exp(log_w - m_new), wfull)
        decay_l = jnp.where(mhs[hd], jnp.exp(last + m_prev - m_new), decay_l)
        mnew_l = jnp.where(mhs[hd], m_new, mnew_l)
    swv = [_dot(sws[hd].astype(bf16), vb) for hd in heads]
    hmix = jnp.zeros((L, D_ML), f32)
    for hd in heads:
        num = inters[hd] * qc[hd] + swv[hd]
        den = (inters[hd] * jnp.sum(qhs[hd] * n_st, axis=-1, keepdims=True)
               + jnp.sum(sws[hd], axis=-1, keepdims=True))
        hout = num / jnp.maximum(jnp.abs(den), jnp.exp(-m_ts[hd]))
        hmix = jnp.where(mhs[hd], hout, hmix)

    kw = k * wfull
    upd = lax.dot_general(kw.astype(bf16), vb, _TN, preferred_element_type=f32)
    same_head = (_iota((D_ML, D_ML), 0) // HEAD_DIM) == (_iota((D_ML, D_ML), 1) // HEAD_DIM)
    c_ref[...] = decay_l * c_bd + jnp.where(same_head, upd, 0.0)
    n_ref[...] = decay_l * n_st + jnp.sum(kw, axis=0, keepdims=True)
    m_ref[...] = mnew_l
    y_ref[...] = lax.logistic(og) * hmix


def _mlstm(u_ml, g_row, conv_w, conv_b, wq_bd, wk_bd, wv_bd, gb_col, gb_row):
    b, s, _ = u_ml.shape
    full = lambda a: pl.BlockSpec(a.shape, lambda i, j: (0,) * a.ndim)
    return pl.pallas_call(
        _mlstm_kernel,
        out_shape=jax.ShapeDtypeStruct((b, s, D_ML), f32),
        grid=(b, s // CHUNK),
        in_specs=[pl.BlockSpec((None, CHUNK, ML_COLS), lambda i, j: (i, j, 0)),
                  pl.BlockSpec((None, 2 * N_ML, CHUNK), lambda i, j: (i, 0, j)),
                  full(conv_w), full(conv_b), full(wq_bd), full(wk_bd), full(wv_bd), full(gb_col), full(gb_row)],
        out_specs=pl.BlockSpec((None, CHUNK, D_ML), lambda i, j: (i, j, 0)),
        scratch_shapes=[pltpu.VMEM((CHUNK, D_ML), f32), pltpu.VMEM((D_ML, D_ML), f32),
                        pltpu.VMEM((1, D_ML), f32), pltpu.VMEM((1, D_ML), f32)],
        compiler_params=_cparams(("parallel", "arbitrary")),
        name="mlstm",
    )(u_ml, g_row, conv_w, conv_b, wq_bd, wk_bd, wv_bd, gb_col, gb_row)


def _rows(start, size, stride):
    return pl.ds(start, size) if stride == 1 else pl.ds(start, size, stride=stride)


def _dil_kernel(q_ref, k_ref, v_ref, slope_ref, o_ref, *scratch, seq, tile):
    npat = len(DILATIONS)
    kd = scratch[0:npat]
    vd = scratch[npat:2 * npat]
    op = scratch[2 * npat:3 * npat]
    ls = scratch[3 * npat:4 * npat]
    bias_ref = scratch[4 * npat]
    L = CHUNK
    ti = pl.program_id(2)
    lane = _iota((1, LANES), 1)
    m0 = lane < HEAD_DIM

    @pl.when(ti == 0)
    def _():
        row = _iota((L, L), 0)
        col = _iota((L, L), 1)
        for p, d in enumerate(DILATIONS):
            per = seq // d
            for r in range(d):
                kd[p][r * per:(r + 1) * per, :] = k_ref[_rows(r, per, d), :].astype(bf16)
                vd[p][r * per:(r + 1) * per, :] = v_ref[_rows(r, per, d), :].astype(bf16)
            for j in range(2):
                sl = slope_ref[j][0:1, :] * float(d)
                bias_ref[p, j, 0] = jnp.where(col >= row, -(sl * (row - col + DIL_SPAN).astype(f32)), -jnp.inf)
                bias_ref[p, j, 1] = jnp.full((L, L), -jnp.inf, f32)
                bias_ref[p, j, 2] = jnp.where(col <= row, -(sl * (row - col).astype(f32)), -jnp.inf)

    nunits = tile // L
    for p, d in enumerate(DILATIONS):
        per = seq // d
        blocks_per_res = tile // d // L

        def body(it, carry, p=p, d=d, per=per, blocks_per_res=blocks_per_res):
            units = []
            for c in range(DIL_UNITS):
                u = it * DIL_UNITS + c
                r = u // blocks_per_res
                blk = u % blocks_per_res
                start = blk * (L * d) + r
                q = q_ref[_rows(start, L, d), :].astype(bf16)
                m_first = (ti * tile) // d + blk * L
                base = r * per + m_first
                has_prev = m_first > 0
                cur0 = pl.multiple_of(base, L)
                prev0 = pl.multiple_of(jnp.where(has_prev, base - L, base), L)
                units.append(dict(start=start, q=q, pk=jnp.where(has_prev, 0, 1),
                                  kc=kd[p][pl.ds(cur0, L), :], vc=vd[p][pl.ds(cur0, L), :],
                                  kp=kd[p][pl.ds(prev0, L), :], vp=vd[p][pl.ds(prev0, L), :]))
            chains = [(un, j) for un in units for j in range(2)]
            scores = []
            for un, j in chains:
                qj = jnp.where(m0 if j == 0 else jnp.logical_not(m0), un["q"], jnp.zeros_like(un["q"]))
                scores.append((_dot_nt(qj, un["kp"]) + bias_ref[p, j, un["pk"]],
                               _dot_nt(qj, un["kc"]) + bias_ref[p, j, 2]))
            probs = []
            for sp, sc in scores:
                mx = jnp.max(jnp.maximum(sp, sc), axis=-1, keepdims=True)
                pp = jnp.exp(sp - mx)
                pc = jnp.exp(sc - mx)
                den = jnp.sum(pp + pc, axis=-1, keepdims=True)
                probs.append((pp.astype(bf16), pc.astype(bf16), den, mx + jnp.log(den)))
            outs = [(_dot(pp, un["vp"]) + _dot(pc, un["vc"])) / den
                    for (un, j), (pp, pc, den, _) in zip(chains, probs)]
            for c, un in enumerate(units):
                op[p][_rows(un["start"], L, d), :] = jnp.where(m0, outs[2 * c], outs[2 * c + 1])
                ls[p][_rows(un["start"], L, d), :] = jnp.where(m0, probs[2 * c][3], probs[2 * c + 1][3])
            return carry

        lax.fori_loop(0, nunits // DIL_UNITS, body, 0)

    l0, l1, l2 = ls[0][...], ls[1][...], ls[2][...]
    mx = jnp.maximum(jnp.maximum(l0, l1), l2)
    e0, e1, e2 = jnp.exp(l0 - mx), jnp.exp(l1 - mx), jnp.exp(l2 - mx)
    o_ref[...] = ((e0 * op[0][...] + e1 * op[1][...] + e2 * op[2][...]) / (e0 + e1 + e2)).astype(o_ref.dtype)


def _dilated(u_dil, slopes):
    b, s, _ = u_dil.shape
    npair = N_DIL // 2
    tile = DILATIONS[-1] * CHUNK
    assert s % tile == 0 and (tile // CHUNK) % DIL_UNITS == 0
    seq_buf = lambda dt: pltpu.VMEM((s, LANES), dt)
    tile_buf = pltpu.VMEM((tile, LANES), f32)
    npat = len(DILATIONS)
    return pl.pallas_call(
        functools.partial(_dil_kernel, seq=s, tile=tile),
        out_shape=jax.ShapeDtypeStruct((b, s, D_DIL), bf16),
        grid=(b, npair, s // tile),
        in_specs=[pl.BlockSpec((None, tile, LANES), lambda i, h, n: (i, n, h)),
                  pl.BlockSpec((None, s, LANES), lambda i, h, n: (i, 0, npair + h)),
                  pl.BlockSpec((None, s, LANES), lambda i, h, n: (i, 0, 2 * npair + h)),
                  pl.BlockSpec((None, 2, 8, LANES), lambda i, h, n: (h, 0, 0, 0))],
        out_specs=pl.BlockSpec((None, tile, LANES), lambda i, h, n: (i, n, h)),
        scratch_shapes=([seq_buf(bf16)] * (2 * npat) + [tile_buf] * (2 * npat)
                        + [pltpu.VMEM((npat, 2, 3, CHUNK, CHUNK), f32)]),
        compiler_params=_cparams(("parallel", "parallel", "arbitrary")),
        name="dilated",
    )(u_dil, u_dil, u_dil, slopes)


def _sb_kernel(q_ref, k_ref, v_ref, o_ref, qm_ref, r_ref, acc_ref, *, nq):
    L = CHUNK
    tile0 = pl.program_id(1) * nq
    row = _iota((L, L), 0)
    col = _iota((L, L), 1)
    strict = col < row
    lane = _iota((1, LANES), 1)
    m0 = lane < HEAD_DIM
    npair = N_SB // 2
    csum = jnp.concatenate([(_iota((L, L), 0) > _iota((L, L), 1)).astype(bf16), jnp.ones((L, L), bf16)], axis=1)

    for qb in range(nq):
        for hp in range(npair):
            q = q_ref[qb * L:(qb + 1) * L, hp * LANES:(hp + 1) * LANES]
            qm_ref[(qb * npair + hp) * 2] = jnp.where(m0, q, jnp.zeros_like(q))
            qm_ref[(qb * npair + hp) * 2 + 1] = jnp.where(m0, jnp.zeros_like(q), q)

    def sweep(it, diagonal):
        rmax = None
        for g0 in range(0, nq, SB_GROUP):
            chains = []
            for qb in range(g0, g0 + SB_GROUP):
                qi = tile0 + qb
                k0 = pl.multiple_of(jnp.maximum(qi - it, 0) * L, L)
                out_of_keys = jnp.where(qi - it <= 0, SB_NO_KEYS, 0.0)
                for hp in range(npair):
                    kblk = k_ref[pl.ds(k0, L), hp * LANES:(hp + 1) * LANES]
                    vblk = v_ref[pl.ds(k0, L), hp * LANES:(hp + 1) * LANES]
                    for j in range(2):
                        chains.append(((qb * npair + hp) * 2 + j, kblk, vblk, out_of_keys))
            zs = [_dot_nt(qm_ref[c], kblk) for c, kblk, _, _ in chains]
            zl, lkb = [], []
            for z in zs:
                lk = -(jnp.maximum(z, 0.0) + jnp.log(1.0 + jnp.exp(-jnp.abs(z))))
                zl.append(z + lk)
                lkb.append((jnp.where(strict, lk, 0.0) if diagonal else lk).astype(bf16))
            ts = [_dot(x, csum) for x in lkb]
            avs = []
            for (c, _, _, out_of_keys), s, t in zip(chains, zl, ts):
                if diagonal:
                    a = jnp.where(strict, jnp.exp(s + t[:, :L]), 0.0)
                    rs = t[:, L:] + out_of_keys
                else:
                    a = jnp.exp(s + r_ref[c] + t[:, :L])
                    rs = r_ref[c] + t[:, L:] + out_of_keys
                avs.append(a.astype(bf16))
                r_ref[c] = rs
                rmax = rs if rmax is None else jnp.maximum(rmax, rs)
            for (c, _, vblk, _), a in zip(chains, avs):
                if diagonal:
                    acc_ref[c] = _dot(a, vblk)
                else:
                    acc_ref[c] += _dot(a, vblk)
        return jnp.max(rmax) > SB_DEAD_LOG

    alive0 = sweep(0, True)

    def body(carry):
        _, it = carry
        return sweep(it, False), it + 1

    lax.while_loop(lambda carry: carry[0], body, (alive0, jnp.int32(1)))

    for qb in range(nq):
        for hp in range(npair):
            c = (qb * npair + hp) * 2
            o_ref[qb * L:(qb + 1) * L, hp * LANES:(hp + 1) * LANES] = (
                jnp.where(m0, acc_ref[c], acc_ref[c + 1]).astype(o_ref.dtype))


def _stickbreak(u_sb):
    b, s, _ = u_sb.shape
    nq = SB_QBLOCKS
    tq = nq * CHUNK
    nchain = nq * N_SB
    return pl.pallas_call(
        functools.partial(_sb_kernel, nq=nq),
        out_shape=jax.ShapeDtypeStruct((b, s, D_SB), bf16),
        grid=(b, s // tq),
        in_specs=[pl.BlockSpec((None, tq, D_SB), lambda i, n: (i, n, 0)),
                  pl.BlockSpec((None, s, D_SB), lambda i, n: (i, 0, 1)),
                  pl.BlockSpec((None, s, D_SB), lambda i, n: (i, 0, 2))],
        out_specs=pl.BlockSpec((None, tq, D_SB), lambda i, n: (i, n, 0)),
        scratch_shapes=[pltpu.VMEM((nchain, CHUNK, LANES), bf16),
                        pltpu.VMEM((nchain, CHUNK, CHUNK), f32),
                        pltpu.VMEM((nchain, CHUNK, LANES), f32)],
        compiler_params=_cparams(("parallel", "arbitrary")),
        name="stickbreak",
    )(u_sb, u_sb, u_sb)


def _post_kernel(x_ref, yml_ref, ydil_ref, ysb_ref, mod_ref, gh_ref, wout_ref, wr_hi_ref, wr_lo_ref, rb_ref,
                 xo_ref, h_ref, selt_ref, w_ref):
    mod = mod_ref[...]
    y = jnp.concatenate([yml_ref[...], ydil_ref[...].astype(f32), ysb_ref[...].astype(f32)], axis=1)
    pair = ((_iota((LANES, LANES), 0) // HEAD_DIM) == (_iota((LANES, LANES), 1) // HEAD_DIM)).astype(bf16)
    cols = []
    for cblk in range(y.shape[1] // LANES):
        yb = y[:, cblk * LANES:(cblk + 1) * LANES]
        hi, lo = _split_bf16(yb * yb)
        ms = (_dot(hi, pair) + _dot(lo, pair)) * (1.0 / HEAD_DIM)
        cols.append(yb * lax.rsqrt(ms + EPS))
    yn = (jnp.concatenate(cols, axis=1) * gh_ref[...]).astype(bf16)
    xn = x_ref[...] + mod[2:3, :] * _dot(yn, wout_ref[...])
    xo_ref[...] = xn
    h2 = _rms(xn) * (1.0 + mod[4:5, :]) + mod[3:4, :]
    hh, hl = _split_bf16(h2)
    h_ref[...] = h2
    logits = _dot(hh, wr_hi_ref[...]) + _dot(hl, wr_hi_ref[...]) + _dot(hh, wr_lo_ref[...])
    lane = _iota((1, LANES), 1)
    glane = lane < N_GROUPS
    sc = [lax.logistic(logits[:, j * LANES:(j + 1) * LANES]) for j in range(EPG)]
    bs = [jnp.where(glane, sc[j] + rb_ref[:, j * LANES:(j + 1) * LANES], -jnp.inf) for j in range(EPG)]
    hi01, lo01 = jnp.maximum(bs[0], bs[1]), jnp.minimum(bs[0], bs[1])
    hi23, lo23 = jnp.maximum(bs[2], bs[3]), jnp.minimum(bs[2], bs[3])
    top1 = jnp.maximum(hi01, hi23)
    top2 = jnp.maximum(jnp.minimum(hi01, hi23), jnp.maximum(lo01, lo23))
    gscore = jnp.where(glane, top1 + top2, -jnp.inf)
    gmax = jnp.max(gscore, axis=-1, keepdims=True)
    gidx = jnp.min(jnp.where(gscore == gmax, lane, LANES), axis=-1, keepdims=True)
    gsel = lane == gidx
    picks, vals = [], []
    for j in range(EPG):
        rank = jnp.zeros(bs[j].shape, jnp.int32)
        for jj in range(EPG):
            if jj != j:
                ahead = (bs[jj] >= bs[j]) if jj < j else (bs[jj] > bs[j])
                rank = rank + ahead.astype(jnp.int32)
        picks.append(gsel & (rank < 2))
        vals.append(jnp.sum(jnp.where(gsel, sc[j], 0.0), axis=-1, keepdims=True))
    erow = _iota((N_EXPERTS, LANES), 0)
    gcol = _iota((N_EXPERTS, LANES), 1)
    sel_t = jnp.zeros((N_EXPERTS, x_ref.shape[0]), f32)
    for j in range(EPG):
        place = ((erow == gcol * EPG + j) & (gcol < N_GROUPS)).astype(bf16)
        sel_t = sel_t + _dot_nt(place, picks[j].astype(bf16))
    selt_ref[...] = sel_t.astype(selt_ref.dtype)
    seen = jnp.zeros_like(vals[0])
    w_lo = jnp.zeros_like(vals[0])
    w_hi = jnp.zeros_like(vals[0])
    for j in range(EPG):
        picked = jnp.sum(picks[j].astype(f32), axis=-1, keepdims=True) > 0.0
        w_lo = w_lo + jnp.where(picked & (seen == 0.0), vals[j], 0.0)
        w_hi = w_hi + jnp.where(picked & (seen == 1.0), vals[j], 0.0)
        seen = seen + picked.astype(f32)
    tot = w_lo + w_hi
    w_ref[...] = jnp.where(lane == 0, w_lo / tot, jnp.where(lane == 1, w_hi / tot, 0.0))


def _post(x, y_ml, y_dil, y_sb, mod, g_head, w_out, wr_hi, wr_lo, rb, tm):
    b, s, d = x.shape
    tok = lambda w: pl.BlockSpec((None, tm, w), lambda i, j: (i, j, 0))
    full = lambda a: pl.BlockSpec(a.shape, lambda i, j: (0,) * a.ndim)
    return pl.pallas_call(
        _post_kernel,
        out_shape=(jax.ShapeDtypeStruct((b, s, d), f32),
                   jax.ShapeDtypeStruct((b, s, d), f32),
                   jax.ShapeDtypeStruct((N_EXPERTS, b * s), bf16),
                   jax.ShapeDtypeStruct((b, s, LANES), f32)),
        grid=(b, s // tm),
        in_specs=[tok(d), tok(D_ML), tok(D_DIL), tok(D_SB), pl.BlockSpec((None, 6, d), lambda i, j: (i, 0, 0)),
                  full(g_head), full(w_out), full(wr_hi), full(wr_lo), full(rb)],
        out_specs=(tok(d), tok(d),
                   pl.BlockSpec((N_EXPERTS, tm), lambda i, j: (0, i * (s // tm) + j)), tok(LANES)),
        compiler_params=_cparams(("parallel", "parallel")),
        name="merge_outproj_router",
    )(x, y_ml, y_dil, y_sb, mod, g_head, w_out, wr_hi, wr_lo, rb)


def _plan_kernel(selt_ref, lpos0_ref, lpos1_ref, base_ref, nchunk_ref, lofs_ref, tiles_ref, tails_ref,
                 before_ref, cnt_ref, lofs_sc, *, ntw, ntile_pad, nrow_tiles):
    phase = pl.program_id(0)
    step = pl.program_id(1)
    tk = MOE_TOK
    sel = selt_ref[...]
    tlane = _iota((1, ntw), 1)

    @pl.when((phase == 0) & (step == 0))
    def _():
        cnt_ref[...] = jnp.zeros_like(cnt_ref)
        before_ref[...] = (_iota((tk, tk), 0) < _iota((tk, tk), 1)).astype(bf16)

    @pl.when(phase == 0)
    def _():
        here = _dot(sel, jnp.ones((tk, ntw), bf16))
        cnt_ref[...] = jnp.where(tlane == step, here, cnt_ref[...])
        lpos0_ref[...] = jnp.zeros_like(lpos0_ref)
        lpos1_ref[...] = jnp.zeros_like(lpos1_ref)

    @pl.when((phase == 1) & (step == 0))
    def _():
        cnt = cnt_ref[...]
        nch = jnp.floor((cnt + (MOE_CHUNK - 1.0)) * (1.0 / MOE_CHUNK))
        nchb = nch.astype(bf16)
        earlier_tiles = (_iota((ntw, ntw), 0) < _iota((ntw, ntw), 1)).astype(bf16)
        earlier_experts = (_iota((N_EXPERTS, N_EXPERTS), 1) < _iota((N_EXPERTS, N_EXPERTS), 0)).astype(bf16)
        in_expert = _dot(nchb, earlier_tiles) * MOE_CHUNK
        lofs = _dot(earlier_experts, nchb) * MOE_CHUNK
        total = _dot(nchb, jnp.ones((ntw, LANES), bf16)) * MOE_CHUNK
        padded = jnp.floor((total + (MOE_ROWS - 1.0)) * (1.0 / MOE_ROWS)) * MOE_ROWS
        rows, acc = [], jnp.zeros((1, LANES), f32)
        for e in range(N_EXPERTS):
            rows.append(acc)
            acc = acc + padded[e:e + 1, :]
        off = jnp.concatenate(rows, axis=0)
        base_ref[...] = (off[:, 0:1] + in_expert).astype(jnp.int32)
        nchunk_ref[...] = nch.astype(jnp.int32)
        lofs_ref[...] = lofs.astype(jnp.int32)
        lofs_sc[...] = lofs
        lane = _iota((1, LANES), 1)
        tails_ref[...] = jnp.where(
            lane == 0, off + total,
            jnp.where(lane == 1, (padded - total) * (1.0 / MOE_CHUNK),
                      jnp.where(lane == 2, acc, jnp.where(lane == 3, nrow_tiles - acc * (1.0 / MOE_ROWS), 0.0)))
        ).astype(jnp.int32)
        start = _iota((1, ntile_pad), 1).astype(f32) * MOE_ROWS
        ends = (off + padded)[:, 0:1]
        owner = jnp.sum((start >= ends).astype(f32), axis=0, keepdims=True)
        tiles_ref[...] = jnp.zeros_like(tiles_ref)
        tiles_ref[0:1, :] = jnp.minimum(owner, N_EXPERTS - 1.0).astype(jnp.int32)
        tiles_ref[1:2, :] = (start < acc[:, 0:1]).astype(jnp.int32)

    @pl.when(phase == 1)
    def _():
        mine = jnp.sum(jnp.where(tlane == step, lofs_sc[...], 0.0), axis=1, keepdims=True)
        pos = _dot(sel, before_ref[...]) + mine
        picked = sel > 0
        lpos0_ref[...] = jnp.min(jnp.where(picked, pos, 1e9), axis=0, keepdims=True).astype(jnp.int32)
        lpos1_ref[...] = jnp.max(jnp.where(picked, pos, -1.0), axis=0, keepdims=True).astype(jnp.int32)


def _plan(sel_t, nrow_tiles):
    ne, t = sel_t.shape
    nt = t // MOE_TOK
    ntw = -(-nt // LANES) * LANES
    ntile_pad = -(-nrow_tiles // LANES) * LANES
    i32 = jnp.int32
    whole = lambda shape: pl.BlockSpec(shape, lambda p, i: (0, 0))
    per_tile = pl.BlockSpec((1, MOE_TOK), lambda p, i: (0, i * p))
    lpos0, lpos1, base, nchunk, lofs, tiles, tails = pl.pallas_call(
        functools.partial(_plan_kernel, ntw=ntw, ntile_pad=ntile_pad, nrow_tiles=nrow_tiles),
        out_shape=(jax.ShapeDtypeStruct((1, t), i32), jax.ShapeDtypeStruct((1, t), i32),
                   jax.ShapeDtypeStruct((ne, ntw), i32), jax.ShapeDtypeStruct((ne, ntw), i32),
                   jax.ShapeDtypeStruct((ne, ntw), i32), jax.ShapeDtypeStruct((8, ntile_pad), i32),
                   jax.ShapeDtypeStruct((ne, LANES), i32)),
        grid=(2, nt),
        in_specs=[pl.BlockSpec((ne, MOE_TOK), lambda p, i: (0, i))],
        out_specs=(per_tile, per_tile, whole((ne, ntw)), whole((ne, ntw)), whole((ne, ntw)),
                   whole((8, ntile_pad)), whole((ne, LANES))),
        scratch_shapes=[pltpu.VMEM((MOE_TOK, MOE_TOK), bf16), pltpu.VMEM((ne, ntw), f32),
                        pltpu.VMEM((ne, ntw), f32)],
        compiler_params=_cparams(("arbitrary", "arbitrary")),
        name="moe_plan",
    )(sel_t)
    flat = lambda a: a[:, :nt].T.reshape(nt * ne)
    return dict(lpos0=lpos0, lpos1=lpos1, base=flat(base), nchunk=flat(nchunk), lofs=flat(lofs),
                owner=tiles[0, :nrow_tiles], used=tiles[1, :nrow_tiles],
                tail_start=tails[:, 0], tail_chunks=tails[:, 1], unused=tails[0, 2:4])


def _segment_chunks(tile, base_ref, nchunk_ref, lofs_ref, make_copy, start):
    for e in range(N_EXPERTS):
        idx = tile * N_EXPERTS + e
        first_global = base_ref[idx]
        first_local = lofs_ref[idx]

        def one(c, carry, e=e, first_global=first_global, first_local=first_local):
            cp = make_copy(pl.multiple_of(first_local + c * MOE_CHUNK, MOE_CHUNK),
                           pl.multiple_of(first_global + c * MOE_CHUNK, MOE_CHUNK), e % 2)
            cp.start(priority=e % 2) if start else cp.wait()
            return carry

        lax.fori_loop(0, nchunk_ref[idx], one, 0)


def _one_hot_rows(lpos_ref):
    return (_iota((MOE_LOCAL, MOE_TOK), 0) == lpos_ref[...]).astype(bf16)


def _dispatch_kernel(base_ref, nchunk_ref, lofs_ref, tstart_ref, tchunks_ref, unused_ref, lpos0_ref, lpos1_ref,
                     h_ref, xs_hbm, loc, zeros, sem):
    tile = pl.program_id(0)
    place = _one_hot_rows(lpos0_ref) + _one_hot_rows(lpos1_ref)
    loc[...] = _dot(place, h_ref[...].astype(bf16))

    def seg_copy(local_row, global_row, s):
        return pltpu.make_async_copy(loc.at[pl.ds(local_row, MOE_CHUNK)], xs_hbm.at[pl.ds(global_row, MOE_CHUNK)],
                                     sem.at[s])

    _segment_chunks(tile, base_ref, nchunk_ref, lofs_ref, seg_copy, True)
    _segment_chunks(tile, base_ref, nchunk_ref, lofs_ref, seg_copy, False)

    @pl.when(tile == pl.num_programs(0) - 1)
    def _():
        zeros[...] = jnp.zeros_like(zeros)
        for start in (True, False):
            for e in range(N_EXPERTS):
                def one(c, carry, e=e, start=start):
                    row = pl.multiple_of(tstart_ref[e] + c * MOE_CHUNK, MOE_CHUNK)
                    cp = pltpu.make_async_copy(zeros.at[pl.ds(0, MOE_CHUNK)], xs_hbm.at[pl.ds(row, MOE_CHUNK)],
                                               sem.at[e % 2])
                    cp.start() if start else cp.wait()
                    return carry
                lax.fori_loop(0, tchunks_ref[e], one, 0)

            def whole_tile(c, carry, start=start):
                row = pl.multiple_of(unused_ref[0] + c * MOE_ROWS, MOE_ROWS)
                cp = pltpu.make_async_copy(zeros, xs_hbm.at[pl.ds(row, MOE_ROWS)], sem.at[0])
                cp.start() if start else cp.wait()
                return carry
            lax.fori_loop(0, unused_ref[1], whole_tile, 0)


def _dispatch(h_rows, plan, nrows):
    t, w = h_rows.shape
    per_tile = lambda: pl.BlockSpec((1, MOE_TOK), lambda i, *_: (0, i))
    return pl.pallas_call(
        _dispatch_kernel,
        out_shape=jax.ShapeDtypeStruct((nrows, w), h_rows.dtype),
        grid_spec=pltpu.PrefetchScalarGridSpec(
            num_scalar_prefetch=6,
            grid=(t // MOE_TOK,),
            in_specs=[per_tile(), per_tile(), pl.BlockSpec((MOE_TOK, w), lambda i, *_: (i, 0))],
            out_specs=pl.BlockSpec(memory_space=pl.ANY),
            scratch_shapes=[pltpu.VMEM((MOE_LOCAL, w), f32), pltpu.VMEM((MOE_ROWS, w), f32),
                            pltpu.SemaphoreType.DMA((2,))]),
        compiler_params=_cparams(("arbitrary",)),
        name="moe_dispatch",
    )(plan["base"], plan["nchunk"], plan["lofs"], plan["tail_start"], plan["tail_chunks"], plan["unused"],
      plan["lpos0"], plan["lpos1"], h_rows)


def _experts_kernel(owner_ref, used_ref, xs_ref, wg_ref, wu_ref, wd_ref, ys_ref):
    tile = pl.program_id(0)

    @pl.when(used_ref[tile] > 0)
    def _():
        xb = xs_ref[...].astype(bf16)
        g = _dot(xb, wg_ref[...].astype(bf16))
        he = g * lax.logistic(g) * _dot(xb, wu_ref[...].astype(bf16))
        ys_ref[...] = _dot(he.astype(bf16), wd_ref[...].astype(bf16))

    @pl.when(used_ref[tile] == 0)
    def _():
        ys_ref[...] = jnp.zeros_like(ys_ref)


def _experts(xs, owner, used, wg, wu, wd):
    nrows, w = xs.shape
    _, d, de = wg.shape
    return pl.pallas_call(
        _experts_kernel,
        out_shape=jax.ShapeDtypeStruct((nrows, w), xs.dtype),
        grid_spec=pltpu.PrefetchScalarGridSpec(
            num_scalar_prefetch=2,
            grid=(nrows // MOE_ROWS,),
            in_specs=[pl.BlockSpec((MOE_ROWS, w), lambda g, own, use: (g, 0)),
                      pl.BlockSpec((None, d, de), lambda g, own, use: (own[g], 0, 0)),
                      pl.BlockSpec((None, d, de), lambda g, own, use: (own[g], 0, 0)),
                      pl.BlockSpec((None, de, d), lambda g, own, use: (own[g], 0, 0))],
            out_specs=pl.BlockSpec((MOE_ROWS, w), lambda g, own, use: (g, 0))),
        compiler_params=_cparams(("arbitrary",)),
        name="moe_experts",
    )(owner, used, xs, wg, wu, wd)


def _combine_kernel(base_ref, nchunk_ref, lofs_ref, lpos0_ref, lpos1_ref, ys_hbm, x_ref, w_ref, mod_ref, gf_ref,
                    o_ref, loc, sem, *, final):
    tile = pl.program_id(0)
    ntile = pl.num_programs(0)
    slot = tile % 2

    def gather(tile_, slot_, start):
        def seg_copy(local_row, global_row, s):
            return pltpu.make_async_copy(ys_hbm.at[pl.ds(global_row, MOE_CHUNK)],
                                         loc.at[slot_, pl.ds(local_row, MOE_CHUNK)], sem.at[slot_, s])
        _segment_chunks(tile_, base_ref, nchunk_ref, lofs_ref, seg_copy, start)

    @pl.when(tile == 0)
    def _():
        loc[...] = jnp.zeros_like(loc)
        gather(0, 0, True)

    @pl.when(tile + 1 < ntile)
    def _():
        gather(tile + 1, 1 - slot, True)

    gather(tile, slot, False)
    ysl = loc[slot].astype(bf16)
    w = w_ref[...]
    y = (w[:, 0:1] * lax.dot_general(_one_hot_rows(lpos0_ref), ysl, _TN, preferred_element_type=f32)
         + w[:, 1:2] * lax.dot_general(_one_hot_rows(lpos1_ref), ysl, _TN, preferred_element_type=f32))
    xn = x_ref[...] + mod_ref[...][5:6, :] * y
    if final:
        xn = _rms(xn) * gf_ref[...]
    o_ref[...] = xn


def _combine(ys, plan, x, w_pair, mod, g_final, final):
    b, s, d = x.shape
    t = b * s
    per_batch = s // MOE_TOK
    per_tile = lambda: pl.BlockSpec((1, MOE_TOK), lambda i, *_: (0, i))
    out = pl.pallas_call(
        functools.partial(_combine_kernel, final=final),
        out_shape=jax.ShapeDtypeStruct((t, d), f32),
        grid_spec=pltpu.PrefetchScalarGridSpec(
            num_scalar_prefetch=3,
            grid=(t // MOE_TOK,),
            in_specs=[per_tile(), per_tile(), pl.BlockSpec(memory_space=pl.ANY),
                      pl.BlockSpec((MOE_TOK, d), lambda i, *_: (i, 0)),
                      pl.BlockSpec((MOE_TOK, LANES), lambda i, *_: (i, 0)),
                      pl.BlockSpec((None, 6, d), lambda i, *_: (i // per_batch, 0, 0)),
                      pl.BlockSpec((1, d), lambda i, *_: (0, 0))],
            out_specs=pl.BlockSpec((MOE_TOK, d), lambda i, *_: (i, 0)),
            scratch_shapes=[pltpu.VMEM((2, MOE_LOCAL, d), f32), pltpu.SemaphoreType.DMA((2, 2))]),
        compiler_params=_cparams(("arbitrary",)),
        name="moe_combine_final" if final else "moe_combine",
    )(plan["base"], plan["nchunk"], plan["lofs"], plan["lpos0"], plan["lpos1"], ys,
      x.reshape(t, d), w_pair.reshape(t, LANES), mod, g_final)
    return out.reshape(b, s, d)


def _moe(x, h_rows, sel_t, w_pair, mod, wg, wu, wd, g_final, final):
    b, s, d = x.shape
    t = b * s
    worst = 2 * t + (t // MOE_TOK) * N_EXPERTS * (MOE_CHUNK - 1) + N_EXPERTS * MOE_ROWS
    nrows = -(-worst // MOE_ROWS) * MOE_ROWS
    plan = _plan(sel_t, nrows // MOE_ROWS)
    xs = _dispatch(h_rows.reshape(t, d), plan, nrows)
    ys = _experts(xs, plan["owner"], plan["used"], wg, wu, wd)
    return _combine(ys, plan, x, w_pair, mod, g_final, final)


def _block_diag(w):
    n, a, c = w.shape
    out = jnp.zeros((n * a, n * c), w.dtype)
    for i in range(n):
        out = out.at[i * a:(i + 1) * a, i * c:(i + 1) * c].set(w[i])
    return out


def kernel(x, c, w_in, conv_w, conv_b, w_mq, w_mk, w_mv, gate_bias, g_head, w_out, w_ada, b_ada,
           w_router, router_bias, w_gate_e, w_up_e, w_down_e, g_final):
    b, s, d = x.shape
    depth = w_in.shape[0]
    qscale = HEAD_DIM ** -0.5
    tm = min(512, s)

    mod_all = _ada(c, w_ada, b_ada).reshape(depth, b, 6, d)

    wr = jnp.zeros((d, EPG, LANES), f32).at[:, :, :N_GROUPS].set(
        w_router.reshape(d, N_GROUPS, EPG).transpose(0, 2, 1)).reshape(d, EPG * LANES)
    wr_hi = wr.astype(bf16)
    wr_lo = (wr - wr_hi.astype(f32)).astype(bf16)
    rb = jnp.zeros((1, EPG, LANES), f32).at[0, :, :N_GROUPS].set(
        router_bias.reshape(N_GROUPS, EPG).T).reshape(1, EPG * LANES)

    slope = jnp.exp2(-ALIBI_MAX_BIAS * jnp.arange(1, N_DIL + 1, dtype=f32) / N_DIL)
    slopes = jnp.broadcast_to(slope.reshape(N_DIL // 2, 2, 1, 1), (N_DIL // 2, 2, 8, LANES))
    g_final2 = g_final.reshape(1, d)

    off_i = 2 * D_ML
    off_dil = off_i + 2 * N_ML
    off_sb = off_dil + 3 * D_DIL

    for l in range(depth):
        w = w_in[l]
        w_ml = jnp.concatenate([w[:, :off_i], w[:, off_i:off_dil],
                                jnp.zeros((d, LANES - 2 * N_ML), f32)], axis=1).astype(bf16)
        w_gt = w[:, off_i:off_dil].T.astype(bf16)
        wdil = w[:, off_dil:off_sb]
        w_dil = jnp.concatenate([wdil[:, :D_DIL] * qscale, wdil[:, D_DIL:]], axis=1).astype(bf16)
        wsb = w[:, off_sb:]
        w_sb = jnp.concatenate([wsb[:, :D_SB] * qscale, wsb[:, D_SB:]], axis=1).astype(bf16)
        mod = mod_all[l]

        u_ml, u_dil, u_sb, g_row = _inproj(x, mod, w_ml, w_dil, w_sb, w_gt, tm)

        gb = gate_bias[l]
        gb_col = jnp.zeros((1, LANES), f32).at[0, :2 * N_ML].set(gb.reshape(-1))
        gb_row = gb.reshape(2 * N_ML, 1)
        y_ml = _mlstm(u_ml, g_row, conv_w[l], conv_b[l].reshape(1, D_ML),
                      _block_diag(w_mq[l]).astype(bf16), (_block_diag(w_mk[l]) * qscale).astype(bf16),
                      _block_diag(w_mv[l]).astype(bf16), gb_col, gb_row)
        y_dil = _dilated(u_dil, slopes)
        y_sb = _stickbreak(u_sb)

        x, h_rows, sel_t, w_pair = _post(x, y_ml, y_dil, y_sb, mod, g_head[l].reshape(1, d),
                                           w_out[l].astype(bf16), wr_hi, wr_lo, rb, tm)
        x = _moe(x, h_rows, sel_t, w_pair, mod, w_gate_e[l], w_up_e[l], w_down_e[l], g_final2,
                 final=(l == depth - 1))
    return x
```

```python
import functools

import jax
import jax.numpy as jnp
from jax import lax
from jax.experimental import pallas as pl
from jax.experimental.pallas import tpu as pltpu

HEAD_DIM = 64
N_ML = 4
N_DIL = 6
N_SB = 6
D_ML = N_ML * HEAD_DIM
D_DIL = N_DIL * HEAD_DIM
D_SB = N_SB * HEAD_DIM
CONV_WIDTH = 4
CHUNK = 128
DILATIONS = (1, 4, 16)
DIL_SPAN = 128
ALIBI_MAX_BIAS = 8.0
N_EXPERTS = 16
N_GROUPS = 4
EPG = N_EXPERTS // N_GROUPS
EPS = 1e-6
LANES = 128
ML_COLS = 2 * D_ML + LANES
SB_DEAD_LOG = -104.0
SB_NO_KEYS = -1e30
SB_QBLOCKS = 4
SB_GROUP = 2
DIL_UNITS = 8
MOE_ROWS = 512
MOE_TOK = 256
MOE_CHUNK = 8
PLAN_GROUP = 4
SUBLANES = 8
MOE_LOCAL = 2 * MOE_TOK + N_EXPERTS * MOE_CHUNK
VMEM_LIMIT = 56 * 1024 * 1024

_NT = (((1,), (1,)), ((), ()))
_TN = (((0,), (0,)), ((), ()))

f32 = jnp.float32
bf16 = jnp.bfloat16


def _dot(a, b):
    return jnp.dot(a, b, preferred_element_type=f32)


def _dot_nt(a, b):
    return lax.dot_general(a, b, _NT, preferred_element_type=f32)


def _split_bf16(x):
    hi = x.astype(bf16)
    lo = (x - hi.astype(f32)).astype(bf16)
    return hi, lo


def _iota(shape, dim):
    return lax.broadcasted_iota(jnp.int32, shape, dim)


def _rms(x):
    return x * lax.rsqrt(jnp.mean(x * x, axis=-1, keepdims=True) + EPS)


def _cparams(sem):
    return pltpu.CompilerParams(dimension_semantics=sem, vmem_limit_bytes=VMEM_LIMIT)


def _ada_kernel(c_ref, w_ref, b_ref, o_ref):
    c = c_ref[...]
    c_act = (c * lax.logistic(c)).astype(bf16)
    o_ref[...] = _dot(c_act, w_ref[...].astype(bf16)) + b_ref[...]


def _ada(c, w_ada, b_ada):
    depth, d, nd = w_ada.shape
    b = c.shape[0]
    nblk = nd // d
    return pl.pallas_call(
        _ada_kernel,
        out_shape=jax.ShapeDtypeStruct((depth, b, nd), f32),
        grid=(depth, nblk),
        in_specs=[pl.BlockSpec((b, d), lambda l, j: (0, 0)),
                  pl.BlockSpec((None, d, d), lambda l, j: (l, 0, j)),
                  pl.BlockSpec((None, 1, d), lambda l, j: (l, 0, j))],
        out_specs=pl.BlockSpec((None, b, d), lambda l, j: (l, 0, j)),
        compiler_params=_cparams(("parallel", "parallel")),
        name="ada_mod",
    )(c, w_ada, b_ada.reshape(depth, 1, nd))


def _inproj_kernel(x_ref, mod_ref, wml_ref, wdil_ref, wsb_ref,
                   uml_ref, udil_ref, usb_ref, grow_ref):
    x = x_ref[...]
    mod = mod_ref[...]
    h = (_rms(x) * (1.0 + mod[1:2, :]) + mod[0:1, :]).astype(bf16)
    uml = _dot(h, wml_ref[...])
    uml_ref[...] = uml
    udil_ref[...] = _dot(h, wdil_ref[...])
    usb_ref[...] = _dot(h, wsb_ref[...]).astype(bf16)
    grow_ref[...] = uml[:, 2 * D_ML:].T[:2 * N_ML, :]


def _inproj(x, mod, w_ml, w_dil, w_sb, tm):
    b, s, d = x.shape
    full = lambda a: pl.BlockSpec(a.shape, lambda i, j: (0,) * a.ndim)
    return pl.pallas_call(
        _inproj_kernel,
        out_shape=(jax.ShapeDtypeStruct((b, s, ML_COLS), f32),
                   jax.ShapeDtypeStruct((b, s, 3 * D_DIL), f32),
                   jax.ShapeDtypeStruct((b, s, 3 * D_SB), bf16),
                   jax.ShapeDtypeStruct((b, 2 * N_ML, s), f32)),
        grid=(b, s // tm),
        in_specs=[pl.BlockSpec((None, tm, d), lambda i, j: (i, j, 0)),
                  pl.BlockSpec((None, 6, d), lambda i, j: (i, 0, 0)),
                  full(w_ml), full(w_dil), full(w_sb)],
        out_specs=(pl.BlockSpec((None, tm, ML_COLS), lambda i, j: (i, j, 0)),
                   pl.BlockSpec((None, tm, 3 * D_DIL), lambda i, j: (i, j, 0)),
                   pl.BlockSpec((None, tm, 3 * D_SB), lambda i, j: (i, j, 0)),
                   pl.BlockSpec((None, 2 * N_ML, tm), lambda i, j: (i, 0, j))),
        compiler_params=_cparams(("parallel", "parallel")),
        name="norm_inproj",
    )(x, mod, w_ml, w_dil, w_sb)


def _mlstm_kernel(u_ref, grow_ref, convw_ref, convb_ref, wq_ref, wk_ref, wv_ref, gbc_ref, gbr_ref,
                  y_ref, prev_ref, c_ref, n_ref, m_ref):
    L = CHUNK

    @pl.when(pl.program_id(1) == 0)
    def _():
        prev_ref[...] = jnp.zeros_like(prev_ref)
        c_ref[...] = jnp.zeros_like(c_ref)
        n_ref[...] = jnp.zeros_like(n_ref)
        m_ref[...] = jnp.zeros_like(m_ref)

    u = u_ref[...]
    xm = u[:, :D_ML]
    og = u[:, D_ML:2 * D_ML]
    gcol = u[:, 2 * D_ML:] + gbc_ref[...]
    grow = grow_ref[...] + gbr_ref[...]

    prev = prev_ref[...]
    row = _iota((L, D_ML), 0)
    convw = convw_ref[...]
    xc = xm * convw[CONV_WIDTH - 1:CONV_WIDTH, :]
    for sft in range(1, CONV_WIDTH):
        shifted = jnp.where(row < sft, pltpu.roll(prev, sft, 0), pltpu.roll(xm, sft, 0))
        xc = xc + shifted * convw[CONV_WIDTH - 1 - sft:CONV_WIDTH - sft, :]
    prev_ref[...] = xm
    xc = xc + convb_ref[...]
    xc = xc * lax.logistic(xc)
    xcb = xc.astype(bf16)
    q = _dot(xcb, wq_ref[...])
    k = _dot(xcb, wk_ref[...])
    v = _dot(xm.astype(bf16), wv_ref[...])
    kb = k.astype(bf16)
    vb = v.astype(bf16)

    def logsig(t):
        return jnp.minimum(t, 0.0) - jnp.log1p(jnp.exp(-jnp.abs(t)))

    lf_col = logsig(gcol)
    lf_row = logsig(grow)
    ltri = (_iota((L, L), 1) <= _iota((L, L), 0)).astype(bf16)
    utri = (_iota((L, L), 0) <= _iota((L, L), 1)).astype(bf16)
    ch, cl = _split_bf16(lf_col)
    cum_col = _dot(ltri, ch) + _dot(ltri, cl)
    rh, rl = _split_bf16(lf_row)
    cum_row = _dot(rh, utri) + _dot(rl, utri)

    causal = _iota((L, L), 1) <= _iota((L, L), 0)
    lane = _iota((1, D_ML), 1)
    c_bd = c_ref[...]
    n_st = n_ref[...]
    m_st = m_ref[...]
    c_bdb = c_bd.astype(bf16)

    heads = range(N_ML)
    mhs = [(lane >= hd * HEAD_DIM) & (lane < (hd + 1) * HEAD_DIM) for hd in heads]
    qhs = [jnp.where(mhs[hd], q, 0.0) for hd in heads]
    qhb = [qh.astype(bf16) for qh in qhs]
    qk = [_dot_nt(qhb[hd], kb) for hd in heads]
    qc = [_dot(qhb[hd], c_bdb) for hd in heads]
    m_ts, inters, sws = [], [], []
    wfull = jnp.zeros((L, D_ML), f32)
    decay_l = jnp.zeros((1, D_ML), f32)
    mnew_l = jnp.zeros((1, D_ML), f32)
    for hd in heads:
        cumc = cum_col[:, N_ML + hd:N_ML + hd + 1]
        cumr = cum_row[N_ML + hd:N_ML + hd + 1, :]
        igr = grow[hd:hd + 1, :]
        igc = gcol[:, hd:hd + 1]
        m_prev = m_st[:, hd * HEAD_DIM:hd * HEAD_DIM + 1]
        log_d = jnp.where(causal, cumc - cumr + igr, -jnp.inf)
        log_inter = cumc + m_prev
        m_t = jnp.maximum(log_inter, jnp.max(log_d, axis=-1, keepdims=True))
        m_ts.append(m_t)
        inters.append(jnp.exp(log_inter - m_t))
        sws.append(qk[hd] * jnp.exp(log_d - m_t))
        last = cumc[L - 1:L, :]
        log_w = last - cumc + igc
        m_new = jnp.maximum(last + m_prev, jnp.max(log_w, axis=0, keepdims=True))
        wfull = jnp.where(mhs[hd], jnp.exp(log_w - m_new), wfull)
        decay_l = jnp.where(mhs[hd], jnp.exp(last + m_prev - m_new), decay_l)
        mnew_l = jnp.where(mhs[hd], m_new, mnew_l)
    swv = [_dot(sws[hd].astype(bf16), vb) for hd in heads]
    hmix = jnp.zeros((L, D_ML), f32)
    for hd in heads:
        num = inters[hd] * qc[hd] + swv[hd]
        den = (inters[hd] * jnp.sum(qhs[hd] * n_st, axis=-1, keepdims=True)
               + jnp.sum(sws[hd], axis=-1, keepdims=True))
        hout = num / jnp.maximum(jnp.abs(den), jnp.exp(-m_ts[hd]))
        hmix = jnp.where(mhs[hd], hout, hmix)

    kw = k * wfull
    upd = lax.dot_general(kw.astype(bf16), vb, _TN, preferred_element_type=f32)
    same_head = (_iota((D_ML, D_ML), 0) // HEAD_DIM) == (_iota((D_ML, D_ML), 1) // HEAD_DIM)
    c_ref[...] = decay_l * c_bd + jnp.where(same_head, upd, 0.0)
    n_ref[...] = decay_l * n_st + jnp.sum(kw, axis=0, keepdims=True)
    m_ref[...] = mnew_l
    y_ref[...] = lax.logistic(og) * hmix


def _mlstm(u_ml, g_row, conv_w, conv_b, wq_bd, wk_bd, wv_bd, gb_col, gb_row):
    b, s, _ = u_ml.shape
    full = lambda a: pl.BlockSpec(a.shape, lambda i, j: (0,) * a.ndim)
    return pl.pallas_call(
        _mlstm_kernel,
        out_shape=jax.ShapeDtypeStruct((b, s, D_ML), f32),
        grid=(b, s // CHUNK),
        in_specs=[pl.BlockSpec((None, CHUNK, ML_COLS), lambda i, j: (i, j, 0)),
                  pl.BlockSpec((None, 2 * N_ML, CHUNK), lambda i, j: (i, 0, j)),
                  full(conv_w), full(conv_b), full(wq_bd), full(wk_bd), full(wv_bd), full(gb_col), full(gb_row)],
        out_specs=pl.BlockSpec((None, CHUNK, D_ML), lambda i, j: (i, j, 0)),
        scratch_shapes=[pltpu.VMEM((CHUNK, D_ML), f32), pltpu.VMEM((D_ML, D_ML), f32),
                        pltpu.VMEM((1, D_ML), f32), pltpu.VMEM((1, D_ML), f32)],
        compiler_params=_cparams(("parallel", "arbitrary")),
        name="mlstm",
    )(u_ml, g_row, conv_w, conv_b, wq_bd, wk_bd, wv_bd, gb_col, gb_row)


def _rows(start, size, stride):
    return pl.ds(start, size) if stride == 1 else pl.ds(start, size, stride=stride)


def _dil_kernel(q_ref, k_ref, v_ref, slope_ref, o_ref, *scratch, seq, tile):
    npat = len(DILATIONS)
    kd = scratch[0:npat]
    vd = scratch[npat:2 * npat]
    op = scratch[2 * npat:3 * npat]
    ls = scratch[3 * npat:4 * npat]
    bias_ref = scratch[4 * npat]
    L = CHUNK
    ti = pl.program_id(2)
    lane = _iota((1, LANES), 1)
    m0 = lane < HEAD_DIM

    @pl.when(ti == 0)
    def _():
        row = _iota((L, L), 0)
        col = _iota((L, L), 1)
        for p, d in enumerate(DILATIONS):
            per = seq // d
            for r in range(d):
                kd[p][r * per:(r + 1) * per, :] = k_ref[_rows(r, per, d), :].astype(bf16)
                vd[p][r * per:(r + 1) * per, :] = v_ref[_rows(r, per, d), :].astype(bf16)
            for j in range(2):
                sl = slope_ref[j][0:1, :] * float(d)
                bias_ref[p, j, 0] = jnp.where(col >= row, -(sl * (row - col + DIL_SPAN).astype(f32)), -jnp.inf)
                bias_ref[p, j, 1] = jnp.full((L, L), -jnp.inf, f32)
                bias_ref[p, j, 2] = jnp.where(col <= row, -(sl * (row - col).astype(f32)), -jnp.inf)

    nunits = tile // L
    for p, d in enumerate(DILATIONS):
        per = seq // d
        blocks_per_res = tile // d // L

        def body(it, carry, p=p, d=d, per=per, blocks_per_res=blocks_per_res):
            units = []
            for c in range(DIL_UNITS):
                u = it * DIL_UNITS + c
                r = u // blocks_per_res
                blk = u % blocks_per_res
                start = blk * (L * d) + r
                q = q_ref[_rows(start, L, d), :].astype(bf16)
                m_first = (ti * tile) // d + blk * L
                base = r * per + m_first
                has_prev = m_first > 0
                cur0 = pl.multiple_of(base, L)
                prev0 = pl.multiple_of(jnp.where(has_prev, base - L, base), L)
                units.append(dict(start=start, q=q, pk=jnp.where(has_prev, 0, 1),
                                  kc=kd[p][pl.ds(cur0, L), :], vc=vd[p][pl.ds(cur0, L), :],
                                  kp=kd[p][pl.ds(prev0, L), :], vp=vd[p][pl.ds(prev0, L), :]))
            chains = [(un, j) for un in units for j in range(2)]
            scores = []
            for un, j in chains:
                qj = jnp.where(m0 if j == 0 else jnp.logical_not(m0), un["q"], jnp.zeros_like(un["q"]))
                scores.append((_dot_nt(qj, un["kp"]) + bias_ref[p, j, un["pk"]],
                               _dot_nt(qj, un["kc"]) + bias_ref[p, j, 2]))
            probs = []
            for sp, sc in scores:
                mx = jnp.max(jnp.maximum(sp, sc), axis=-1, keepdims=True)
                pp = jnp.exp(sp - mx)
                pc = jnp.exp(sc - mx)
                den = jnp.sum(pp + pc, axis=-1, keepdims=True)
                probs.append((pp.astype(bf16), pc.astype(bf16), den, mx + jnp.log(den)))
            outs = [(_dot(pp, un["vp"]) + _dot(pc, un["vc"])) / den
                    for (un, j), (pp, pc, den, _) in zip(chains, probs)]
            for c, un in enumerate(units):
                op[p][_rows(un["start"], L, d), :] = jnp.where(m0, outs[2 * c], outs[2 * c + 1])
                ls[p][_rows(un["start"], L, d), :] = jnp.where(m0, probs[2 * c][3], probs[2 * c + 1][3])
            return carry

        lax.fori_loop(0, nunits // DIL_UNITS, body, 0)

    l0, l1, l2 = ls[0][...], ls[1][...], ls[2][...]
    mx = jnp.maximum(jnp.maximum(l0, l1), l2)
    e0, e1, e2 = jnp.exp(l0 - mx), jnp.exp(l1 - mx), jnp.exp(l2 - mx)
    o_ref[...] = ((e0 * op[0][...] + e1 * op[1][...] + e2 * op[2][...]) / (e0 + e1 + e2)).astype(o_ref.dtype)


def _dilated(u_dil, slopes):
    b, s, _ = u_dil.shape
    npair = N_DIL // 2
    tile = DILATIONS[-1] * CHUNK
    assert s % tile == 0 and (tile // CHUNK) % DIL_UNITS == 0
    seq_buf = lambda dt: pltpu.VMEM((s, LANES), dt)
    tile_buf = pltpu.VMEM((tile, LANES), f32)
    npat = len(DILATIONS)
    return pl.pallas_call(
        functools.partial(_dil_kernel, seq=s, tile=tile),
        out_shape=jax.ShapeDtypeStruct((b, s, D_DIL), bf16),
        grid=(b, npair, s // tile),
        in_specs=[pl.BlockSpec((None, tile, LANES), lambda i, h, n: (i, n, h)),
                  pl.BlockSpec((None, s, LANES), lambda i, h, n: (i, 0, npair + h)),
                  pl.BlockSpec((None, s, LANES), lambda i, h, n: (i, 0, 2 * npair + h)),
                  pl.BlockSpec((None, 2, 8, LANES), lambda i, h, n: (h, 0, 0, 0))],
        out_specs=pl.BlockSpec((None, tile, LANES), lambda i, h, n: (i, n, h)),
        scratch_shapes=([seq_buf(bf16)] * (2 * npat) + [tile_buf] * (2 * npat)
                        + [pltpu.VMEM((npat, 2, 3, CHUNK, CHUNK), f32)]),
        compiler_params=_cparams(("parallel", "parallel", "arbitrary")),
        name="dilated",
    )(u_dil, u_dil, u_dil, slopes)


def _sb_kernel(q_ref, k_ref, v_ref, o_ref, qm_ref, r_ref, acc_ref, *, nq):
    L = CHUNK
    tile0 = pl.program_id(1) * nq
    row = _iota((L, L), 0)
    col = _iota((L, L), 1)
    strict = col < row
    lane = _iota((1, LANES), 1)
    m0 = lane < HEAD_DIM
    npair = N_SB // 2
    csum = jnp.concatenate([(_iota((L, L), 0) > _iota((L, L), 1)).astype(bf16), jnp.ones((L, L), bf16)], axis=1)

    for qb in range(nq):
        for hp in range(npair):
            q = q_ref[qb * L:(qb + 1) * L, hp * LANES:(hp + 1) * LANES]
            qm_ref[(qb * npair + hp) * 2] = jnp.where(m0, q, jnp.zeros_like(q))
            qm_ref[(qb * npair + hp) * 2 + 1] = jnp.where(m0, jnp.zeros_like(q), q)

    def sweep(it, diagonal):
        rmax = None
        for g0 in range(0, nq, SB_GROUP):
            chains = []
            for qb in range(g0, g0 + SB_GROUP):
                qi = tile0 + qb
                k0 = pl.multiple_of(jnp.maximum(qi - it, 0) * L, L)
                out_of_keys = jnp.where(qi - it <= 0, SB_NO_KEYS, 0.0)
                for hp in range(npair):
                    kblk = k_ref[pl.ds(k0, L), hp * LANES:(hp + 1) * LANES]
                    vblk = v_ref[pl.ds(k0, L), hp * LANES:(hp + 1) * LANES]
                    for j in range(2):
                        chains.append(((qb * npair + hp) * 2 + j, kblk, vblk, out_of_keys))
            zs = [_dot_nt(qm_ref[c], kblk) for c, kblk, _, _ in chains]
            zl, lkb = [], []
            for z in zs:
                lk = -(jnp.maximum(z, 0.0) + jnp.log(1.0 + jnp.exp(-jnp.abs(z))))
                zl.append(z + lk)
                lkb.append((jnp.where(strict, lk, 0.0) if diagonal else lk).astype(bf16))
            ts = [_dot(x, csum) for x in lkb]
            avs = []
            for (c, _, _, out_of_keys), s, t in zip(chains, zl, ts):
                if diagonal:
                    a = jnp.where(strict, jnp.exp(s + t[:, :L]), 0.0)
                    rs = t[:, L:] + out_of_keys
                else:
                    a = jnp.exp(s + r_ref[c] + t[:, :L])
                    rs = r_ref[c] + t[:, L:] + out_of_keys
                avs.append(a.astype(bf16))
                r_ref[c] = rs
                rmax = rs if rmax is None else jnp.maximum(rmax, rs)
            for (c, _, vblk, _), a in zip(chains, avs):
                if diagonal:
                    acc_ref[c] = _dot(a, vblk)
                else:
                    acc_ref[c] += _dot(a, vblk)
        return jnp.max(rmax) > SB_DEAD_LOG

    alive0 = sweep(0, True)

    def body(carry):
        _, it = carry
        return sweep(it, False), it + 1

    lax.while_loop(lambda carry: carry[0], body, (alive0, jnp.int32(1)))

    for qb in range(nq):
        for hp in range(npair):
            c = (qb * npair + hp) * 2
            o_ref[qb * L:(qb + 1) * L, hp * LANES:(hp + 1) * LANES] = (
                jnp.where(m0, acc_ref[c], acc_ref[c + 1]).astype(o_ref.dtype))


def _stickbreak(u_sb):
    b, s, _ = u_sb.shape
    nq = SB_QBLOCKS
    tq = nq * CHUNK
    nchain = nq * N_SB
    return pl.pallas_call(
        functools.partial(_sb_kernel, nq=nq),
        out_shape=jax.ShapeDtypeStruct((b, s, D_SB), bf16),
        grid=(b, s // tq),
        in_specs=[pl.BlockSpec((None, tq, D_SB), lambda i, n: (i, n, 0)),
                  pl.BlockSpec((None, s, D_SB), lambda i, n: (i, 0, 1)),
                  pl.BlockSpec((None, s, D_SB), lambda i, n: (i, 0, 2))],
        out_specs=pl.BlockSpec((None, tq, D_SB), lambda i, n: (i, n, 0)),
        scratch_shapes=[pltpu.VMEM((nchain, CHUNK, LANES), bf16),
                        pltpu.VMEM((nchain, CHUNK, CHUNK), f32),
                        pltpu.VMEM((nchain, CHUNK, LANES), f32)],
        compiler_params=_cparams(("parallel", "arbitrary")),
        name="stickbreak",
    )(u_sb, u_sb, u_sb)


def _post_kernel(x_ref, yml_ref, ydil_ref, ysb_ref, mod_ref, gh_ref, wout_ref, wr_hi_ref, wr_lo_ref, rb_ref,
                 xo_ref, h_ref, selt_ref, w_ref):
    mod = mod_ref[...]
    y = jnp.concatenate([yml_ref[...], ydil_ref[...].astype(f32), ysb_ref[...].astype(f32)], axis=1)
    pair = ((_iota((LANES, LANES), 0) // HEAD_DIM) == (_iota((LANES, LANES), 1) // HEAD_DIM)).astype(bf16)
    cols = []
    for cblk in range(y.shape[1] // LANES):
        yb = y[:, cblk * LANES:(cblk + 1) * LANES]
        hi, lo = _split_bf16(yb * yb)
        ms = (_dot(hi, pair) + _dot(lo, pair)) * (1.0 / HEAD_DIM)
        cols.append(yb * lax.rsqrt(ms + EPS))
    yn = (jnp.concatenate(cols, axis=1) * gh_ref[...]).astype(bf16)
    xn = x_ref[...] + mod[2:3, :] * _dot(yn, wout_ref[...])
    xo_ref[...] = xn
    h2 = _rms(xn) * (1.0 + mod[4:5, :]) + mod[3:4, :]
    hh, hl = _split_bf16(h2)
    h_ref[...] = h2
    tm = x_ref.shape[0]
    row = _iota((SUBLANES, tm), 0)
    grow = row < N_GROUPS
    logits = _dot(hh, wr_hi_ref[...]) + _dot(hl, wr_hi_ref[...]) + _dot(hh, wr_lo_ref[...])
    logits_t = logits.T
    sc, bs = [], []
    for j in range(EPG):
        sc.append(lax.logistic(logits_t[j * SUBLANES:(j + 1) * SUBLANES, :]))
        bs.append(jnp.where(grow, sc[j] + rb_ref[j][:, 0:1], -jnp.inf))
    hi01, lo01 = jnp.maximum(bs[0], bs[1]), jnp.minimum(bs[0], bs[1])
    hi23, lo23 = jnp.maximum(bs[2], bs[3]), jnp.minimum(bs[2], bs[3])
    top1 = jnp.maximum(hi01, hi23)
    top2 = jnp.maximum(jnp.minimum(hi01, hi23), jnp.maximum(lo01, lo23))
    gscore = jnp.where(grow, top1 + top2, -jnp.inf)
    gmax = jnp.max(gscore, axis=0, keepdims=True)
    gidx = jnp.min(jnp.where(gscore == gmax, row, SUBLANES), axis=0, keepdims=True)
    gsel = row == gidx
    picks, vals = [], []
    for j in range(EPG):
        rank = jnp.zeros(bs[j].shape, jnp.int32)
        for jj in range(EPG):
            if jj != j:
                ahead = (bs[jj] >= bs[j]) if jj < j else (bs[jj] > bs[j])
                rank = rank + ahead.astype(jnp.int32)
        picks.append(gsel & (rank < 2))
        vals.append(jnp.sum(jnp.where(gsel, sc[j], 0.0), axis=0, keepdims=True))
    erow = _iota((N_EXPERTS, SUBLANES), 0)
    gcol = _iota((N_EXPERTS, SUBLANES), 1)
    sel_t = jnp.zeros((N_EXPERTS, tm), f32)
    for j in range(EPG):
        place = ((erow == gcol * EPG + j) & (gcol < N_GROUPS)).astype(bf16)
        sel_t = sel_t + _dot(place, picks[j].astype(bf16))
    selt_ref[...] = sel_t.astype(selt_ref.dtype)
    seen = jnp.zeros_like(vals[0])
    w_lo = jnp.zeros_like(vals[0])
    w_hi = jnp.zeros_like(vals[0])
    for j in range(EPG):
        picked = jnp.sum(picks[j].astype(f32), axis=0, keepdims=True) > 0.0
        w_lo = w_lo + jnp.where(picked & (seen == 0.0), vals[j], 0.0)
        w_hi = w_hi + jnp.where(picked & (seen == 1.0), vals[j], 0.0)
        seen = seen + picked.astype(f32)
    tot = w_lo + w_hi
    w_ref[...] = jnp.where(row == 0, w_lo / tot, jnp.where(row == 1, w_hi / tot, 0.0))


def _post(x, y_ml, y_dil, y_sb, mod, g_head, w_out, wr_hi, wr_lo, rb, tm):
    b, s, d = x.shape
    tok = lambda w: pl.BlockSpec((None, tm, w), lambda i, j: (i, j, 0))
    full = lambda a: pl.BlockSpec(a.shape, lambda i, j: (0,) * a.ndim)
    tok_lanes = lambda rows: pl.BlockSpec((rows, tm), lambda i, j: (0, i * (s // tm) + j))
    return pl.pallas_call(
        _post_kernel,
        out_shape=(jax.ShapeDtypeStruct((b, s, d), f32),
                   jax.ShapeDtypeStruct((b, s, d), f32),
                   jax.ShapeDtypeStruct((N_EXPERTS, b * s), bf16),
                   jax.ShapeDtypeStruct((SUBLANES, b * s), f32)),
        grid=(b, s // tm),
        in_specs=[tok(d), tok(D_ML), tok(D_DIL), tok(D_SB), pl.BlockSpec((None, 6, d), lambda i, j: (i, 0, 0)),
                  full(g_head), full(w_out), full(wr_hi), full(wr_lo), full(rb)],
        out_specs=(tok(d), tok(d), tok_lanes(N_EXPERTS), tok_lanes(SUBLANES)),
        compiler_params=_cparams(("parallel", "parallel")),
        name="merge_outproj_router",
    )(x, y_ml, y_dil, y_sb, mod, g_head, w_out, wr_hi, wr_lo, rb)


def _plan_kernel(selt_ref, lpos0_ref, lpos1_ref, base_ref, nchunk_ref, lofs_ref, tiles_ref, tails_ref,
                 before_ref, cnt_ref, lofs_sc, *, ntw, ntile_pad, nrow_tiles):
    phase = pl.program_id(0)
    step = pl.program_id(1)
    tk = MOE_TOK
    sel = selt_ref[...]
    tlane = _iota((1, ntw), 1)

    @pl.when((phase == 0) & (step == 0))
    def _():
        cnt_ref[...] = jnp.zeros_like(cnt_ref)
        before_ref[...] = (_iota((tk, tk), 0) < _iota((tk, tk), 1)).astype(bf16)

    @pl.when(phase == 0)
    def _():
        cnt = cnt_ref[...]
        for j in range(PLAN_GROUP):
            here = _dot(sel[:, j * tk:(j + 1) * tk], jnp.ones((tk, ntw), bf16))
            cnt = jnp.where(tlane == step * PLAN_GROUP + j, here, cnt)
        cnt_ref[...] = cnt
        lpos0_ref[...] = jnp.zeros_like(lpos0_ref)
        lpos1_ref[...] = jnp.zeros_like(lpos1_ref)

    @pl.when((phase == 1) & (step == 0))
    def _():
        cnt = cnt_ref[...]
        nch = jnp.floor((cnt + (MOE_CHUNK - 1.0)) * (1.0 / MOE_CHUNK))
        nchb = nch.astype(bf16)
        earlier_tiles = (_iota((ntw, ntw), 0) < _iota((ntw, ntw), 1)).astype(bf16)
        earlier_experts = (_iota((N_EXPERTS, N_EXPERTS), 1) < _iota((N_EXPERTS, N_EXPERTS), 0)).astype(bf16)
        in_expert = _dot(nchb, earlier_tiles) * MOE_CHUNK
        lofs = _dot(earlier_experts, nchb) * MOE_CHUNK
        total = _dot(nchb, jnp.ones((ntw, LANES), bf16)) * MOE_CHUNK
        padded = jnp.floor((total + (MOE_ROWS - 1.0)) * (1.0 / MOE_ROWS)) * MOE_ROWS
        rows, acc = [], jnp.zeros((1, LANES), f32)
        for e in range(N_EXPERTS):
            rows.append(acc)
            acc = acc + padded[e:e + 1, :]
        off = jnp.concatenate(rows, axis=0)
        base_ref[...] = (off[:, 0:1] + in_expert).astype(jnp.int32)
        nchunk_ref[...] = nch.astype(jnp.int32)
        lofs_ref[...] = lofs.astype(jnp.int32)
        lofs_sc[...] = lofs
        lane = _iota((1, LANES), 1)
        tails_ref[...] = jnp.where(
            lane == 0, off + total,
            jnp.where(lane == 1, (padded - total) * (1.0 / MOE_CHUNK),
                      jnp.where(lane == 2, acc, jnp.where(lane == 3, nrow_tiles - acc * (1.0 / MOE_ROWS), 0.0)))
        ).astype(jnp.int32)
        start = _iota((1, ntile_pad), 1).astype(f32) * MOE_ROWS
        ends = (off + padded)[:, 0:1]
        owner = jnp.sum((start >= ends).astype(f32), axis=0, keepdims=True)
        tiles_ref[...] = jnp.zeros_like(tiles_ref)
        tiles_ref[0:1, :] = jnp.minimum(owner, N_EXPERTS - 1.0).astype(jnp.int32)
        tiles_ref[1:2, :] = (start < acc[:, 0:1]).astype(jnp.int32)

    @pl.when(phase == 1)
    def _():
        for j in range(PLAN_GROUP):
            cols = slice(j * tk, (j + 1) * tk)
            mine = jnp.sum(jnp.where(tlane == step * PLAN_GROUP + j, lofs_sc[...], 0.0), axis=1, keepdims=True)
            pos = _dot(sel[:, cols], before_ref[...]) + mine
            picked = sel[:, cols] > 0
            lpos0_ref[:, cols] = jnp.min(jnp.where(picked, pos, 1e9), axis=0, keepdims=True).astype(jnp.int32)
            lpos1_ref[:, cols] = jnp.max(jnp.where(picked, pos, -1.0), axis=0, keepdims=True).astype(jnp.int32)


def _plan(sel_t, nrow_tiles):
    ne, t = sel_t.shape
    nt = t // MOE_TOK
    ntw = -(-nt // LANES) * LANES
    ntile_pad = -(-nrow_tiles // LANES) * LANES
    i32 = jnp.int32
    whole = lambda shape: pl.BlockSpec(shape, lambda p, i: (0, 0))
    per_tile = pl.BlockSpec((1, PLAN_GROUP * MOE_TOK), lambda p, i: (0, i * p))
    lpos0, lpos1, base, nchunk, lofs, tiles, tails = pl.pallas_call(
        functools.partial(_plan_kernel, ntw=ntw, ntile_pad=ntile_pad, nrow_tiles=nrow_tiles),
        out_shape=(jax.ShapeDtypeStruct((1, t), i32), jax.ShapeDtypeStruct((1, t), i32),
                   jax.ShapeDtypeStruct((ne, ntw), i32), jax.ShapeDtypeStruct((ne, ntw), i32),
                   jax.ShapeDtypeStruct((ne, ntw), i32), jax.ShapeDtypeStruct((8, ntile_pad), i32),
                   jax.ShapeDtypeStruct((ne, LANES), i32)),
        grid=(2, nt // PLAN_GROUP),
        in_specs=[pl.BlockSpec((ne, PLAN_GROUP * MOE_TOK), lambda p, i: (0, i))],
        out_specs=(per_tile, per_tile, whole((ne, ntw)), whole((ne, ntw)), whole((ne, ntw)),
                   whole((8, ntile_pad)), whole((ne, LANES))),
        scratch_shapes=[pltpu.VMEM((MOE_TOK, MOE_TOK), bf16), pltpu.VMEM((ne, ntw), f32),
                        pltpu.VMEM((ne, ntw), f32)],
        compiler_params=_cparams(("arbitrary", "arbitrary")),
        name="moe_plan",
    )(sel_t)
    flat = lambda a: a[:, :nt].T.reshape(nt * ne)
    return dict(lpos0=lpos0, lpos1=lpos1, base=flat(base), nchunk=flat(nchunk), lofs=flat(lofs),
                owner=tiles[0, :nrow_tiles], used=tiles[1, :nrow_tiles],
                tail_start=tails[:, 0], tail_chunks=tails[:, 1], unused=tails[0, 2:4])


def _segment_chunks(tile, base_ref, nchunk_ref, lofs_ref, make_copy, start):
    for e in range(N_EXPERTS):
        idx = tile * N_EXPERTS + e
        first_global = base_ref[idx]
        first_local = lofs_ref[idx]

        def one(c, carry, e=e, first_global=first_global, first_local=first_local):
            cp = make_copy(pl.multiple_of(first_local + c * MOE_CHUNK, MOE_CHUNK),
                           pl.multiple_of(first_global + c * MOE_CHUNK, MOE_CHUNK), e % 2)
            cp.start(priority=e % 2) if start else cp.wait()
            return carry

        lax.fori_loop(0, nchunk_ref[idx], one, 0)


def _one_hot_rows(lpos_ref):
    return (_iota((MOE_LOCAL, MOE_TOK), 0) == lpos_ref[...]).astype(bf16)


def _dispatch_kernel(base_ref, nchunk_ref, lofs_ref, tstart_ref, tchunks_ref, unused_ref, lpos0_ref, lpos1_ref,
                     h_ref, xs_hbm, loc, zeros, sem):
    tile = pl.program_id(0)
    place = _one_hot_rows(lpos0_ref) + _one_hot_rows(lpos1_ref)
    loc[...] = _dot(place, h_ref[...].astype(bf16))

    def seg_copy(local_row, global_row, s):
        return pltpu.make_async_copy(loc.at[pl.ds(local_row, MOE_CHUNK)], xs_hbm.at[pl.ds(global_row, MOE_CHUNK)],
                                     sem.at[s])

    _segment_chunks(tile, base_ref, nchunk_ref, lofs_ref, seg_copy, True)
    _segment_chunks(tile, base_ref, nchunk_ref, lofs_ref, seg_copy, False)

    @pl.when(tile == pl.num_programs(0) - 1)
    def _():
        zeros[...] = jnp.zeros_like(zeros)
        for start in (True, False):
            for e in range(N_EXPERTS):
                def one(c, carry, e=e, start=start):
                    row = pl.multiple_of(tstart_ref[e] + c * MOE_CHUNK, MOE_CHUNK)
                    cp = pltpu.make_async_copy(zeros.at[pl.ds(0, MOE_CHUNK)], xs_hbm.at[pl.ds(row, MOE_CHUNK)],
                                               sem.at[e % 2])
                    cp.start() if start else cp.wait()
                    return carry
                lax.fori_loop(0, tchunks_ref[e], one, 0)

            def whole_tile(c, carry, start=start):
                row = pl.multiple_of(unused_ref[0] + c * MOE_ROWS, MOE_ROWS)
                cp = pltpu.make_async_copy(zeros, xs_hbm.at[pl.ds(row, MOE_ROWS)], sem.at[0])
                cp.start() if start else cp.wait()
                return carry
            lax.fori_loop(0, unused_ref[1], whole_tile, 0)


def _dispatch(h_rows, plan, nrows):
    t, w = h_rows.shape
    per_tile = lambda: pl.BlockSpec((1, MOE_TOK), lambda i, *_: (0, i))
    return pl.pallas_call(
        _dispatch_kernel,
        out_shape=jax.ShapeDtypeStruct((nrows, w), h_rows.dtype),
        grid_spec=pltpu.PrefetchScalarGridSpec(
            num_scalar_prefetch=6,
            grid=(t // MOE_TOK,),
            in_specs=[per_tile(), per_tile(), pl.BlockSpec((MOE_TOK, w), lambda i, *_: (i, 0))],
            out_specs=pl.BlockSpec(memory_space=pl.ANY),
            scratch_shapes=[pltpu.VMEM((MOE_LOCAL, w), f32), pltpu.VMEM((MOE_ROWS, w), f32),
                            pltpu.SemaphoreType.DMA((2,))]),
        compiler_params=_cparams(("arbitrary",)),
        name="moe_dispatch",
    )(plan["base"], plan["nchunk"], plan["lofs"], plan["tail_start"], plan["tail_chunks"], plan["unused"],
      plan["lpos0"], plan["lpos1"], h_rows)


def _experts_kernel(owner_ref, used_ref, xs_ref, wg_ref, wu_ref, wd_ref, ys_ref):
    tile = pl.program_id(0)

    @pl.when(used_ref[tile] > 0)
    def _():
        xb = xs_ref[...].astype(bf16)
        g = _dot(xb, wg_ref[...].astype(bf16))
        he = g * lax.logistic(g) * _dot(xb, wu_ref[...].astype(bf16))
        ys_ref[...] = _dot(he.astype(bf16), wd_ref[...].astype(bf16))

    @pl.when(used_ref[tile] == 0)
    def _():
        ys_ref[...] = jnp.zeros_like(ys_ref)


def _experts(xs, owner, used, wg, wu, wd):
    nrows, w = xs.shape
    _, d, de = wg.shape
    return pl.pallas_call(
        _experts_kernel,
        out_shape=jax.ShapeDtypeStruct((nrows, w), xs.dtype),
        grid_spec=pltpu.PrefetchScalarGridSpec(
            num_scalar_prefetch=2,
            grid=(nrows // MOE_ROWS,),
            in_specs=[pl.BlockSpec((MOE_ROWS, w), lambda g, own, use: (g * use[g], 0)),
                      pl.BlockSpec((None, d, de), lambda g, own, use: (own[g], 0, 0)),
                      pl.BlockSpec((None, d, de), lambda g, own, use: (own[g], 0, 0)),
                      pl.BlockSpec((None, de, d), lambda g, own, use: (own[g], 0, 0))],
            out_specs=pl.BlockSpec((MOE_ROWS, w), lambda g, own, use: (g, 0))),
        compiler_params=_cparams(("arbitrary",)),
        name="moe_experts",
    )(owner, used, xs, wg, wu, wd)


def _combine_kernel(base_ref, nchunk_ref, lofs_ref, lpos0_ref, lpos1_ref, ys_hbm, x_ref, w_ref, mod_ref, gf_ref,
                    o_ref, loc, sem, *, final):
    tile = pl.program_id(0)
    ntile = pl.num_programs(0)
    slot = tile % 2

    def gather(tile_, slot_, start):
        def seg_copy(local_row, global_row, s):
            return pltpu.make_async_copy(ys_hbm.at[pl.ds(global_row, MOE_CHUNK)],
                                         loc.at[slot_, pl.ds(local_row, MOE_CHUNK)], sem.at[slot_, s])
        _segment_chunks(tile_, base_ref, nchunk_ref, lofs_ref, seg_copy, start)

    @pl.when(tile == 0)
    def _():
        loc[...] = jnp.zeros_like(loc)
        gather(0, 0, True)

    @pl.when(tile + 1 < ntile)
    def _():
        gather(tile + 1, 1 - slot, True)

    gather(tile, slot, False)
    ysl = loc[slot].astype(bf16)
    prow = _iota((MOE_LOCAL, MOE_TOK), 0)
    gates = (jnp.where(prow == lpos0_ref[...], w_ref[0:1, :], 0.0)
             + jnp.where(prow == lpos1_ref[...], w_ref[1:2, :], 0.0))
    g_hi, g_lo = _split_bf16(gates)
    y = (lax.dot_general(g_hi, ysl, _TN, preferred_element_type=f32)
         + lax.dot_general(g_lo, ysl, _TN, preferred_element_type=f32))
    xn = x_ref[...] + mod_ref[...][5:6, :] * y
    if final:
        xn = _rms(xn) * gf_ref[...]
    o_ref[...] = xn


def _combine(ys, plan, x, w_pair, mod, g_final, final):
    b, s, d = x.shape
    t = b * s
    per_batch = s // MOE_TOK
    per_tile = lambda: pl.BlockSpec((1, MOE_TOK), lambda i, *_: (0, i))
    out = pl.pallas_call(
        functools.partial(_combine_kernel, final=final),
        out_shape=jax.ShapeDtypeStruct((t, d), f32),
        grid_spec=pltpu.PrefetchScalarGridSpec(
            num_scalar_prefetch=3,
            grid=(t // MOE_TOK,),
            in_specs=[per_tile(), per_tile(), pl.BlockSpec(memory_space=pl.ANY),
                      pl.BlockSpec((MOE_TOK, d), lambda i, *_: (i, 0)),
                      pl.BlockSpec((SUBLANES, MOE_TOK), lambda i, *_: (0, i)),
                      pl.BlockSpec((None, 6, d), lambda i, *_: (i // per_batch, 0, 0)),
                      pl.BlockSpec((1, d), lambda i, *_: (0, 0))],
            out_specs=pl.BlockSpec((MOE_TOK, d), lambda i, *_: (i, 0)),
            scratch_shapes=[pltpu.VMEM((2, MOE_LOCAL, d), f32), pltpu.SemaphoreType.DMA((2, 2))]),
        compiler_params=_cparams(("arbitrary",)),
        name="moe_combine_final" if final else "moe_combine",
    )(plan["base"], plan["nchunk"], plan["lofs"], plan["lpos0"], plan["lpos1"], ys,
      x.reshape(t, d), w_pair, mod, g_final)
    return out.reshape(b, s, d)


def _moe(x, h_rows, sel_t, w_pair, mod, wg, wu, wd, g_final, final):
    b, s, d = x.shape
    t = b * s
    worst = 2 * t + (t // MOE_TOK) * N_EXPERTS * (MOE_CHUNK - 1) + N_EXPERTS * MOE_ROWS
    nrows = -(-worst // MOE_ROWS) * MOE_ROWS
    plan = _plan(sel_t, nrows // MOE_ROWS)
    xs = _dispatch(h_rows.reshape(t, d), plan, nrows)
    ys = _experts(xs, plan["owner"], plan["used"], wg, wu, wd)
    return _combine(ys, plan, x, w_pair, mod, g_final, final)


def _block_diag(w):
    n, a, c = w.shape
    out = jnp.zeros((n * a, n * c), w.dtype)
    for i in range(n):
        out = out.at[i * a:(i + 1) * a, i * c:(i + 1) * c].set(w[i])
    return out


def kernel(x, c, w_in, conv_w, conv_b, w_mq, w_mk, w_mv, gate_bias, g_head, w_out, w_ada, b_ada,
           w_router, router_bias, w_gate_e, w_up_e, w_down_e, g_final):
    b, s, d = x.shape
    depth = w_in.shape[0]
    qscale = HEAD_DIM ** -0.5
    tm = min(512, s)

    mod_all = _ada(c, w_ada, b_ada).reshape(depth, b, 6, d)

    wr = jnp.zeros((d, LANES // SUBLANES, SUBLANES), f32).at[:, :EPG, :N_GROUPS].set(
        w_router.reshape(d, N_GROUPS, EPG).transpose(0, 2, 1)).reshape(d, LANES)
    wr_hi = wr.astype(bf16)
    wr_lo = (wr - wr_hi.astype(f32)).astype(bf16)
    rb = jnp.zeros((EPG, SUBLANES, LANES), f32).at[:, :N_GROUPS, :].set(
        jnp.broadcast_to(router_bias.reshape(N_GROUPS, EPG).T[:, :, None], (EPG, N_GROUPS, LANES)))

    slope = jnp.exp2(-ALIBI_MAX_BIAS * jnp.arange(1, N_DIL + 1, dtype=f32) / N_DIL)
    slopes = jnp.broadcast_to(slope.reshape(N_DIL // 2, 2, 1, 1), (N_DIL // 2, 2, 8, LANES))
    g_final2 = g_final.reshape(1, d)

    off_i = 2 * D_ML
    off_dil = off_i + 2 * N_ML
    off_sb = off_dil + 3 * D_DIL

    for l in range(depth):
        w = w_in[l]
        w_ml = jnp.concatenate([w[:, :off_i], w[:, off_i:off_dil],
                                jnp.zeros((d, LANES - 2 * N_ML), f32)], axis=1).astype(bf16)
        wdil = w[:, off_dil:off_sb]
        w_dil = jnp.concatenate([wdil[:, :D_DIL] * qscale, wdil[:, D_DIL:]], axis=1).astype(bf16)
        wsb = w[:, off_sb:]
        w_sb = jnp.concatenate([wsb[:, :D_SB] * qscale, wsb[:, D_SB:]], axis=1).astype(bf16)
        mod = mod_all[l]

        u_ml, u_dil, u_sb, g_row = _inproj(x, mod, w_ml, w_dil, w_sb, tm)

        gb = gate_bias[l]
        gb_col = jnp.zeros((1, LANES), f32).at[0, :2 * N_ML].set(gb.reshape(-1))
        gb_row = gb.reshape(2 * N_ML, 1)
        y_ml = _mlstm(u_ml, g_row, conv_w[l], conv_b[l].reshape(1, D_ML),
                      _block_diag(w_mq[l]).astype(bf16), (_block_diag(w_mk[l]) * qscale).astype(bf16),
                      _block_diag(w_mv[l]).astype(bf16), gb_col, gb_row)
        y_dil = _dilated(u_dil, slopes)
        y_sb = _stickbreak(u_sb)

        x, h_rows, sel_t, w_pair = _post(x, y_ml, y_dil, y_sb, mod, g_head[l].reshape(1, d),
                                           w_out[l].astype(bf16), wr_hi, wr_lo, rb, tm)
        x = _moe(x, h_rows, sel_t, w_pair, mod, w_gate_e[l], w_up_e[l], w_down_e[l], g_final2,
                 final=(l == depth - 1))
    return x
```

```python
import functools

import jax
import jax.numpy as jnp
from jax import lax
from jax.experimental import pallas as pl
from jax.experimental.pallas import tpu as pltpu

HEAD_DIM = 64
N_ML = 4
N_DIL = 6
N_SB = 6
D_ML = N_ML * HEAD_DIM
D_DIL = N_DIL * HEAD_DIM
D_SB = N_SB * HEAD_DIM
CONV_WIDTH = 4
CHUNK = 128
DILATIONS = (1, 4, 16)
DIL_SPAN = 128
ALIBI_MAX_BIAS = 8.0
N_EXPERTS = 16
N_GROUPS = 4
EPG = N_EXPERTS // N_GROUPS
EPS = 1e-6
LANES = 128
ML_COLS = 2 * D_ML + LANES
SB_DEAD_LOG = -104.0
SB_NO_KEYS = -1e30
SB_QBLOCKS = 4
SB_GROUP = 2
DIL_UNITS = 8
MOE_ROWS = 512
MOE_TOK = 256
MOE_CHUNK = 8
PLAN_GROUP = 4
SUBLANES = 8
MLSTM_ROWS = 2
MLSTM_STAGES = 6
MOE_LOCAL = 2 * MOE_TOK + N_EXPERTS * MOE_CHUNK
VMEM_LIMIT = 56 * 1024 * 1024

_NT = (((1,), (1,)), ((), ()))
_TN = (((0,), (0,)), ((), ()))

f32 = jnp.float32
bf16 = jnp.bfloat16


def _dot(a, b):
    return jnp.dot(a, b, preferred_element_type=f32)


def _dot_nt(a, b):
    return lax.dot_general(a, b, _NT, preferred_element_type=f32)


def _split_bf16(x):
    hi = x.astype(bf16)
    lo = (x - hi.astype(f32)).astype(bf16)
    return hi, lo


def _iota(shape, dim):
    return lax.broadcasted_iota(jnp.int32, shape, dim)


def _rms(x):
    return x * lax.rsqrt(jnp.mean(x * x, axis=-1, keepdims=True) + EPS)


def _cparams(sem):
    return pltpu.CompilerParams(dimension_semantics=sem, vmem_limit_bytes=VMEM_LIMIT)


def _ada_kernel(c_ref, w_ref, b_ref, o_ref):
    c = c_ref[...]
    c_act = (c * lax.logistic(c)).astype(bf16)
    o_ref[...] = _dot(c_act, w_ref[...].astype(bf16)) + b_ref[...]


def _ada(c, w_ada, b_ada):
    depth, d, nd = w_ada.shape
    b = c.shape[0]
    nblk = nd // d
    return pl.pallas_call(
        _ada_kernel,
        out_shape=jax.ShapeDtypeStruct((depth, b, nd), f32),
        grid=(depth, nblk),
        in_specs=[pl.BlockSpec((b, d), lambda l, j: (0, 0)),
                  pl.BlockSpec((None, d, d), lambda l, j: (l, 0, j)),
                  pl.BlockSpec((None, 1, d), lambda l, j: (l, 0, j))],
        out_specs=pl.BlockSpec((None, b, d), lambda l, j: (l, 0, j)),
        compiler_params=_cparams(("parallel", "parallel")),
        name="ada_mod",
    )(c, w_ada, b_ada.reshape(depth, 1, nd))


def _inproj_kernel(x_ref, mod_ref, wml_ref, wdil_ref, wsb_ref,
                   uml_ref, udil_ref, usb_ref, grow_ref):
    x = x_ref[...]
    mod = mod_ref[...]
    h = (_rms(x) * (1.0 + mod[1:2, :]) + mod[0:1, :]).astype(bf16)
    uml = _dot(h, wml_ref[...])
    uml_ref[...] = uml
    udil_ref[...] = _dot(h, wdil_ref[...])
    usb_ref[...] = _dot(h, wsb_ref[...]).astype(bf16)
    grow_ref[...] = uml[:, 2 * D_ML:].T[:2 * N_ML, :]


def _inproj(x, mod, w_ml, w_dil, w_sb, tm):
    b, s, d = x.shape
    full = lambda a: pl.BlockSpec(a.shape, lambda i, j: (0,) * a.ndim)
    return pl.pallas_call(
        _inproj_kernel,
        out_shape=(jax.ShapeDtypeStruct((b, s, ML_COLS), f32),
                   jax.ShapeDtypeStruct((b, s, 3 * D_DIL), f32),
                   jax.ShapeDtypeStruct((b, s, 3 * D_SB), bf16),
                   jax.ShapeDtypeStruct((b, 2 * N_ML, s), f32)),
        grid=(b, s // tm),
        in_specs=[pl.BlockSpec((None, tm, d), lambda i, j: (i, j, 0)),
                  pl.BlockSpec((None, 6, d), lambda i, j: (i, 0, 0)),
                  full(w_ml), full(w_dil), full(w_sb)],
        out_specs=(pl.BlockSpec((None, tm, ML_COLS), lambda i, j: (i, j, 0)),
                   pl.BlockSpec((None, tm, 3 * D_DIL), lambda i, j: (i, j, 0)),
                   pl.BlockSpec((None, tm, 3 * D_SB), lambda i, j: (i, j, 0)),
                   pl.BlockSpec((None, 2 * N_ML, tm), lambda i, j: (i, 0, j))),
        compiler_params=_cparams(("parallel", "parallel")),
        name="norm_inproj",
    )(x, mod, w_ml, w_dil, w_sb)


def _mlstm_kernel(u_ref, grow_ref, convw_ref, convb_ref, wq_ref, wk_ref, wv_ref, gbc_ref, gbr_ref,
                  y_ref, prev_ref, c_ref, n_ref, m_ref):
    L = CHUNK

    @pl.when(pl.program_id(1) == 0)
    def _():
        prev_ref[...] = jnp.zeros_like(prev_ref)
        c_ref[...] = jnp.zeros_like(c_ref)
        n_ref[...] = jnp.zeros_like(n_ref)
        m_ref[...] = jnp.zeros_like(m_ref)

    row = _iota((L, D_ML), 0)
    convw = convw_ref[...]
    ltri = (_iota((L, L), 1) <= _iota((L, L), 0)).astype(bf16)
    utri = (_iota((L, L), 0) <= _iota((L, L), 1)).astype(bf16)
    causal = _iota((L, L), 1) <= _iota((L, L), 0)
    lane = _iota((1, D_ML), 1)
    heads = range(N_ML)
    mhs = [(lane >= hd * HEAD_DIM) & (lane < (hd + 1) * HEAD_DIM) for hd in heads]
    same_head = (_iota((D_ML, D_ML), 0) // HEAD_DIM) == (_iota((D_ML, D_ML), 1) // HEAD_DIM)

    def logsig(t):
        return jnp.minimum(t, 0.0) - jnp.log1p(jnp.exp(-jnp.abs(t)))

    def chain(bb):
        u = u_ref[bb]
        xm = u[:, :D_ML]
        og = u[:, D_ML:2 * D_ML]
        gcol = u[:, 2 * D_ML:] + gbc_ref[...]
        grow = grow_ref[bb] + gbr_ref[...]
        prev = prev_ref[bb]
        xc = xm * convw[CONV_WIDTH - 1:CONV_WIDTH, :]
        for sft in range(1, CONV_WIDTH):
            shifted = jnp.where(row < sft, pltpu.roll(prev, sft, 0), pltpu.roll(xm, sft, 0))
            xc = xc + shifted * convw[CONV_WIDTH - 1 - sft:CONV_WIDTH - sft, :]
        prev_ref[bb] = xm
        xc = xc + convb_ref[...]
        xc = xc * lax.logistic(xc)
        xcb = xc.astype(bf16)
        wide = lambda col: jnp.broadcast_to(col, (L, LANES))
        twice = lambda a: jnp.concatenate([a, a], axis=1)
        lf_wide = [wide(logsig(gcol[:, N_ML + hd:N_ML + hd + 1])) for hd in heads]
        ig_wide = [wide(gcol[:, hd:hd + 1]) for hd in heads]
        lf_split = [_split_bf16(x) for x in lf_wide]
        rh, rl = _split_bf16(logsig(grow))
        yield
        q = _dot(xcb, wq_ref[...])
        k = _dot(xcb, wk_ref[...])
        v = _dot(xm.astype(bf16), wv_ref[...])
        cum_wide = [_dot(ltri, hi) + _dot(ltri, lo) for hi, lo in lf_split]
        cum_row = _dot(rh, utri) + _dot(rl, utri)
        kb = k.astype(bf16)
        vb = v.astype(bf16)
        c_bd = c_ref[bb]
        n_st = n_ref[bb]
        m_st = m_ref[bb]
        c_bdb = c_bd.astype(bf16)
        qhs = [jnp.where(mhs[hd], q, 0.0) for hd in heads]
        qhb = [qh.astype(bf16) for qh in qhs]
        qn_split = [_split_bf16(qh * n_st) for qh in qhs]
        ones_ml = jnp.ones((D_ML, LANES), bf16)
        ones_l = jnp.ones((L, LANES), bf16)
        yield
        qk = [_dot_nt(qhb[hd], kb) for hd in heads]
        qc = [_dot(qhb[hd], c_bdb) for hd in heads]
        qn = [_dot(hi, ones_ml) + _dot(lo, ones_ml) for hi, lo in qn_split]
        yield
        m_ts, inters, sws = [], [], []
        wfull = jnp.zeros((L, D_ML), f32)
        decay_l = jnp.zeros((1, D_ML), f32)
        mnew_l = jnp.zeros((1, D_ML), f32)
        for hd in heads:
            cumc = cum_wide[hd]
            cumr = cum_row[N_ML + hd:N_ML + hd + 1, :]
            igr = grow[hd:hd + 1, :]
            m_prev = m_st[:, hd * HEAD_DIM:hd * HEAD_DIM + 1]
            log_d = jnp.where(causal, cumc - cumr + igr, -jnp.inf)
            log_inter = cumc + m_prev
            m_t = jnp.maximum(log_inter, jnp.max(log_d, axis=-1, keepdims=True))
            m_ts.append(m_t)
            inters.append(jnp.exp(log_inter - m_t))
            sws.append(qk[hd] * jnp.exp(log_d - m_t))
            last = cumc[L - 1:L, :]
            log_w = last - cumc + ig_wide[hd]
            m_new = jnp.maximum(last + m_prev, jnp.max(log_w, axis=0, keepdims=True))
            wfull = jnp.where(mhs[hd], twice(jnp.exp(log_w - m_new)), wfull)
            decay_l = jnp.where(mhs[hd], twice(jnp.exp(last + m_prev - m_new)), decay_l)
            mnew_l = jnp.where(mhs[hd], twice(m_new), mnew_l)
        kw = k * wfull
        sw_split = [_split_bf16(sw) for sw in sws]
        kw16 = kw.astype(bf16)
        yield
        swv = [_dot(sw_split[hd][0], vb) for hd in heads]
        swsum = [_dot(hi, ones_l) + _dot(lo, ones_l) for hi, lo in sw_split]
        upd = lax.dot_general(kw16, vb, _TN, preferred_element_type=f32)
        yield
        hmix = jnp.zeros((L, D_ML), f32)
        for hd in heads:
            num = twice(inters[hd]) * qc[hd] + swv[hd]
            den = inters[hd] * qn[hd] + swsum[hd]
            hout = num / twice(jnp.maximum(jnp.abs(den), jnp.exp(-m_ts[hd])))
            hmix = jnp.where(mhs[hd], hout, hmix)
        c_ref[bb] = decay_l * c_bd + jnp.where(same_head, upd, 0.0)
        n_ref[bb] = decay_l * n_st + jnp.sum(kw, axis=0, keepdims=True)
        m_ref[bb] = mnew_l
        y_ref[bb] = lax.logistic(og) * hmix
        yield

    chains = [chain(bb) for bb in range(u_ref.shape[0])]
    for _ in range(MLSTM_STAGES):
        for ch_ in chains:
            next(ch_)


def _mlstm(u_ml, g_row, conv_w, conv_b, wq_bd, wk_bd, wv_bd, gb_col, gb_row):
    b, s, _ = u_ml.shape
    nb = MLSTM_ROWS if b % MLSTM_ROWS == 0 else 1
    full = lambda a: pl.BlockSpec(a.shape, lambda i, j: (0,) * a.ndim)
    return pl.pallas_call(
        _mlstm_kernel,
        out_shape=jax.ShapeDtypeStruct((b, s, D_ML), f32),
        grid=(b // nb, s // CHUNK),
        in_specs=[pl.BlockSpec((nb, CHUNK, ML_COLS), lambda i, j: (i, j, 0)),
                  pl.BlockSpec((nb, 2 * N_ML, CHUNK), lambda i, j: (i, 0, j)),
                  full(conv_w), full(conv_b), full(wq_bd), full(wk_bd), full(wv_bd), full(gb_col), full(gb_row)],
        out_specs=pl.BlockSpec((nb, CHUNK, D_ML), lambda i, j: (i, j, 0)),
        scratch_shapes=[pltpu.VMEM((nb, CHUNK, D_ML), f32), pltpu.VMEM((nb, D_ML, D_ML), f32),
                        pltpu.VMEM((nb, 1, D_ML), f32), pltpu.VMEM((nb, 1, D_ML), f32)],
        compiler_params=_cparams(("parallel", "arbitrary")),
        name="mlstm",
    )(u_ml, g_row, conv_w, conv_b, wq_bd, wk_bd, wv_bd, gb_col, gb_row)


def _rows(start, size, stride):
    return pl.ds(start, size) if stride == 1 else pl.ds(start, size, stride=stride)


def _dil_kernel(q_ref, k_ref, v_ref, slope_ref, o_ref, *scratch, seq, tile):
    npat = len(DILATIONS)
    kd = scratch[0:npat]
    vd = scratch[npat:2 * npat]
    op = scratch[2 * npat:3 * npat]
    ls = scratch[3 * npat:4 * npat]
    bias_ref = scratch[4 * npat]
    L = CHUNK
    ti = pl.program_id(2)
    lane = _iota((1, LANES), 1)
    m0 = lane < HEAD_DIM

    @pl.when(ti == 0)
    def _():
        row = _iota((L, L), 0)
        col = _iota((L, L), 1)
        for p, d in enumerate(DILATIONS):
            per = seq // d
            for r in range(d):
                kd[p][r * per:(r + 1) * per, :] = k_ref[_rows(r, per, d), :].astype(bf16)
                vd[p][r * per:(r + 1) * per, :] = v_ref[_rows(r, per, d), :].astype(bf16)
            for j in range(2):
                sl = slope_ref[j][0:1, :] * float(d)
                bias_ref[p, j, 0] = jnp.where(col >= row, -(sl * (row - col + DIL_SPAN).astype(f32)), -jnp.inf)
                bias_ref[p, j, 1] = jnp.full((L, L), -jnp.inf, f32)
                bias_ref[p, j, 2] = jnp.where(col <= row, -(sl * (row - col).astype(f32)), -jnp.inf)

    nunits = tile // L
    for p, d in enumerate(DILATIONS):
        per = seq // d
        blocks_per_res = tile // d // L

        def body(it, carry, p=p, d=d, per=per, blocks_per_res=blocks_per_res):
            units = []
            for c in range(DIL_UNITS):
                u = it * DIL_UNITS + c
                r = u // blocks_per_res
                blk = u % blocks_per_res
                start = blk * (L * d) + r
                q = q_ref[_rows(start, L, d), :].astype(bf16)
                m_first = (ti * tile) // d + blk * L
                base = r * per + m_first
                has_prev = m_first > 0
                cur0 = pl.multiple_of(base, L)
                prev0 = pl.multiple_of(jnp.where(has_prev, base - L, base), L)
                units.append(dict(start=start, q=q, pk=jnp.where(has_prev, 0, 1),
                                  kc=kd[p][pl.ds(cur0, L), :], vc=vd[p][pl.ds(cur0, L), :],
                                  kp=kd[p][pl.ds(prev0, L), :], vp=vd[p][pl.ds(prev0, L), :]))
            chains = [(un, j) for un in units for j in range(2)]
            scores = []
            for un, j in chains:
                qj = jnp.where(m0 if j == 0 else jnp.logical_not(m0), un["q"], jnp.zeros_like(un["q"]))
                scores.append((_dot_nt(qj, un["kp"]) + bias_ref[p, j, un["pk"]],
                               _dot_nt(qj, un["kc"]) + bias_ref[p, j, 2]))
            probs = []
            for sp, sc in scores:
                mx = jnp.max(jnp.maximum(sp, sc), axis=-1, keepdims=True)
                pp = jnp.exp(sp - mx)
                pc = jnp.exp(sc - mx)
                den = jnp.sum(pp + pc, axis=-1, keepdims=True)
                probs.append((pp.astype(bf16), pc.astype(bf16), den, mx + jnp.log(den)))
            outs = [(_dot(pp, un["vp"]) + _dot(pc, un["vc"])) / den
                    for (un, j), (pp, pc, den, _) in zip(chains, probs)]
            for c, un in enumerate(units):
                op[p][_rows(un["start"], L, d), :] = jnp.where(m0, outs[2 * c], outs[2 * c + 1])
                ls[p][_rows(un["start"], L, d), :] = jnp.where(m0, probs[2 * c][3], probs[2 * c + 1][3])
            return carry

        lax.fori_loop(0, nunits // DIL_UNITS, body, 0)

    l0, l1, l2 = ls[0][...], ls[1][...], ls[2][...]
    mx = jnp.maximum(jnp.maximum(l0, l1), l2)
    e0, e1, e2 = jnp.exp(l0 - mx), jnp.exp(l1 - mx), jnp.exp(l2 - mx)
    o_ref[...] = ((e0 * op[0][...] + e1 * op[1][...] + e2 * op[2][...]) / (e0 + e1 + e2)).astype(o_ref.dtype)


def _dilated(u_dil, slopes):
    b, s, _ = u_dil.shape
    npair = N_DIL // 2
    tile = DILATIONS[-1] * CHUNK
    assert s % tile == 0 and (tile // CHUNK) % DIL_UNITS == 0
    seq_buf = lambda dt: pltpu.VMEM((s, LANES), dt)
    tile_buf = pltpu.VMEM((tile, LANES), f32)
    npat = len(DILATIONS)
    return pl.pallas_call(
        functools.partial(_dil_kernel, seq=s, tile=tile),
        out_shape=jax.ShapeDtypeStruct((b, s, D_DIL), bf16),
        grid=(b, npair, s // tile),
        in_specs=[pl.BlockSpec((None, tile, LANES), lambda i, h, n: (i, n, h)),
                  pl.BlockSpec((None, s, LANES), lambda i, h, n: (i, 0, npair + h)),
                  pl.BlockSpec((None, s, LANES), lambda i, h, n: (i, 0, 2 * npair + h)),
                  pl.BlockSpec((None, 2, 8, LANES), lambda i, h, n: (h, 0, 0, 0))],
        out_specs=pl.BlockSpec((None, tile, LANES), lambda i, h, n: (i, n, h)),
        scratch_shapes=([seq_buf(bf16)] * (2 * npat) + [tile_buf] * (2 * npat)
                        + [pltpu.VMEM((npat, 2, 3, CHUNK, CHUNK), f32)]),
        compiler_params=_cparams(("parallel", "parallel", "arbitrary")),
        name="dilated",
    )(u_dil, u_dil, u_dil, slopes)


def _sb_kernel(q_ref, k_ref, v_ref, o_ref, qm_ref, r_ref, acc_ref, *, nq):
    L = CHUNK
    tile0 = pl.program_id(1) * nq
    row = _iota((L, L), 0)
    col = _iota((L, L), 1)
    strict = col < row
    lane = _iota((1, LANES), 1)
    m0 = lane < HEAD_DIM
    npair = N_SB // 2
    csum = jnp.concatenate([(_iota((L, L), 0) > _iota((L, L), 1)).astype(bf16), jnp.ones((L, L), bf16)], axis=1)

    for qb in range(nq):
        for hp in range(npair):
            q = q_ref[qb * L:(qb + 1) * L, hp * LANES:(hp + 1) * LANES]
            qm_ref[(qb * npair + hp) * 2] = jnp.where(m0, q, jnp.zeros_like(q))
            qm_ref[(qb * npair + hp) * 2 + 1] = jnp.where(m0, jnp.zeros_like(q), q)

    def sweep(it, diagonal):
        rmax = None
        for g0 in range(0, nq, SB_GROUP):
            chains = []
            for qb in range(g0, g0 + SB_GROUP):
                qi = tile0 + qb
                k0 = pl.multiple_of(jnp.maximum(qi - it, 0) * L, L)
                out_of_keys = jnp.where(qi - it <= 0, SB_NO_KEYS, 0.0)
                for hp in range(npair):
                    kblk = k_ref[pl.ds(k0, L), hp * LANES:(hp + 1) * LANES]
                    vblk = v_ref[pl.ds(k0, L), hp * LANES:(hp + 1) * LANES]
                    for j in range(2):
                        chains.append(((qb * npair + hp) * 2 + j, kblk, vblk, out_of_keys))
            zs = [_dot_nt(qm_ref[c], kblk) for c, kblk, _, _ in chains]
            zl, lkb = [], []
            for z in zs:
                lk = -(jnp.maximum(z, 0.0) + jnp.log(1.0 + jnp.exp(-jnp.abs(z))))
                zl.append(z + lk)
                lkb.append((jnp.where(strict, lk, 0.0) if diagonal else lk).astype(bf16))
            ts = [_dot(x, csum) for x in lkb]
            avs = []
            for (c, _, _, out_of_keys), s, t in zip(chains, zl, ts):
                if diagonal:
                    a = jnp.where(strict, jnp.exp(s + t[:, :L]), 0.0)
                    rs = t[:, L:] + out_of_keys
                else:
                    a = jnp.exp(s + r_ref[c] + t[:, :L])
                    rs = r_ref[c] + t[:, L:] + out_of_keys
                avs.append(a.astype(bf16))
                r_ref[c] = rs
                rmax = rs if rmax is None else jnp.maximum(rmax, rs)
            for (c, _, vblk, _), a in zip(chains, avs):
                if diagonal:
                    acc_ref[c] = _dot(a, vblk)
                else:
                    acc_ref[c] += _dot(a, vblk)
        return jnp.max(rmax) > SB_DEAD_LOG

    alive0 = sweep(0, True)

    def body(carry):
        _, it = carry
        return sweep(it, False), it + 1

    lax.while_loop(lambda carry: carry[0], body, (alive0, jnp.int32(1)))

    for qb in range(nq):
        for hp in range(npair):
            c = (qb * npair + hp) * 2
            o_ref[qb * L:(qb + 1) * L, hp * LANES:(hp + 1) * LANES] = (
                jnp.where(m0, acc_ref[c], acc_ref[c + 1]).astype(o_ref.dtype))


def _stickbreak(u_sb):
    b, s, _ = u_sb.shape
    nq = SB_QBLOCKS
    tq = nq * CHUNK
    nchain = nq * N_SB
    return pl.pallas_call(
        functools.partial(_sb_kernel, nq=nq),
        out_shape=jax.ShapeDtypeStruct((b, s, D_SB), bf16),
        grid=(b, s // tq),
        in_specs=[pl.BlockSpec((None, tq, D_SB), lambda i, n: (i, n, 0)),
                  pl.BlockSpec((None, s, D_SB), lambda i, n: (i, 0, 1)),
                  pl.BlockSpec((None, s, D_SB), lambda i, n: (i, 0, 2))],
        out_specs=pl.BlockSpec((None, tq, D_SB), lambda i, n: (i, n, 0)),
        scratch_shapes=[pltpu.VMEM((nchain, CHUNK, LANES), bf16),
                        pltpu.VMEM((nchain, CHUNK, CHUNK), f32),
                        pltpu.VMEM((nchain, CHUNK, LANES), f32)],
        compiler_params=_cparams(("parallel", "arbitrary")),
        name="stickbreak",
    )(u_sb, u_sb, u_sb)


def _post_kernel(x_ref, yml_ref, ydil_ref, ysb_ref, mod_ref, gh_ref, wout_ref, wr_hi_ref, wr_lo_ref, rb_ref,
                 xo_ref, h_ref, selt_ref, w_ref):
    mod = mod_ref[...]
    y = jnp.concatenate([yml_ref[...], ydil_ref[...].astype(f32), ysb_ref[...].astype(f32)], axis=1)
    pair = ((_iota((LANES, LANES), 0) // HEAD_DIM) == (_iota((LANES, LANES), 1) // HEAD_DIM)).astype(bf16)
    cols = []
    for cblk in range(y.shape[1] // LANES):
        yb = y[:, cblk * LANES:(cblk + 1) * LANES]
        hi, lo = _split_bf16(yb * yb)
        ms = (_dot(hi, pair) + _dot(lo, pair)) * (1.0 / HEAD_DIM)
        cols.append(yb * lax.rsqrt(ms + EPS))
    yn = (jnp.concatenate(cols, axis=1) * gh_ref[...]).astype(bf16)
    xn = x_ref[...] + mod[2:3, :] * _dot(yn, wout_ref[...])
    xo_ref[...] = xn
    h2 = _rms(xn) * (1.0 + mod[4:5, :]) + mod[3:4, :]
    hh, hl = _split_bf16(h2)
    h_ref[...] = h2
    tm = x_ref.shape[0]
    row = _iota((SUBLANES, tm), 0)
    grow = row < N_GROUPS
    logits = _dot(hh, wr_hi_ref[...]) + _dot(hl, wr_hi_ref[...]) + _dot(hh, wr_lo_ref[...])
    logits_t = logits.T
    sc, bs = [], []
    for j in range(EPG):
        sc.append(lax.logistic(logits_t[j * SUBLANES:(j + 1) * SUBLANES, :]))
        bs.append(jnp.where(grow, sc[j] + rb_ref[j][:, 0:1], -jnp.inf))
    hi01, lo01 = jnp.maximum(bs[0], bs[1]), jnp.minimum(bs[0], bs[1])
    hi23, lo23 = jnp.maximum(bs[2], bs[3]), jnp.minimum(bs[2], bs[3])
    top1 = jnp.maximum(hi01, hi23)
    top2 = jnp.maximum(jnp.minimum(hi01, hi23), jnp.maximum(lo01, lo23))
    gscore = jnp.where(grow, top1 + top2, -jnp.inf)
    gmax = jnp.max(gscore, axis=0, keepdims=True)
    gidx = jnp.min(jnp.where(gscore == gmax, row, SUBLANES), axis=0, keepdims=True)
    gsel = row == gidx
    picks, vals = [], []
    for j in range(EPG):
        rank = jnp.zeros(bs[j].shape, jnp.int32)
        for jj in range(EPG):
            if jj != j:
                ahead = (bs[jj] >= bs[j]) if jj < j else (bs[jj] > bs[j])
                rank = rank + ahead.astype(jnp.int32)
        picks.append(gsel & (rank < 2))
        vals.append(jnp.sum(jnp.where(gsel, sc[j], 0.0), axis=0, keepdims=True))
    erow = _iota((N_EXPERTS, SUBLANES), 0)
    gcol = _iota((N_EXPERTS, SUBLANES), 1)
    sel_t = jnp.zeros((N_EXPERTS, tm), f32)
    for j in range(EPG):
        place = ((erow == gcol * EPG + j) & (gcol < N_GROUPS)).astype(bf16)
        sel_t = sel_t + _dot(place, picks[j].astype(bf16))
    selt_ref[...] = sel_t.astype(selt_ref.dtype)
    seen = jnp.zeros_like(vals[0])
    w_lo = jnp.zeros_like(vals[0])
    w_hi = jnp.zeros_like(vals[0])
    for j in range(EPG):
        picked = jnp.sum(picks[j].astype(f32), axis=0, keepdims=True) > 0.0
        w_lo = w_lo + jnp.where(picked & (seen == 0.0), vals[j], 0.0)
        w_hi = w_hi + jnp.where(picked & (seen == 1.0), vals[j], 0.0)
        seen = seen + picked.astype(f32)
    tot = w_lo + w_hi
    w_ref[...] = jnp.where(row == 0, w_lo / tot, jnp.where(row == 1, w_hi / tot, 0.0))


def _post(x, y_ml, y_dil, y_sb, mod, g_head, w_out, wr_hi, wr_lo, rb, tm):
    b, s, d = x.shape
    tok = lambda w: pl.BlockSpec((None, tm, w), lambda i, j: (i, j, 0))
    full = lambda a: pl.BlockSpec(a.shape, lambda i, j: (0,) * a.ndim)
    tok_lanes = lambda rows: pl.BlockSpec((rows, tm), lambda i, j: (0, i * (s // tm) + j))
    return pl.pallas_call(
        _post_kernel,
        out_shape=(jax.ShapeDtypeStruct((b, s, d), f32),
                   jax.ShapeDtypeStruct((b, s, d), f32),
                   jax.ShapeDtypeStruct((N_EXPERTS, b * s), bf16),
                   jax.ShapeDtypeStruct((SUBLANES, b * s), f32)),
        grid=(b, s // tm),
        in_specs=[tok(d), tok(D_ML), tok(D_DIL), tok(D_SB), pl.BlockSpec((None, 6, d), lambda i, j: (i, 0, 0)),
                  full(g_head), full(w_out), full(wr_hi), full(wr_lo), full(rb)],
        out_specs=(tok(d), tok(d), tok_lanes(N_EXPERTS), tok_lanes(SUBLANES)),
        compiler_params=_cparams(("parallel", "parallel")),
        name="merge_outproj_router",
    )(x, y_ml, y_dil, y_sb, mod, g_head, w_out, wr_hi, wr_lo, rb)


def _plan_kernel(selt_ref, lpos0_ref, lpos1_ref, base_ref, nchunk_ref, lofs_ref, tiles_ref, tails_ref,
                 before_ref, cnt_ref, lofs_sc, *, ntw, ntile_pad, nrow_tiles):
    phase = pl.program_id(0)
    step = pl.program_id(1)
    tk = MOE_TOK
    sel = selt_ref[...]
    tlane = _iota((1, ntw), 1)

    @pl.when((phase == 0) & (step == 0))
    def _():
        cnt_ref[...] = jnp.zeros_like(cnt_ref)
        before_ref[...] = (_iota((tk, tk), 0) < _iota((tk, tk), 1)).astype(bf16)

    @pl.when(phase == 0)
    def _():
        cnt = cnt_ref[...]
        for j in range(PLAN_GROUP):
            here = _dot(sel[:, j * tk:(j + 1) * tk], jnp.ones((tk, ntw), bf16))
            cnt = jnp.where(tlane == step * PLAN_GROUP + j, here, cnt)
        cnt_ref[...] = cnt
        lpos0_ref[...] = jnp.zeros_like(lpos0_ref)
        lpos1_ref[...] = jnp.zeros_like(lpos1_ref)

    @pl.when((phase == 1) & (step == 0))
    def _():
        cnt = cnt_ref[...]
        nch = jnp.floor((cnt + (MOE_CHUNK - 1.0)) * (1.0 / MOE_CHUNK))
        nchb = nch.astype(bf16)
        earlier_tiles = (_iota((ntw, ntw), 0) < _iota((ntw, ntw), 1)).astype(bf16)
        earlier_experts = (_iota((N_EXPERTS, N_EXPERTS), 1) < _iota((N_EXPERTS, N_EXPERTS), 0)).astype(bf16)
        in_expert = _dot(nchb, earlier_tiles) * MOE_CHUNK
        lofs = _dot(earlier_experts, nchb) * MOE_CHUNK
        total = _dot(nchb, jnp.ones((ntw, LANES), bf16)) * MOE_CHUNK
        padded = jnp.floor((total + (MOE_ROWS - 1.0)) * (1.0 / MOE_ROWS)) * MOE_ROWS
        rows, acc = [], jnp.zeros((1, LANES), f32)
        for e in range(N_EXPERTS):
            rows.append(acc)
            acc = acc + padded[e:e + 1, :]
        off = jnp.concatenate(rows, axis=0)
        base_ref[...] = (off[:, 0:1] + in_expert).astype(jnp.int32)
        nchunk_ref[...] = nch.astype(jnp.int32)
        lofs_ref[...] = lofs.astype(jnp.int32)
        lofs_sc[...] = lofs
        lane = _iota((1, LANES), 1)
        tails_ref[...] = jnp.where(
            lane == 0, off + total,
            jnp.where(lane == 1, (padded - total) * (1.0 / MOE_CHUNK),
                      jnp.where(lane == 2, acc, jnp.where(lane == 3, nrow_tiles - acc * (1.0 / MOE_ROWS), 0.0)))
        ).astype(jnp.int32)
        start = _iota((1, ntile_pad), 1).astype(f32) * MOE_ROWS
        ends = (off + padded)[:, 0:1]
        owner = jnp.sum((start >= ends).astype(f32), axis=0, keepdims=True)
        tiles_ref[...] = jnp.zeros_like(tiles_ref)
        tiles_ref[0:1, :] = jnp.minimum(owner, N_EXPERTS - 1.0).astype(jnp.int32)
        tiles_ref[1:2, :] = (start < acc[:, 0:1]).astype(jnp.int32)

    @pl.when(phase == 1)
    def _():
        for j in range(PLAN_GROUP):
            cols = slice(j * tk, (j + 1) * tk)
            mine = jnp.sum(jnp.where(tlane == step * PLAN_GROUP + j, lofs_sc[...], 0.0), axis=1, keepdims=True)
            pos = _dot(sel[:, cols], before_ref[...]) + mine
            picked = sel[:, cols] > 0
            lpos0_ref[:, cols] = jnp.min(jnp.where(picked, pos, 1e9), axis=0, keepdims=True).astype(jnp.int32)
            lpos1_ref[:, cols] = jnp.max(jnp.where(picked, pos, -1.0), axis=0, keepdims=True).astype(jnp.int32)


def _plan(sel_t, nrow_tiles):
    ne, t = sel_t.shape
    nt = t // MOE_TOK
    ntw = -(-nt // LANES) * LANES
    ntile_pad = -(-nrow_tiles // LANES) * LANES
    i32 = jnp.int32
    whole = lambda shape: pl.BlockSpec(shape, lambda p, i: (0, 0))
    per_tile = pl.BlockSpec((1, PLAN_GROUP * MOE_TOK), lambda p, i: (0, i * p))
    lpos0, lpos1, base, nchunk, lofs, tiles, tails = pl.pallas_call(
        functools.partial(_plan_kernel, ntw=ntw, ntile_pad=ntile_pad, nrow_tiles=nrow_tiles),
        out_shape=(jax.ShapeDtypeStruct((1, t), i32), jax.ShapeDtypeStruct((1, t), i32),
                   jax.ShapeDtypeStruct((ne, ntw), i32), jax.ShapeDtypeStruct((ne, ntw), i32),
                   jax.ShapeDtypeStruct((ne, ntw), i32), jax.ShapeDtypeStruct((8, ntile_pad), i32),
                   jax.ShapeDtypeStruct((ne, LANES), i32)),
        grid=(2, nt // PLAN_GROUP),
        in_specs=[pl.BlockSpec((ne, PLAN_GROUP * MOE_TOK), lambda p, i: (0, i))],
        out_specs=(per_tile, per_tile, whole((ne, ntw)), whole((ne, ntw)), whole((ne, ntw)),
                   whole((8, ntile_pad)), whole((ne, LANES))),
        scratch_shapes=[pltpu.VMEM((MOE_TOK, MOE_TOK), bf16), pltpu.VMEM((ne, ntw), f32),
                        pltpu.VMEM((ne, ntw), f32)],
        compiler_params=_cparams(("arbitrary", "arbitrary")),
        name="moe_plan",
    )(sel_t)
    flat = lambda a: a[:, :nt].T.reshape(nt * ne)
    return dict(lpos0=lpos0, lpos1=lpos1, base=flat(base), nchunk=flat(nchunk), lofs=flat(lofs),
                owner=tiles[0, :nrow_tiles], used=tiles[1, :nrow_tiles],
                tail_start=tails[:, 0], tail_chunks=tails[:, 1], unused=tails[0, 2:4])


def _segment_chunks(tile, base_ref, nchunk_ref, lofs_ref, make_copy, start):
    for e in range(N_EXPERTS):
        idx = tile * N_EXPERTS + e
        first_global = base_ref[idx]
        first_local = lofs_ref[idx]

        def one(c, carry, e=e, first_global=first_global, first_local=first_local):
            cp = make_copy(pl.multiple_of(first_local + c * MOE_CHUNK, MOE_CHUNK),
                           pl.multiple_of(first_global + c * MOE_CHUNK, MOE_CHUNK), e % 2)
            cp.start(priority=e % 2) if start else cp.wait()
            return carry

        lax.fori_loop(0, nchunk_ref[idx], one, 0)


def _one_hot_rows(lpos_ref):
    return (_iota((MOE_LOCAL, MOE_TOK), 0) == lpos_ref[...]).astype(bf16)


def _dispatch_kernel(base_ref, nchunk_ref, lofs_ref, tstart_ref, tchunks_ref, unused_ref, lpos0_ref, lpos1_ref,
                     h_ref, xs_hbm, loc, zeros, sem):
    tile = pl.program_id(0)
    place = _one_hot_rows(lpos0_ref) + _one_hot_rows(lpos1_ref)
    loc[...] = _dot(place, h_ref[...].astype(bf16))

    def seg_copy(local_row, global_row, s):
        return pltpu.make_async_copy(loc.at[pl.ds(local_row, MOE_CHUNK)], xs_hbm.at[pl.ds(global_row, MOE_CHUNK)],
                                     sem.at[s])

    _segment_chunks(tile, base_ref, nchunk_ref, lofs_ref, seg_copy, True)
    _segment_chunks(tile, base_ref, nchunk_ref, lofs_ref, seg_copy, False)

    @pl.when(tile == pl.num_programs(0) - 1)
    def _():
        zeros[...] = jnp.zeros_like(zeros)
        for start in (True, False):
            for e in range(N_EXPERTS):
                def one(c, carry, e=e, start=start):
                    row = pl.multiple_of(tstart_ref[e] + c * MOE_CHUNK, MOE_CHUNK)
                    cp = pltpu.make_async_copy(zeros.at[pl.ds(0, MOE_CHUNK)], xs_hbm.at[pl.ds(row, MOE_CHUNK)],
                                               sem.at[e % 2])
                    cp.start() if start else cp.wait()
                    return carry
                lax.fori_loop(0, tchunks_ref[e], one, 0)

            def whole_tile(c, carry, start=start):
                row = pl.multiple_of(unused_ref[0] + c * MOE_ROWS, MOE_ROWS)
                cp = pltpu.make_async_copy(zeros, xs_hbm.at[pl.ds(row, MOE_ROWS)], sem.at[0])
                cp.start() if start else cp.wait()
                return carry
            lax.fori_loop(0, unused_ref[1], whole_tile, 0)


def _dispatch(h_rows, plan, nrows):
    t, w = h_rows.shape
    per_tile = lambda: pl.BlockSpec((1, MOE_TOK), lambda i, *_: (0, i))
    return pl.pallas_call(
        _dispatch_kernel,
        out_shape=jax.ShapeDtypeStruct((nrows, w), h_rows.dtype),
        grid_spec=pltpu.PrefetchScalarGridSpec(
            num_scalar_prefetch=6,
            grid=(t // MOE_TOK,),
            in_specs=[per_tile(), per_tile(), pl.BlockSpec((MOE_TOK, w), lambda i, *_: (i, 0))],
            out_specs=pl.BlockSpec(memory_space=pl.ANY),
            scratch_shapes=[pltpu.VMEM((MOE_LOCAL, w), f32), pltpu.VMEM((MOE_ROWS, w), f32),
                            pltpu.SemaphoreType.DMA((2,))]),
        compiler_params=_cparams(("arbitrary",)),
        name="moe_dispatch",
    )(plan["base"], plan["nchunk"], plan["lofs"], plan["tail_start"], plan["tail_chunks"], plan["unused"],
      plan["lpos0"], plan["lpos1"], h_rows)


def _experts_kernel(owner_ref, used_ref, xs_ref, wg_ref, wu_ref, wd_ref, ys_ref):
    tile = pl.program_id(0)

    @pl.when(used_ref[tile] > 0)
    def _():
        xb = xs_ref[...].astype(bf16)
        g = _dot(xb, wg_ref[...].astype(bf16))
        he = g * lax.logistic(g) * _dot(xb, wu_ref[...].astype(bf16))
        ys_ref[...] = _dot(he.astype(bf16), wd_ref[...].astype(bf16))

    @pl.when(used_ref[tile] == 0)
    def _():
        ys_ref[...] = jnp.zeros_like(ys_ref)


def _experts(xs, owner, used, wg, wu, wd):
    nrows, w = xs.shape
    _, d, de = wg.shape
    return pl.pallas_call(
        _experts_kernel,
        out_shape=jax.ShapeDtypeStruct((nrows, w), xs.dtype),
        grid_spec=pltpu.PrefetchScalarGridSpec(
            num_scalar_prefetch=2,
            grid=(nrows // MOE_ROWS,),
            in_specs=[pl.BlockSpec((MOE_ROWS, w), lambda g, own, use: (g * use[g], 0)),
                      pl.BlockSpec((None, d, de), lambda g, own, use: (own[g], 0, 0)),
                      pl.BlockSpec((None, d, de), lambda g, own, use: (own[g], 0, 0)),
                      pl.BlockSpec((None, de, d), lambda g, own, use: (own[g], 0, 0))],
            out_specs=pl.BlockSpec((MOE_ROWS, w), lambda g, own, use: (g, 0))),
        compiler_params=_cparams(("arbitrary",)),
        name="moe_experts",
    )(owner, used, xs, wg, wu, wd)


def _combine_kernel(base_ref, nchunk_ref, lofs_ref, lpos0_ref, lpos1_ref, ys_hbm, x_ref, w_ref, mod_ref, gf_ref,
                    o_ref, loc, sem, *, final):
    tile = pl.program_id(0)
    ntile = pl.num_programs(0)
    slot = tile % 2

    def gather(tile_, slot_, start):
        def seg_copy(local_row, global_row, s):
            return pltpu.make_async_copy(ys_hbm.at[pl.ds(global_row, MOE_CHUNK)],
                                         loc.at[slot_, pl.ds(local_row, MOE_CHUNK)], sem.at[slot_, s])
        _segment_chunks(tile_, base_ref, nchunk_ref, lofs_ref, seg_copy, start)

    @pl.when(tile == 0)
    def _():
        loc[...] = jnp.zeros_like(loc)
        gather(0, 0, True)

    @pl.when(tile + 1 < ntile)
    def _():
        gather(tile + 1, 1 - slot, True)

    gather(tile, slot, False)
    ysl = loc[slot].astype(bf16)
    prow = _iota((MOE_LOCAL, MOE_TOK), 0)
    gates = (jnp.where(prow == lpos0_ref[...], w_ref[0:1, :], 0.0)
             + jnp.where(prow == lpos1_ref[...], w_ref[1:2, :], 0.0))
    g_hi, g_lo = _split_bf16(gates)
    y = (lax.dot_general(g_hi, ysl, _TN, preferred_element_type=f32)
         + lax.dot_general(g_lo, ysl, _TN, preferred_element_type=f32))
    xn = x_ref[...] + mod_ref[...][5:6, :] * y
    if final:
        xn = _rms(xn) * gf_ref[...]
    o_ref[...] = xn


def _combine(ys, plan, x, w_pair, mod, g_final, final):
    b, s, d = x.shape
    t = b * s
    per_batch = s // MOE_TOK
    per_tile = lambda: pl.BlockSpec((1, MOE_TOK), lambda i, *_: (0, i))
    out = pl.pallas_call(
        functools.partial(_combine_kernel, final=final),
        out_shape=jax.ShapeDtypeStruct((t, d), f32),
        grid_spec=pltpu.PrefetchScalarGridSpec(
            num_scalar_prefetch=3,
            grid=(t // MOE_TOK,),
            in_specs=[per_tile(), per_tile(), pl.BlockSpec(memory_space=pl.ANY),
                      pl.BlockSpec((MOE_TOK, d), lambda i, *_: (i, 0)),
                      pl.BlockSpec((SUBLANES, MOE_TOK), lambda i, *_: (0, i)),
                      pl.BlockSpec((None, 6, d), lambda i, *_: (i // per_batch, 0, 0)),
                      pl.BlockSpec((1, d), lambda i, *_: (0, 0))],
            out_specs=pl.BlockSpec((MOE_TOK, d), lambda i, *_: (i, 0)),
            scratch_shapes=[pltpu.VMEM((2, MOE_LOCAL, d), f32), pltpu.SemaphoreType.DMA((2, 2))]),
        compiler_params=_cparams(("arbitrary",)),
        name="moe_combine_final" if final else "moe_combine",
    )(plan["base"], plan["nchunk"], plan["lofs"], plan["lpos0"], plan["lpos1"], ys,
      x.reshape(t, d), w_pair, mod, g_final)
    return out.reshape(b, s, d)


def _moe(x, h_rows, sel_t, w_pair, mod, wg, wu, wd, g_final, final):
    b, s, d = x.shape
    t = b * s
    worst = 2 * t + (t // MOE_TOK) * N_EXPERTS * (MOE_CHUNK - 1) + N_EXPERTS * MOE_ROWS
    nrows = -(-worst // MOE_ROWS) * MOE_ROWS
    plan = _plan(sel_t, nrows // MOE_ROWS)
    xs = _dispatch(h_rows.reshape(t, d), plan, nrows)
    ys = _experts(xs, plan["owner"], plan["used"], wg, wu, wd)
    return _combine(ys, plan, x, w_pair, mod, g_final, final)


def _block_diag(w):
    n, a, c = w.shape
    out = jnp.zeros((n * a, n * c), w.dtype)
    for i in range(n):
        out = out.at[i * a:(i + 1) * a, i * c:(i + 1) * c].set(w[i])
    return out


def kernel(x, c, w_in, conv_w, conv_b, w_mq, w_mk, w_mv, gate_bias, g_head, w_out, w_ada, b_ada,
           w_router, router_bias, w_gate_e, w_up_e, w_down_e, g_final):
    b, s, d = x.shape
    depth = w_in.shape[0]
    qscale = HEAD_DIM ** -0.5
    tm = min(512, s)

    mod_all = _ada(c, w_ada, b_ada).reshape(depth, b, 6, d)

    wr = jnp.zeros((d, LANES // SUBLANES, SUBLANES), f32).at[:, :EPG, :N_GROUPS].set(
        w_router.reshape(d, N_GROUPS, EPG).transpose(0, 2, 1)).reshape(d, LANES)
    wr_hi = wr.astype(bf16)
    wr_lo = (wr - wr_hi.astype(f32)).astype(bf16)
    rb = jnp.zeros((EPG, SUBLANES, LANES), f32).at[:, :N_GROUPS, :].set(
        jnp.broadcast_to(router_bias.reshape(N_GROUPS, EPG).T[:, :, None], (EPG, N_GROUPS, LANES)))

    slope = jnp.exp2(-ALIBI_MAX_BIAS * jnp.arange(1, N_DIL + 1, dtype=f32) / N_DIL)
    slopes = jnp.broadcast_to(slope.reshape(N_DIL // 2, 2, 1, 1), (N_DIL // 2, 2, 8, LANES))
    g_final2 = g_final.reshape(1, d)

    off_i = 2 * D_ML
    off_dil = off_i + 2 * N_ML
    off_sb = off_dil + 3 * D_DIL

    for l in range(depth):
        w = w_in[l]
        w_ml = jnp.concatenate([w[:, :off_i], w[:, off_i:off_dil],
                                jnp.zeros((d, LANES - 2 * N_ML), f32)], axis=1).astype(bf16)
        wdil = w[:, off_dil:off_sb]
        w_dil = jnp.concatenate([wdil[:, :D_DIL] * qscale, wdil[:, D_DIL:]], axis=1).astype(bf16)
        wsb = w[:, off_sb:]
        w_sb = jnp.concatenate([wsb[:, :D_SB] * qscale, wsb[:, D_SB:]], axis=1).astype(bf16)
        mod = mod_all[l]

        u_ml, u_dil, u_sb, g_row = _inproj(x, mod, w_ml, w_dil, w_sb, tm)

        gb = gate_bias[l]
        gb_col = jnp.zeros((1, LANES), f32).at[0, :2 * N_ML].set(gb.reshape(-1))
        gb_row = gb.reshape(2 * N_ML, 1)
        y_ml = _mlstm(u_ml, g_row, conv_w[l], conv_b[l].reshape(1, D_ML),
                      _block_diag(w_mq[l]).astype(bf16), (_block_diag(w_mk[l]) * qscale).astype(bf16),
                      _block_diag(w_mv[l]).astype(bf16), gb_col, gb_row)
        y_dil = _dilated(u_dil, slopes)
        y_sb = _stickbreak(u_sb)

        x, h_rows, sel_t, w_pair = _post(x, y_ml, y_dil, y_sb, mod, g_head[l].reshape(1, d),
                                           w_out[l].astype(bf16), wr_hi, wr_lo, rb, tm)
        x = _moe(x, h_rows, sel_t, w_pair, mod, w_gate_e[l], w_up_e[l], w_down_e[l], g_final2,
                 final=(l == depth - 1))
    return x
```

```python
import functools

import jax
import jax.numpy as jnp
from jax import lax
from jax.experimental import pallas as pl
from jax.experimental.pallas import tpu as pltpu

HEAD_DIM = 64
N_ML = 4
N_DIL = 6
N_SB = 6
D_ML = N_ML * HEAD_DIM
D_DIL = N_DIL * HEAD_DIM
D_SB = N_SB * HEAD_DIM
CONV_WIDTH = 4
CHUNK = 128
DILATIONS = (1, 4, 16)
DIL_SPAN = 128
ALIBI_MAX_BIAS = 8.0
N_EXPERTS = 16
N_GROUPS = 4
EPG = N_EXPERTS // N_GROUPS
EPS = 1e-6
LANES = 128
ML_COLS = 2 * D_ML + LANES
SB_DEAD_LOG = -104.0
SB_NO_KEYS = -1e30
SB_QBLOCKS = 4
SB_GROUP = 2
DIL_UNITS = 8
MOE_ROWS = 512
MOE_TOK = 256
MOE_CHUNK = 8
PLAN_GROUP = 4
SUBLANES = 8
MLSTM_ROWS = 2
MLSTM_STAGES = 6
MOE_LOCAL = 2 * MOE_TOK + N_EXPERTS * MOE_CHUNK
VMEM_LIMIT = 56 * 1024 * 1024

_NT = (((1,), (1,)), ((), ()))
_TN = (((0,), (0,)), ((), ()))

f32 = jnp.float32
bf16 = jnp.bfloat16


def _dot(a, b):
    return jnp.dot(a, b, preferred_element_type=f32)


def _dot_nt(a, b):
    return lax.dot_general(a, b, _NT, preferred_element_type=f32)


def _split_bf16(x):
    hi = x.astype(bf16)
    lo = (x - hi.astype(f32)).astype(bf16)
    return hi, lo


def _iota(shape, dim):
    return lax.broadcasted_iota(jnp.int32, shape, dim)


def _rms(x):
    return x * lax.rsqrt(jnp.mean(x * x, axis=-1, keepdims=True) + EPS)


def _cparams(sem):
    return pltpu.CompilerParams(dimension_semantics=sem, vmem_limit_bytes=VMEM_LIMIT)


def _ada_kernel(c_ref, w_ref, b_ref, o_ref):
    c = c_ref[...]
    c_act = (c * lax.logistic(c)).astype(bf16)
    o_ref[...] = _dot(c_act, w_ref[...].astype(bf16)) + b_ref[...]


def _ada(c, w_ada, b_ada):
    depth, d, nd = w_ada.shape
    b = c.shape[0]
    nblk = nd // d
    return pl.pallas_call(
        _ada_kernel,
        out_shape=jax.ShapeDtypeStruct((depth, b, nd), f32),
        grid=(depth, nblk),
        in_specs=[pl.BlockSpec((b, d), lambda l, j: (0, 0)),
                  pl.BlockSpec((None, d, d), lambda l, j: (l, 0, j)),
                  pl.BlockSpec((None, 1, d), lambda l, j: (l, 0, j))],
        out_specs=pl.BlockSpec((None, b, d), lambda l, j: (l, 0, j)),
        compiler_params=_cparams(("parallel", "parallel")),
        name="ada_mod",
    )(c, w_ada, b_ada.reshape(depth, 1, nd))


def _wprep_kernel(w_ref, wml_ref, wdil_ref, wsb_ref):
    w = w_ref[...]
    off_dil = 2 * D_ML + 2 * N_ML
    off_sb = off_dil + 3 * D_DIL
    qs = HEAD_DIM ** -0.5
    ml = w[:, :ML_COLS]
    wml_ref[...] = jnp.where(_iota((1, ML_COLS), 1) < off_dil, ml, 0.0).astype(bf16)
    q_cols = jnp.where(_iota((1, 3 * D_DIL), 1) < D_DIL, qs, 1.0)
    wdil_ref[...] = (w[:, off_dil:off_sb] * q_cols).astype(bf16)
    wsb_ref[...] = (w[:, off_sb:off_sb + 3 * D_SB] * q_cols).astype(bf16)


def _wprep(w_in, rows):
    depth, d, d_in = w_in.shape
    assert d_in == 2 * D_ML + 2 * N_ML + 3 * D_DIL + 3 * D_SB and D_DIL == D_SB
    out = lambda w: pl.BlockSpec((None, rows, w), lambda l, i: (l, i, 0))
    return pl.pallas_call(
        _wprep_kernel,
        out_shape=(jax.ShapeDtypeStruct((depth, d, ML_COLS), bf16),
                   jax.ShapeDtypeStruct((depth, d, 3 * D_DIL), bf16),
                   jax.ShapeDtypeStruct((depth, d, 3 * D_SB), bf16)),
        grid=(depth, d // rows),
        in_specs=[pl.BlockSpec((None, rows, d_in), lambda l, i: (l, i, 0))],
        out_specs=(out(ML_COLS), out(3 * D_DIL), out(3 * D_SB)),
        compiler_params=_cparams(("parallel", "parallel")),
        name="inproj_weight_layout",
    )(w_in)


def _inproj_kernel(x_ref, mod_ref, wml_ref, wdil_ref, wsb_ref,
                   uml_ref, udil_ref, usb_ref, grow_ref):
    x = x_ref[...]
    mod = mod_ref[...]
    h = (_rms(x) * (1.0 + mod[1:2, :]) + mod[0:1, :]).astype(bf16)
    uml = _dot(h, wml_ref[...])
    uml_ref[...] = uml
    udil_ref[...] = _dot(h, wdil_ref[...])
    usb_ref[...] = _dot(h, wsb_ref[...]).astype(bf16)
    grow_ref[...] = uml[:, 2 * D_ML:].T[:2 * N_ML, :]


def _inproj(x, mod, w_ml, w_dil, w_sb, tm):
    b, s, d = x.shape
    full = lambda a: pl.BlockSpec(a.shape, lambda i, j: (0,) * a.ndim)
    return pl.pallas_call(
        _inproj_kernel,
        out_shape=(jax.ShapeDtypeStruct((b, s, ML_COLS), f32),
                   jax.ShapeDtypeStruct((b, s, 3 * D_DIL), f32),
                   jax.ShapeDtypeStruct((b, s, 3 * D_SB), bf16),
                   jax.ShapeDtypeStruct((b, 2 * N_ML, s), f32)),
        grid=(b, s // tm),
        in_specs=[pl.BlockSpec((None, tm, d), lambda i, j: (i, j, 0)),
                  pl.BlockSpec((None, 6, d), lambda i, j: (i, 0, 0)),
                  full(w_ml), full(w_dil), full(w_sb)],
        out_specs=(pl.BlockSpec((None, tm, ML_COLS), lambda i, j: (i, j, 0)),
                   pl.BlockSpec((None, tm, 3 * D_DIL), lambda i, j: (i, j, 0)),
                   pl.BlockSpec((None, tm, 3 * D_SB), lambda i, j: (i, j, 0)),
                   pl.BlockSpec((None, 2 * N_ML, tm), lambda i, j: (i, 0, j))),
        compiler_params=_cparams(("parallel", "parallel")),
        name="norm_inproj",
    )(x, mod, w_ml, w_dil, w_sb)


def _mlstm_kernel(u_ref, grow_ref, convw_ref, convb_ref, wq_ref, wk_ref, wv_ref, gbc_ref, gbr_ref,
                  y_ref, prev_ref, c_ref, n_ref, m_ref):
    L = CHUNK

    @pl.when(pl.program_id(1) == 0)
    def _():
        prev_ref[...] = jnp.zeros_like(prev_ref)
        c_ref[...] = jnp.zeros_like(c_ref)
        n_ref[...] = jnp.zeros_like(n_ref)
        m_ref[...] = jnp.zeros_like(m_ref)

    row = _iota((L, D_ML), 0)
    convw = convw_ref[...]
    ltri = (_iota((L, L), 1) <= _iota((L, L), 0)).astype(bf16)
    utri = (_iota((L, L), 0) <= _iota((L, L), 1)).astype(bf16)
    causal = _iota((L, L), 1) <= _iota((L, L), 0)
    lane = _iota((1, D_ML), 1)
    heads = range(N_ML)
    mhs = [(lane >= hd * HEAD_DIM) & (lane < (hd + 1) * HEAD_DIM) for hd in heads]
    same_head = (_iota((D_ML, D_ML), 0) // HEAD_DIM) == (_iota((D_ML, D_ML), 1) // HEAD_DIM)

    def logsig(t):
        return jnp.minimum(t, 0.0) - jnp.log1p(jnp.exp(-jnp.abs(t)))

    def chain(bb):
        u = u_ref[bb]
        xm = u[:, :D_ML]
        og = u[:, D_ML:2 * D_ML]
        gcol = u[:, 2 * D_ML:] + gbc_ref[...]
        grow = grow_ref[bb] + gbr_ref[...]
        prev = prev_ref[bb]
        xc = xm * convw[CONV_WIDTH - 1:CONV_WIDTH, :]
        for sft in range(1, CONV_WIDTH):
            shifted = jnp.where(row < sft, pltpu.roll(prev, sft, 0), pltpu.roll(xm, sft, 0))
            xc = xc + shifted * convw[CONV_WIDTH - 1 - sft:CONV_WIDTH - sft, :]
        prev_ref[bb] = xm
        xc = xc + convb_ref[...]
        xc = xc * lax.logistic(xc)
        xcb = xc.astype(bf16)
        wide = lambda col: jnp.broadcast_to(col, (L, LANES))
        twice = lambda a: jnp.concatenate([a, a], axis=1)
        lf_wide = [wide(logsig(gcol[:, N_ML + hd:N_ML + hd + 1])) for hd in heads]
        ig_wide = [wide(gcol[:, hd:hd + 1]) for hd in heads]
        lf_split = [_split_bf16(x) for x in lf_wide]
        rh, rl = _split_bf16(logsig(grow))
        yield
        q = _dot(xcb, wq_ref[...])
        k = _dot(xcb, wk_ref[...])
        v = _dot(xm.astype(bf16), wv_ref[...])
        cum_wide = [_dot(ltri, hi) + _dot(ltri, lo) for hi, lo in lf_split]
        cum_row = _dot(rh, utri) + _dot(rl, utri)
        kb = k.astype(bf16)
        vb = v.astype(bf16)
        c_bd = c_ref[bb]
        n_st = n_ref[bb]
        m_st = m_ref[bb]
        c_bdb = c_bd.astype(bf16)
        qhs = [jnp.where(mhs[hd], q, 0.0) for hd in heads]
        qhb = [qh.astype(bf16) for qh in qhs]
        qn_split = [_split_bf16(qh * n_st) for qh in qhs]
        ones_ml = jnp.ones((D_ML, LANES), bf16)
        ones_l = jnp.ones((L, LANES), bf16)
        yield
        qk = [_dot_nt(qhb[hd], kb) for hd in heads]
        qc = [_dot(qhb[hd], c_bdb) for hd in heads]
        qn = [_dot(hi, ones_ml) + _dot(lo, ones_ml) for hi, lo in qn_split]
        yield
        m_ts, inters, sws = [], [], []
        wfull = jnp.zeros((L, D_ML), f32)
        decay_l = jnp.zeros((1, D_ML), f32)
        mnew_l = jnp.zeros((1, D_ML), f32)
        for hd in heads:
            cumc = cum_wide[hd]
            cumr = cum_row[N_ML + hd:N_ML + hd + 1, :]
            igr = grow[hd:hd + 1, :]
            m_prev = m_st[:, hd * HEAD_DIM:hd * HEAD_DIM + 1]
            log_d = jnp.where(causal, cumc - cumr + igr, -jnp.inf)
            log_inter = cumc + m_prev
            m_t = jnp.maximum(log_inter, jnp.max(log_d, axis=-1, keepdims=True))
            m_ts.append(m_t)
            inters.append(jnp.exp(log_inter - m_t))
            sws.append(qk[hd] * jnp.exp(log_d - m_t))
            last = cumc[L - 1:L, :]
            log_w = last - cumc + ig_wide[hd]
            m_new = jnp.maximum(last + m_prev, jnp.max(log_w, axis=0, keepdims=True))
            wfull = jnp.where(mhs[hd], twice(jnp.exp(log_w - m_new)), wfull)
            decay_l = jnp.where(mhs[hd], twice(jnp.exp(last + m_prev - m_new)), decay_l)
            mnew_l = jnp.where(mhs[hd], twice(m_new), mnew_l)
        kw = k * wfull
        sw_split = [_split_bf16(sw) for sw in sws]
        kw16 = kw.astype(bf16)
        yield
        swv = [_dot(sw_split[hd][0], vb) for hd in heads]
        swsum = [_dot(hi, ones_l) + _dot(lo, ones_l) for hi, lo in sw_split]
        upd = lax.dot_general(kw16, vb, _TN, preferred_element_type=f32)
        yield
        hmix = jnp.zeros((L, D_ML), f32)
        for hd in heads:
            num = twice(inters[hd]) * qc[hd] + swv[hd]
            den = inters[hd] * qn[hd] + swsum[hd]
            hout = num / twice(jnp.maximum(jnp.abs(den), jnp.exp(-m_ts[hd])))
            hmix = jnp.where(mhs[hd], hout, hmix)
        c_ref[bb] = decay_l * c_bd + jnp.where(same_head, upd, 0.0)
        n_ref[bb] = decay_l * n_st + jnp.sum(kw, axis=0, keepdims=True)
        m_ref[bb] = mnew_l
        y_ref[bb] = lax.logistic(og) * hmix
        yield

    chains = [chain(bb) for bb in range(u_ref.shape[0])]
    for _ in range(MLSTM_STAGES):
        for ch_ in chains:
            next(ch_)


def _mlstm(u_ml, g_row, conv_w, conv_b, wq_bd, wk_bd, wv_bd, gb_col, gb_row):
    b, s, _ = u_ml.shape
    nb = MLSTM_ROWS if b % MLSTM_ROWS == 0 else 1
    full = lambda a: pl.BlockSpec(a.shape, lambda i, j: (0,) * a.ndim)
    return pl.pallas_call(
        _mlstm_kernel,
        out_shape=jax.ShapeDtypeStruct((b, s, D_ML), f32),
        grid=(b // nb, s // CHUNK),
        in_specs=[pl.BlockSpec((nb, CHUNK, ML_COLS), lambda i, j: (i, j, 0)),
                  pl.BlockSpec((nb, 2 * N_ML, CHUNK), lambda i, j: (i, 0, j)),
                  full(conv_w), full(conv_b), full(wq_bd), full(wk_bd), full(wv_bd), full(gb_col), full(gb_row)],
        out_specs=pl.BlockSpec((nb, CHUNK, D_ML), lambda i, j: (i, j, 0)),
        scratch_shapes=[pltpu.VMEM((nb, CHUNK, D_ML), f32), pltpu.VMEM((nb, D_ML, D_ML), f32),
                        pltpu.VMEM((nb, 1, D_ML), f32), pltpu.VMEM((nb, 1, D_ML), f32)],
        compiler_params=_cparams(("parallel", "arbitrary")),
        name="mlstm",
    )(u_ml, g_row, conv_w, conv_b, wq_bd, wk_bd, wv_bd, gb_col, gb_row)


def _rows(start, size, stride):
    return pl.ds(start, size) if stride == 1 else pl.ds(start, size, stride=stride)


def _dil_kernel(q_ref, k_ref, v_ref, slope_ref, o_ref, *scratch, seq, tile):
    npat = len(DILATIONS)
    kd = scratch[0:npat]
    vd = scratch[npat:2 * npat]
    op = scratch[2 * npat:3 * npat]
    ls = scratch[3 * npat:4 * npat]
    bias_ref = scratch[4 * npat]
    L = CHUNK
    ti = pl.program_id(2)
    lane = _iota((1, LANES), 1)
    m0 = lane < HEAD_DIM

    @pl.when(ti == 0)
    def _():
        row = _iota((L, L), 0)
        col = _iota((L, L), 1)
        for p, d in enumerate(DILATIONS):
            per = seq // d
            for r in range(d):
                kd[p][r * per:(r + 1) * per, :] = k_ref[_rows(r, per, d), :].astype(bf16)
                vd[p][r * per:(r + 1) * per, :] = v_ref[_rows(r, per, d), :].astype(bf16)
            for j in range(2):
                sl = slope_ref[j][0:1, :] * float(d)
                bias_ref[p, j, 0] = jnp.where(col >= row, -(sl * (row - col + DIL_SPAN).astype(f32)), -jnp.inf)
                bias_ref[p, j, 1] = jnp.full((L, L), -jnp.inf, f32)
                bias_ref[p, j, 2] = jnp.where(col <= row, -(sl * (row - col).astype(f32)), -jnp.inf)

    nunits = tile // L
    for p, d in enumerate(DILATIONS):
        per = seq // d
        blocks_per_res = tile // d // L

        def body(it, carry, p=p, d=d, per=per, blocks_per_res=blocks_per_res):
            units = []
            for c in range(DIL_UNITS):
                u = it * DIL_UNITS + c
                r = u // blocks_per_res
                blk = u % blocks_per_res
                start = blk * (L * d) + r
                q = q_ref[_rows(start, L, d), :].astype(bf16)
                m_first = (ti * tile) // d + blk * L
                base = r * per + m_first
                has_prev = m_first > 0
                cur0 = pl.multiple_of(base, L)
                prev0 = pl.multiple_of(jnp.where(has_prev, base - L, base), L)
                units.append(dict(start=start, q=q, pk=jnp.where(has_prev, 0, 1),
                                  kc=kd[p][pl.ds(cur0, L), :], vc=vd[p][pl.ds(cur0, L), :],
                                  kp=kd[p][pl.ds(prev0, L), :], vp=vd[p][pl.ds(prev0, L), :]))
            chains = [(un, j) for un in units for j in range(2)]
            scores = []
            for un, j in chains:
                qj = jnp.where(m0 if j == 0 else jnp.logical_not(m0), un["q"], jnp.zeros_like(un["q"]))
                scores.append((_dot_nt(qj, un["kp"]) + bias_ref[p, j, un["pk"]],
                               _dot_nt(qj, un["kc"]) + bias_ref[p, j, 2]))
            probs = []
            for sp, sc in scores:
                mx = jnp.max(jnp.maximum(sp, sc), axis=-1, keepdims=True)
                pp = jnp.exp(sp - mx)
                pc = jnp.exp(sc - mx)
                den = jnp.sum(pp + pc, axis=-1, keepdims=True)
                probs.append((pp.astype(bf16), pc.astype(bf16), den, mx + jnp.log(den)))
            outs = [(_dot(pp, un["vp"]) + _dot(pc, un["vc"])) / den
                    for (un, j), (pp, pc, den, _) in zip(chains, probs)]
            for c, un in enumerate(units):
                op[p][_rows(un["start"], L, d), :] = jnp.where(m0, outs[2 * c], outs[2 * c + 1])
                ls[p][_rows(un["start"], L, d), :] = jnp.where(m0, probs[2 * c][3], probs[2 * c + 1][3])
            return carry

        lax.fori_loop(0, nunits // DIL_UNITS, body, 0)

    l0, l1, l2 = ls[0][...], ls[1][...], ls[2][...]
    mx = jnp.maximum(jnp.maximum(l0, l1), l2)
    e0, e1, e2 = jnp.exp(l0 - mx), jnp.exp(l1 - mx), jnp.exp(l2 - mx)
    o_ref[...] = ((e0 * op[0][...] + e1 * op[1][...] + e2 * op[2][...]) / (e0 + e1 + e2)).astype(o_ref.dtype)


def _dilated(u_dil, slopes):
    b, s, _ = u_dil.shape
    npair = N_DIL // 2
    tile = DILATIONS[-1] * CHUNK
    assert s % tile == 0 and (tile // CHUNK) % DIL_UNITS == 0
    seq_buf = lambda dt: pltpu.VMEM((s, LANES), dt)
    tile_buf = pltpu.VMEM((tile, LANES), f32)
    npat = len(DILATIONS)
    return pl.pallas_call(
        functools.partial(_dil_kernel, seq=s, tile=tile),
        out_shape=jax.ShapeDtypeStruct((b, s, D_DIL), bf16),
        grid=(b, npair, s // tile),
        in_specs=[pl.BlockSpec((None, tile, LANES), lambda i, h, n: (i, n, h)),
                  pl.BlockSpec((None, s, LANES), lambda i, h, n: (i, 0, npair + h)),
                  pl.BlockSpec((None, s, LANES), lambda i, h, n: (i, 0, 2 * npair + h)),
                  pl.BlockSpec((None, 2, 8, LANES), lambda i, h, n: (h, 0, 0, 0))],
        out_specs=pl.BlockSpec((None, tile, LANES), lambda i, h, n: (i, n, h)),
        scratch_shapes=([seq_buf(bf16)] * (2 * npat) + [tile_buf] * (2 * npat)
                        + [pltpu.VMEM((npat, 2, 3, CHUNK, CHUNK), f32)]),
        compiler_params=_cparams(("parallel", "parallel", "arbitrary")),
        name="dilated",
    )(u_dil, u_dil, u_dil, slopes)


def _sb_kernel(q_ref, k_ref, v_ref, o_ref, qm_ref, r_ref, acc_ref, *, nq):
    L = CHUNK
    tile0 = pl.program_id(1) * nq
    row = _iota((L, L), 0)
    col = _iota((L, L), 1)
    strict = col < row
    lane = _iota((1, LANES), 1)
    m0 = lane < HEAD_DIM
    npair = N_SB // 2
    csum = jnp.concatenate([(_iota((L, L), 0) > _iota((L, L), 1)).astype(bf16), jnp.ones((L, L), bf16)], axis=1)

    for qb in range(nq):
        for hp in range(npair):
            q = q_ref[qb * L:(qb + 1) * L, hp * LANES:(hp + 1) * LANES]
            qm_ref[(qb * npair + hp) * 2] = jnp.where(m0, q, jnp.zeros_like(q))
            qm_ref[(qb * npair + hp) * 2 + 1] = jnp.where(m0, jnp.zeros_like(q), q)

    def sweep(it, diagonal):
        rmax = None
        for g0 in range(0, nq, SB_GROUP):
            chains = []
            for qb in range(g0, g0 + SB_GROUP):
                qi = tile0 + qb
                k0 = pl.multiple_of(jnp.maximum(qi - it, 0) * L, L)
                out_of_keys = jnp.where(qi - it <= 0, SB_NO_KEYS, 0.0)
                for hp in range(npair):
                    kblk = k_ref[pl.ds(k0, L), hp * LANES:(hp + 1) * LANES]
                    vblk = v_ref[pl.ds(k0, L), hp * LANES:(hp + 1) * LANES]
                    for j in range(2):
                        chains.append(((qb * npair + hp) * 2 + j, kblk, vblk, out_of_keys))
            zs = [_dot_nt(qm_ref[c], kblk) for c, kblk, _, _ in chains]
            zl, lkb = [], []
            for z in zs:
                ls = jnp.minimum(z, 0.0) - jnp.log(1.0 + jnp.exp(-jnp.abs(z)))
                lk = ls - z
                zl.append(ls)
                lkb.append((jnp.where(strict, lk, 0.0) if diagonal else lk).astype(bf16))
            ts = [_dot(x, csum) for x in lkb]
            avs = []
            for (c, _, _, out_of_keys), s, t in zip(chains, zl, ts):
                if diagonal:
                    a = jnp.where(strict, jnp.exp(s + t[:, :L]), 0.0)
                    rs = t[:, L:] + out_of_keys
                else:
                    a = jnp.exp(s + r_ref[c] + t[:, :L])
                    rs = r_ref[c] + t[:, L:] + out_of_keys
                avs.append(a.astype(bf16))
                r_ref[c] = rs
                rmax = rs if rmax is None else jnp.maximum(rmax, rs)
            for (c, _, vblk, _), a in zip(chains, avs):
                if diagonal:
                    acc_ref[c] = _dot(a, vblk)
                else:
                    acc_ref[c] += _dot(a, vblk)
        return jnp.max(rmax) > SB_DEAD_LOG

    alive0 = sweep(0, True)

    def body(carry):
        _, it = carry
        return sweep(it, False), it + 1

    lax.while_loop(lambda carry: carry[0], body, (alive0, jnp.int32(1)))

    for qb in range(nq):
        for hp in range(npair):
            c = (qb * npair + hp) * 2
            o_ref[qb * L:(qb + 1) * L, hp * LANES:(hp + 1) * LANES] = (
                jnp.where(m0, acc_ref[c], acc_ref[c + 1]).astype(o_ref.dtype))


def _stickbreak(u_sb):
    b, s, _ = u_sb.shape
    nq = SB_QBLOCKS
    tq = nq * CHUNK
    nchain = nq * N_SB
    return pl.pallas_call(
        functools.partial(_sb_kernel, nq=nq),
        out_shape=jax.ShapeDtypeStruct((b, s, D_SB), bf16),
        grid=(b, s // tq),
        in_specs=[pl.BlockSpec((None, tq, D_SB), lambda i, n: (i, n, 0)),
                  pl.BlockSpec((None, s, D_SB), lambda i, n: (i, 0, 1)),
                  pl.BlockSpec((None, s, D_SB), lambda i, n: (i, 0, 2))],
        out_specs=pl.BlockSpec((None, tq, D_SB), lambda i, n: (i, n, 0)),
        scratch_shapes=[pltpu.VMEM((nchain, CHUNK, LANES), bf16),
                        pltpu.VMEM((nchain, CHUNK, CHUNK), f32),
                        pltpu.VMEM((nchain, CHUNK, LANES), f32)],
        compiler_params=_cparams(("parallel", "arbitrary")),
        name="stickbreak",
    )(u_sb, u_sb, u_sb)


def _post_kernel(x_ref, yml_ref, ydil_ref, ysb_ref, mod_ref, gh_ref, wout_ref, wr_hi_ref, wr_lo_ref, rb_ref,
                 xo_ref, h_ref, selt_ref, w_ref):
    mod = mod_ref[...]
    y = jnp.concatenate([yml_ref[...], ydil_ref[...].astype(f32), ysb_ref[...].astype(f32)], axis=1)
    pair = ((_iota((LANES, LANES), 0) // HEAD_DIM) == (_iota((LANES, LANES), 1) // HEAD_DIM)).astype(bf16)
    cols = []
    for cblk in range(y.shape[1] // LANES):
        yb = y[:, cblk * LANES:(cblk + 1) * LANES]
        hi, lo = _split_bf16(yb * yb)
        ms = (_dot(hi, pair) + _dot(lo, pair)) * (1.0 / HEAD_DIM)
        cols.append(yb * lax.rsqrt(ms + EPS))
    yn = (jnp.concatenate(cols, axis=1) * gh_ref[...]).astype(bf16)
    xn = x_ref[...] + mod[2:3, :] * _dot(yn, wout_ref[...])
    xo_ref[...] = xn
    h2 = _rms(xn) * (1.0 + mod[4:5, :]) + mod[3:4, :]
    hh, hl = _split_bf16(h2)
    h_ref[...] = h2
    tm = x_ref.shape[0]
    row = _iota((SUBLANES, tm), 0)
    grow = row < N_GROUPS
    logits = _dot(hh, wr_hi_ref[...]) + _dot(hl, wr_hi_ref[...]) + _dot(hh, wr_lo_ref[...])
    logits_t = logits.T
    sc, bs = [], []
    for j in range(EPG):
        sc.append(lax.logistic(logits_t[j * SUBLANES:(j + 1) * SUBLANES, :]))
        bs.append(jnp.where(grow, sc[j] + rb_ref[j][:, 0:1], -jnp.inf))
    hi01, lo01 = jnp.maximum(bs[0], bs[1]), jnp.minimum(bs[0], bs[1])
    hi23, lo23 = jnp.maximum(bs[2], bs[3]), jnp.minimum(bs[2], bs[3])
    top1 = jnp.maximum(hi01, hi23)
    top2 = jnp.maximum(jnp.minimum(hi01, hi23), jnp.maximum(lo01, lo23))
    gscore = jnp.where(grow, top1 + top2, -jnp.inf)
    gmax = jnp.max(gscore, axis=0, keepdims=True)
    gidx = jnp.min(jnp.where(gscore == gmax, row, SUBLANES), axis=0, keepdims=True)
    gsel = row == gidx
    picks, vals = [], []
    for j in range(EPG):
        rank = jnp.zeros(bs[j].shape, jnp.int32)
        for jj in range(EPG):
            if jj != j:
                ahead = (bs[jj] >= bs[j]) if jj < j else (bs[jj] > bs[j])
                rank = rank + ahead.astype(jnp.int32)
        picks.append(gsel & (rank < 2))
        vals.append(jnp.sum(jnp.where(gsel, sc[j], 0.0), axis=0, keepdims=True))
    erow = _iota((N_EXPERTS, SUBLANES), 0)
    gcol = _iota((N_EXPERTS, SUBLANES), 1)
    sel_t = jnp.zeros((N_EXPERTS, tm), f32)
    for j in range(EPG):
        place = ((erow == gcol * EPG + j) & (gcol < N_GROUPS)).astype(bf16)
        sel_t = sel_t + _dot(place, picks[j].astype(bf16))
    selt_ref[...] = sel_t.astype(selt_ref.dtype)
    seen = jnp.zeros_like(vals[0])
    w_lo = jnp.zeros_like(vals[0])
    w_hi = jnp.zeros_like(vals[0])
    for j in range(EPG):
        picked = jnp.sum(picks[j].astype(f32), axis=0, keepdims=True) > 0.0
        w_lo = w_lo + jnp.where(picked & (seen == 0.0), vals[j], 0.0)
        w_hi = w_hi + jnp.where(picked & (seen == 1.0), vals[j], 0.0)
        seen = seen + picked.astype(f32)
    tot = w_lo + w_hi
    w_ref[...] = jnp.where(row == 0, w_lo / tot, jnp.where(row == 1, w_hi / tot, 0.0))


def _post(x, y_ml, y_dil, y_sb, mod, g_head, w_out, wr_hi, wr_lo, rb, tm):
    b, s, d = x.shape
    tok = lambda w: pl.BlockSpec((None, tm, w), lambda i, j: (i, j, 0))
    full = lambda a: pl.BlockSpec(a.shape, lambda i, j: (0,) * a.ndim)
    tok_lanes = lambda rows: pl.BlockSpec((rows, tm), lambda i, j: (0, i * (s // tm) + j))
    return pl.pallas_call(
        _post_kernel,
        out_shape=(jax.ShapeDtypeStruct((b, s, d), f32),
                   jax.ShapeDtypeStruct((b, s, d), f32),
                   jax.ShapeDtypeStruct((N_EXPERTS, b * s), bf16),
                   jax.ShapeDtypeStruct((SUBLANES, b * s), f32)),
        grid=(b, s // tm),
        in_specs=[tok(d), tok(D_ML), tok(D_DIL), tok(D_SB), pl.BlockSpec((None, 6, d), lambda i, j: (i, 0, 0)),
                  full(g_head), full(w_out), full(wr_hi), full(wr_lo), full(rb)],
        out_specs=(tok(d), tok(d), tok_lanes(N_EXPERTS), tok_lanes(SUBLANES)),
        compiler_params=_cparams(("parallel", "parallel")),
        name="merge_outproj_router",
    )(x, y_ml, y_dil, y_sb, mod, g_head, w_out, wr_hi, wr_lo, rb)


def _plan_kernel(selt_ref, lpos0_ref, lpos1_ref, base_ref, nchunk_ref, lofs_ref, tiles_ref, tails_ref,
                 before_ref, cnt_ref, lofs_sc, *, ntw, ntile_pad, nrow_tiles):
    phase = pl.program_id(0)
    step = pl.program_id(1)
    tk = MOE_TOK
    sel = selt_ref[...]
    tlane = _iota((1, ntw), 1)

    @pl.when((phase == 0) & (step == 0))
    def _():
        cnt_ref[...] = jnp.zeros_like(cnt_ref)
        before_ref[...] = (_iota((tk, tk), 0) < _iota((tk, tk), 1)).astype(bf16)

    @pl.when(phase == 0)
    def _():
        cnt = cnt_ref[...]
        for j in range(PLAN_GROUP):
            here = _dot(sel[:, j * tk:(j + 1) * tk], jnp.ones((tk, ntw), bf16))
            cnt = jnp.where(tlane == step * PLAN_GROUP + j, here, cnt)
        cnt_ref[...] = cnt
        lpos0_ref[...] = jnp.zeros_like(lpos0_ref)
        lpos1_ref[...] = jnp.zeros_like(lpos1_ref)

    @pl.when((phase == 1) & (step == 0))
    def _():
        cnt = cnt_ref[...]
        nch = jnp.floor((cnt + (MOE_CHUNK - 1.0)) * (1.0 / MOE_CHUNK))
        nchb = nch.astype(bf16)
        earlier_tiles = (_iota((ntw, ntw), 0) < _iota((ntw, ntw), 1)).astype(bf16)
        earlier_experts = (_iota((N_EXPERTS, N_EXPERTS), 1) < _iota((N_EXPERTS, N_EXPERTS), 0)).astype(bf16)
        in_expert = _dot(nchb, earlier_tiles) * MOE_CHUNK
        lofs = _dot(earlier_experts, nchb) * MOE_CHUNK
        total = _dot(nchb, jnp.ones((ntw, LANES), bf16)) * MOE_CHUNK
        padded = jnp.floor((total + (MOE_ROWS - 1.0)) * (1.0 / MOE_ROWS)) * MOE_ROWS
        rows, acc = [], jnp.zeros((1, LANES), f32)
        for e in range(N_EXPERTS):
            rows.append(acc)
            acc = acc + padded[e:e + 1, :]
        off = jnp.concatenate(rows, axis=0)
        base_ref[...] = (off[:, 0:1] + in_expert).astype(jnp.int32)
        nchunk_ref[...] = nch.astype(jnp.int32)
        lofs_ref[...] = lofs.astype(jnp.int32)
        lofs_sc[...] = lofs
        lane = _iota((1, LANES), 1)
        tails_ref[...] = jnp.where(
            lane == 0, off + total,
            jnp.where(lane == 1, (padded - total) * (1.0 / MOE_CHUNK),
                      jnp.where(lane == 2, acc, jnp.where(lane == 3, nrow_tiles - acc * (1.0 / MOE_ROWS), 0.0)))
        ).astype(jnp.int32)
        start = _iota((1, ntile_pad), 1).astype(f32) * MOE_ROWS
        ends = (off + padded)[:, 0:1]
        owner = jnp.sum((start >= ends).astype(f32), axis=0, keepdims=True)
        tiles_ref[...] = jnp.zeros_like(tiles_ref)
        tiles_ref[0:1, :] = jnp.minimum(owner, N_EXPERTS - 1.0).astype(jnp.int32)
        tiles_ref[1:2, :] = (start < acc[:, 0:1]).astype(jnp.int32)

    @pl.when(phase == 1)
    def _():
        for j in range(PLAN_GROUP):
            cols = slice(j * tk, (j + 1) * tk)
            mine = jnp.sum(jnp.where(tlane == step * PLAN_GROUP + j, lofs_sc[...], 0.0), axis=1, keepdims=True)
            pos = _dot(sel[:, cols], before_ref[...]) + mine
            picked = sel[:, cols] > 0
            lpos0_ref[:, cols] = jnp.min(jnp.where(picked, pos, 1e9), axis=0, keepdims=True).astype(jnp.int32)
            lpos1_ref[:, cols] = jnp.max(jnp.where(picked, pos, -1.0), axis=0, keepdims=True).astype(jnp.int32)


def _plan(sel_t, nrow_tiles):
    ne, t = sel_t.shape
    nt = t // MOE_TOK
    ntw = -(-nt // LANES) * LANES
    ntile_pad = -(-nrow_tiles // LANES) * LANES
    i32 = jnp.int32
    whole = lambda shape: pl.BlockSpec(shape, lambda p, i: (0, 0))
    per_tile = pl.BlockSpec((1, PLAN_GROUP * MOE_TOK), lambda p, i: (0, i * p))
    lpos0, lpos1, base, nchunk, lofs, tiles, tails = pl.pallas_call(
        functools.partial(_plan_kernel, ntw=ntw, ntile_pad=ntile_pad, nrow_tiles=nrow_tiles),
        out_shape=(jax.ShapeDtypeStruct((1, t), i32), jax.ShapeDtypeStruct((1, t), i32),
                   jax.ShapeDtypeStruct((ne, ntw), i32), jax.ShapeDtypeStruct((ne, ntw), i32),
                   jax.ShapeDtypeStruct((ne, ntw), i32), jax.ShapeDtypeStruct((8, ntile_pad), i32),
                   jax.ShapeDtypeStruct((ne, LANES), i32)),
        grid=(2, nt // PLAN_GROUP),
        in_specs=[pl.BlockSpec((ne, PLAN_GROUP * MOE_TOK), lambda p, i: (0, i))],
        out_specs=(per_tile, per_tile, whole((ne, ntw)), whole((ne, ntw)), whole((ne, ntw)),
                   whole((8, ntile_pad)), whole((ne, LANES))),
        scratch_shapes=[pltpu.VMEM((MOE_TOK, MOE_TOK), bf16), pltpu.VMEM((ne, ntw), f32),
                        pltpu.VMEM((ne, ntw), f32)],
        compiler_params=_cparams(("arbitrary", "arbitrary")),
        name="moe_plan",
    )(sel_t)
    flat = lambda a: a[:, :nt].T.reshape(nt * ne)
    return dict(lpos0=lpos0, lpos1=lpos1, base=flat(base), nchunk=flat(nchunk), lofs=flat(lofs),
                owner=tiles[0, :nrow_tiles], used=tiles[1, :nrow_tiles],
                tail_start=tails[:, 0], tail_chunks=tails[:, 1], unused=tails[0, 2:4])


def _segment_chunks(tile, base_ref, nchunk_ref, lofs_ref, make_copy, start):
    for e in range(N_EXPERTS):
        idx = tile * N_EXPERTS + e
        first_global = base_ref[idx]
        first_local = lofs_ref[idx]

        def one(c, carry, e=e, first_global=first_global, first_local=first_local):
            cp = make_copy(pl.multiple_of(first_local + c * MOE_CHUNK, MOE_CHUNK),
                           pl.multiple_of(first_global + c * MOE_CHUNK, MOE_CHUNK), e % 2)
            cp.start(priority=e % 2) if start else cp.wait()
            return carry

        lax.fori_loop(0, nchunk_ref[idx], one, 0)


def _one_hot_rows(lpos_ref):
    return (_iota((MOE_LOCAL, MOE_TOK), 0) == lpos_ref[...]).astype(bf16)


def _dispatch_kernel(base_ref, nchunk_ref, lofs_ref, tstart_ref, tchunks_ref, unused_ref, lpos0_ref, lpos1_ref,
                     h_ref, xs_hbm, loc, zeros, sem):
    tile = pl.program_id(0)
    place = _one_hot_rows(lpos0_ref) + _one_hot_rows(lpos1_ref)
    loc[...] = _dot(place, h_ref[...].astype(bf16))

    def seg_copy(local_row, global_row, s):
        return pltpu.make_async_copy(loc.at[pl.ds(local_row, MOE_CHUNK)], xs_hbm.at[pl.ds(global_row, MOE_CHUNK)],
                                     sem.at[s])

    _segment_chunks(tile, base_ref, nchunk_ref, lofs_ref, seg_copy, True)
    _segment_chunks(tile, base_ref, nchunk_ref, lofs_ref, seg_copy, False)

    @pl.when(tile == pl.num_programs(0) - 1)
    def _():
        zeros[...] = jnp.zeros_like(zeros)
        for start in (True, False):
            for e in range(N_EXPERTS):
                def one(c, carry, e=e, start=start):
                    row = pl.multiple_of(tstart_ref[e] + c * MOE_CHUNK, MOE_CHUNK)
                    cp = pltpu.make_async_copy(zeros.at[pl.ds(0, MOE_CHUNK)], xs_hbm.at[pl.ds(row, MOE_CHUNK)],
                                               sem.at[e % 2])
                    cp.start() if start else cp.wait()
                    return carry
                lax.fori_loop(0, tchunks_ref[e], one, 0)

            def whole_tile(c, carry, start=start):
                row = pl.multiple_of(unused_ref[0] + c * MOE_ROWS, MOE_ROWS)
                cp = pltpu.make_async_copy(zeros, xs_hbm.at[pl.ds(row, MOE_ROWS)], sem.at[0])
                cp.start() if start else cp.wait()
                return carry
            lax.fori_loop(0, unused_ref[1], whole_tile, 0)


def _dispatch(h_rows, plan, nrows):
    t, w = h_rows.shape
    per_tile = lambda: pl.BlockSpec((1, MOE_TOK), lambda i, *_: (0, i))
    return pl.pallas_call(
        _dispatch_kernel,
        out_shape=jax.ShapeDtypeStruct((nrows, w), h_rows.dtype),
        grid_spec=pltpu.PrefetchScalarGridSpec(
            num_scalar_prefetch=6,
            grid=(t // MOE_TOK,),
            in_specs=[per_tile(), per_tile(), pl.BlockSpec((MOE_TOK, w), lambda i, *_: (i, 0))],
            out_specs=pl.BlockSpec(memory_space=pl.ANY),
            scratch_shapes=[pltpu.VMEM((MOE_LOCAL, w), f32), pltpu.VMEM((MOE_ROWS, w), f32),
                            pltpu.SemaphoreType.DMA((2,))]),
        compiler_params=_cparams(("arbitrary",)),
        name="moe_dispatch",
    )(plan["base"], plan["nchunk"], plan["lofs"], plan["tail_start"], plan["tail_chunks"], plan["unused"],
      plan["lpos0"], plan["lpos1"], h_rows)


def _experts_kernel(owner_ref, used_ref, xs_ref, wg_ref, wu_ref, wd_ref, ys_ref):
    tile = pl.program_id(0)

    @pl.when(used_ref[tile] > 0)
    def _():
        xb = xs_ref[...].astype(bf16)
        g = _dot(xb, wg_ref[...].astype(bf16))
        he = g * lax.logistic(g) * _dot(xb, wu_ref[...].astype(bf16))
        ys_ref[...] = _dot(he.astype(bf16), wd_ref[...].astype(bf16))

    @pl.when(used_ref[tile] == 0)
    def _():
        ys_ref[...] = jnp.zeros_like(ys_ref)


def _experts(xs, owner, used, wg, wu, wd):
    nrows, w = xs.shape
    _, d, de = wg.shape
    return pl.pallas_call(
        _experts_kernel,
        out_shape=jax.ShapeDtypeStruct((nrows, w), xs.dtype),
        grid_spec=pltpu.PrefetchScalarGridSpec(
            num_scalar_prefetch=2,
            grid=(nrows // MOE_ROWS,),
            in_specs=[pl.BlockSpec((MOE_ROWS, w), lambda g, own, use: (g * use[g], 0)),
                      pl.BlockSpec((None, d, de), lambda g, own, use: (own[g], 0, 0)),
                      pl.BlockSpec((None, d, de), lambda g, own, use: (own[g], 0, 0)),
                      pl.BlockSpec((None, de, d), lambda g, own, use: (own[g], 0, 0))],
            out_specs=pl.BlockSpec((MOE_ROWS, w), lambda g, own, use: (g, 0))),
        compiler_params=_cparams(("arbitrary",)),
        name="moe_experts",
    )(owner, used, xs, wg, wu, wd)


def _combine_kernel(base_ref, nchunk_ref, lofs_ref, lpos0_ref, lpos1_ref, ys_hbm, x_ref, w_ref, mod_ref, gf_ref,
                    o_ref, loc, sem, *, final):
    tile = pl.program_id(0)
    ntile = pl.num_programs(0)
    slot = tile % 2

    def gather(tile_, slot_, start):
        def seg_copy(local_row, global_row, s):
            return pltpu.make_async_copy(ys_hbm.at[pl.ds(global_row, MOE_CHUNK)],
                                         loc.at[slot_, pl.ds(local_row, MOE_CHUNK)], sem.at[slot_, s])
        _segment_chunks(tile_, base_ref, nchunk_ref, lofs_ref, seg_copy, start)

    @pl.when(tile == 0)
    def _():
        loc[...] = jnp.zeros_like(loc)
        gather(0, 0, True)

    @pl.when(tile + 1 < ntile)
    def _():
        gather(tile + 1, 1 - slot, True)

    gather(tile, slot, False)
    ysl = loc[slot].astype(bf16)
    prow = _iota((MOE_LOCAL, MOE_TOK), 0)
    gates = (jnp.where(prow == lpos0_ref[...], w_ref[0:1, :], 0.0)
             + jnp.where(prow == lpos1_ref[...], w_ref[1:2, :], 0.0))
    g_hi, g_lo = _split_bf16(gates)
    y = (lax.dot_general(g_hi, ysl, _TN, preferred_element_type=f32)
         + lax.dot_general(g_lo, ysl, _TN, preferred_element_type=f32))
    xn = x_ref[...] + mod_ref[...][5:6, :] * y
    if final:
        xn = _rms(xn) * gf_ref[...]
    o_ref[...] = xn


def _combine(ys, plan, x, w_pair, mod, g_final, final):
    b, s, d = x.shape
    t = b * s
    per_batch = s // MOE_TOK
    per_tile = lambda: pl.BlockSpec((1, MOE_TOK), lambda i, *_: (0, i))
    out = pl.pallas_call(
        functools.partial(_combine_kernel, final=final),
        out_shape=jax.ShapeDtypeStruct((t, d), f32),
        grid_spec=pltpu.PrefetchScalarGridSpec(
            num_scalar_prefetch=3,
            grid=(t // MOE_TOK,),
            in_specs=[per_tile(), per_tile(), pl.BlockSpec(memory_space=pl.ANY),
                      pl.BlockSpec((MOE_TOK, d), lambda i, *_: (i, 0)),
                      pl.BlockSpec((SUBLANES, MOE_TOK), lambda i, *_: (0, i)),
                      pl.BlockSpec((None, 6, d), lambda i, *_: (i // per_batch, 0, 0)),
                      pl.BlockSpec((1, d), lambda i, *_: (0, 0))],
            out_specs=pl.BlockSpec((MOE_TOK, d), lambda i, *_: (i, 0)),
            scratch_shapes=[pltpu.VMEM((2, MOE_LOCAL, d), f32), pltpu.SemaphoreType.DMA((2, 2))]),
        compiler_params=_cparams(("arbitrary",)),
        name="moe_combine_final" if final else "moe_combine",
    )(plan["base"], plan["nchunk"], plan["lofs"], plan["lpos0"], plan["lpos1"], ys,
      x.reshape(t, d), w_pair, mod, g_final)
    return out.reshape(b, s, d)


def _moe(x, h_rows, sel_t, w_pair, mod, wg, wu, wd, g_final, final):
    b, s, d = x.shape
    t = b * s
    worst = 2 * t + (t // MOE_TOK) * N_EXPERTS * (MOE_CHUNK - 1) + N_EXPERTS * MOE_ROWS
    nrows = -(-worst // MOE_ROWS) * MOE_ROWS
    plan = _plan(sel_t, nrows // MOE_ROWS)
    xs = _dispatch(h_rows.reshape(t, d), plan, nrows)
    ys = _experts(xs, plan["owner"], plan["used"], wg, wu, wd)
    return _combine(ys, plan, x, w_pair, mod, g_final, final)


def _block_diag(w):
    n, a, c = w.shape
    out = jnp.zeros((n * a, n * c), w.dtype)
    for i in range(n):
        out = out.at[i * a:(i + 1) * a, i * c:(i + 1) * c].set(w[i])
    return out


def kernel(x, c, w_in, conv_w, conv_b, w_mq, w_mk, w_mv, gate_bias, g_head, w_out, w_ada, b_ada,
           w_router, router_bias, w_gate_e, w_up_e, w_down_e, g_final):
    b, s, d = x.shape
    depth = w_in.shape[0]
    qscale = HEAD_DIM ** -0.5
    tm = min(512, s)

    mod_all = _ada(c, w_ada, b_ada).reshape(depth, b, 6, d)

    wr = jnp.zeros((d, LANES // SUBLANES, SUBLANES), f32).at[:, :EPG, :N_GROUPS].set(
        w_router.reshape(d, N_GROUPS, EPG).transpose(0, 2, 1)).reshape(d, LANES)
    wr_hi = wr.astype(bf16)
    wr_lo = (wr - wr_hi.astype(f32)).astype(bf16)
    rb = jnp.zeros((EPG, SUBLANES, LANES), f32).at[:, :N_GROUPS, :].set(
        jnp.broadcast_to(router_bias.reshape(N_GROUPS, EPG).T[:, :, None], (EPG, N_GROUPS, LANES)))

    slope = jnp.exp2(-ALIBI_MAX_BIAS * jnp.arange(1, N_DIL + 1, dtype=f32) / N_DIL)
    slopes = jnp.broadcast_to(slope.reshape(N_DIL // 2, 2, 1, 1), (N_DIL // 2, 2, 8, LANES))
    g_final2 = g_final.reshape(1, d)

    w_ml_all, w_dil_all, w_sb_all = _wprep(w_in, min(256, d))

    for l in range(depth):
        mod = mod_all[l]

        u_ml, u_dil, u_sb, g_row = _inproj(x, mod, w_ml_all[l], w_dil_all[l], w_sb_all[l], tm)

        gb = gate_bias[l]
        gb_col = jnp.zeros((1, LANES), f32).at[0, :2 * N_ML].set(gb.reshape(-1))
        gb_row = gb.reshape(2 * N_ML, 1)
        y_ml = _mlstm(u_ml, g_row, conv_w[l], conv_b[l].reshape(1, D_ML),
                      _block_diag(w_mq[l]).astype(bf16), (_block_diag(w_mk[l]) * qscale).astype(bf16),
                      _block_diag(w_mv[l]).astype(bf16), gb_col, gb_row)
        y_dil = _dilated(u_dil, slopes)
        y_sb = _stickbreak(u_sb)

        x, h_rows, sel_t, w_pair = _post(x, y_ml, y_dil, y_sb, mod, g_head[l].reshape(1, d),
                                           w_out[l].astype(bf16), wr_hi, wr_lo, rb, tm)
        x = _moe(x, h_rows, sel_t, w_pair, mod, w_gate_e[l], w_up_e[l], w_down_e[l], g_final2,
                 final=(l == depth - 1))
    return x
```

```python
import functools

import jax
import jax.numpy as jnp
from jax import lax
from jax.experimental import pallas as pl
from jax.experimental.pallas import tpu as pltpu

HEAD_DIM = 64
N_ML = 4
N_DIL = 6
N_SB = 6
D_ML = N_ML * HEAD_DIM
D_DIL = N_DIL * HEAD_DIM
D_SB = N_SB * HEAD_DIM
CONV_WIDTH = 4
CHUNK = 128
DILATIONS = (1, 4, 16)
DIL_SPAN = 128
ALIBI_MAX_BIAS = 8.0
N_EXPERTS = 16
N_GROUPS = 4
EPG = N_EXPERTS // N_GROUPS
EPS = 1e-6
LANES = 128
ML_COLS = 2 * D_ML + LANES
SB_DEAD_LOG = -104.0
SB_NO_KEYS = -1e30
SB_QBLOCKS = 4
SB_GROUP = 2
DIL_UNITS = 8
MOE_ROWS = 512
MOE_TOK = 256
MOE_CHUNK = 8
PLAN_GROUP = 4
SUBLANES = 8
MLSTM_ROWS = 2
MLSTM_STAGES = 6
MOE_LOCAL = 2 * MOE_TOK + N_EXPERTS * MOE_CHUNK
VMEM_LIMIT = 56 * 1024 * 1024

_NT = (((1,), (1,)), ((), ()))
_TN = (((0,), (0,)), ((), ()))

f32 = jnp.float32
bf16 = jnp.bfloat16


def _dot(a, b):
    return jnp.dot(a, b, preferred_element_type=f32)


def _dot_nt(a, b):
    return lax.dot_general(a, b, _NT, preferred_element_type=f32)


def _split_bf16(x):
    hi = x.astype(bf16)
    lo = (x - hi.astype(f32)).astype(bf16)
    return hi, lo


def _iota(shape, dim):
    return lax.broadcasted_iota(jnp.int32, shape, dim)


def _rms(x):
    return x * lax.rsqrt(jnp.mean(x * x, axis=-1, keepdims=True) + EPS)


def _cparams(sem):
    return pltpu.CompilerParams(dimension_semantics=sem, vmem_limit_bytes=VMEM_LIMIT)


def _ada_kernel(c_ref, w_ref, b_ref, o_ref):
    c = c_ref[...]
    c_act = (c * lax.logistic(c)).astype(bf16)
    o_ref[...] = _dot(c_act, w_ref[...].astype(bf16)) + b_ref[...]


def _ada(c, w_ada, b_ada):
    depth, d, nd = w_ada.shape
    b = c.shape[0]
    nblk = nd // d
    return pl.pallas_call(
        _ada_kernel,
        out_shape=jax.ShapeDtypeStruct((depth, b, nd), f32),
        grid=(depth, nblk),
        in_specs=[pl.BlockSpec((b, d), lambda l, j: (0, 0)),
                  pl.BlockSpec((None, d, d), lambda l, j: (l, 0, j)),
                  pl.BlockSpec((None, 1, d), lambda l, j: (l, 0, j))],
        out_specs=pl.BlockSpec((None, b, d), lambda l, j: (l, 0, j)),
        compiler_params=_cparams(("parallel", "parallel")),
        name="ada_mod",
    )(c, w_ada, b_ada.reshape(depth, 1, nd))


def _wprep_kernel(w_ref, wml_ref, wdil_ref, wsb_ref):
    w = w_ref[...]
    off_dil = 2 * D_ML + 2 * N_ML
    off_sb = off_dil + 3 * D_DIL
    qs = HEAD_DIM ** -0.5
    ml = w[:, :ML_COLS]
    wml_ref[...] = jnp.where(_iota((1, ML_COLS), 1) < off_dil, ml, 0.0).astype(bf16)
    q_cols = jnp.where(_iota((1, 3 * D_DIL), 1) < D_DIL, qs, 1.0)
    wdil_ref[...] = (w[:, off_dil:off_sb] * q_cols).astype(bf16)
    wsb_ref[...] = (w[:, off_sb:off_sb + 3 * D_SB] * q_cols).astype(bf16)


def _wprep(w_in, rows):
    depth, d, d_in = w_in.shape
    assert d_in == 2 * D_ML + 2 * N_ML + 3 * D_DIL + 3 * D_SB and D_DIL == D_SB
    out = lambda w: pl.BlockSpec((None, rows, w), lambda l, i: (l, i, 0))
    return pl.pallas_call(
        _wprep_kernel,
        out_shape=(jax.ShapeDtypeStruct((depth, d, ML_COLS), bf16),
                   jax.ShapeDtypeStruct((depth, d, 3 * D_DIL), bf16),
                   jax.ShapeDtypeStruct((depth, d, 3 * D_SB), bf16)),
        grid=(depth, d // rows),
        in_specs=[pl.BlockSpec((None, rows, d_in), lambda l, i: (l, i, 0))],
        out_specs=(out(ML_COLS), out(3 * D_DIL), out(3 * D_SB)),
        compiler_params=_cparams(("parallel", "parallel")),
        name="inproj_weight_layout",
    )(w_in)


def _inproj_kernel(x_ref, mod_ref, wml_ref, wdil_ref, wsb_ref,
                   uml_ref, udil_ref, usb_ref, grow_ref):
    x = x_ref[...]
    mod = mod_ref[...]
    h = (_rms(x) * (1.0 + mod[1:2, :]) + mod[0:1, :]).astype(bf16)
    uml = _dot(h, wml_ref[...])
    uml_ref[...] = uml
    udil_ref[...] = _dot(h, wdil_ref[...])
    usb_ref[...] = _dot(h, wsb_ref[...]).astype(bf16)
    grow_ref[...] = uml[:, 2 * D_ML:].T[:2 * N_ML, :]


def _inproj(x, mod, w_ml, w_dil, w_sb, tm):
    b, s, d = x.shape
    full = lambda a: pl.BlockSpec(a.shape, lambda i, j: (0,) * a.ndim)
    return pl.pallas_call(
        _inproj_kernel,
        out_shape=(jax.ShapeDtypeStruct((b, s, ML_COLS), f32),
                   jax.ShapeDtypeStruct((b, s, 3 * D_DIL), f32),
                   jax.ShapeDtypeStruct((b, s, 3 * D_SB), bf16),
                   jax.ShapeDtypeStruct((b, 2 * N_ML, s), f32)),
        grid=(b, s // tm),
        in_specs=[pl.BlockSpec((None, tm, d), lambda i, j: (i, j, 0)),
                  pl.BlockSpec((None, 6, d), lambda i, j: (i, 0, 0)),
                  full(w_ml), full(w_dil), full(w_sb)],
        out_specs=(pl.BlockSpec((None, tm, ML_COLS), lambda i, j: (i, j, 0)),
                   pl.BlockSpec((None, tm, 3 * D_DIL), lambda i, j: (i, j, 0)),
                   pl.BlockSpec((None, tm, 3 * D_SB), lambda i, j: (i, j, 0)),
                   pl.BlockSpec((None, 2 * N_ML, tm), lambda i, j: (i, 0, j))),
        compiler_params=_cparams(("parallel", "parallel")),
        name="norm_inproj",
    )(x, mod, w_ml, w_dil, w_sb)


def _mlstm_kernel(u_ref, grow_ref, convw_ref, convb_ref, wq_ref, wk_ref, wv_ref, gbc_ref, gbr_ref,
                  y_ref, prev_ref, c_ref, n_ref, m_ref):
    L = CHUNK

    @pl.when(pl.program_id(1) == 0)
    def _():
        prev_ref[...] = jnp.zeros_like(prev_ref)
        c_ref[...] = jnp.zeros_like(c_ref)
        n_ref[...] = jnp.zeros_like(n_ref)
        m_ref[...] = jnp.zeros_like(m_ref)

    row = _iota((L, D_ML), 0)
    convw = convw_ref[...]
    ltri = (_iota((L, L), 1) <= _iota((L, L), 0)).astype(bf16)
    utri = (_iota((L, L), 0) <= _iota((L, L), 1)).astype(bf16)
    causal = _iota((L, L), 1) <= _iota((L, L), 0)
    lane = _iota((1, D_ML), 1)
    heads = range(N_ML)
    mhs = [(lane >= hd * HEAD_DIM) & (lane < (hd + 1) * HEAD_DIM) for hd in heads]
    same_head = (_iota((D_ML, D_ML), 0) // HEAD_DIM) == (_iota((D_ML, D_ML), 1) // HEAD_DIM)

    def logsig(t):
        return jnp.minimum(t, 0.0) - jnp.log1p(jnp.exp(-jnp.abs(t)))

    def chain(bb):
        u = u_ref[bb]
        xm = u[:, :D_ML]
        og = u[:, D_ML:2 * D_ML]
        gcol = u[:, 2 * D_ML:] + gbc_ref[...]
        grow = grow_ref[bb] + gbr_ref[...]
        prev = prev_ref[bb]
        xc = xm * convw[CONV_WIDTH - 1:CONV_WIDTH, :]
        for sft in range(1, CONV_WIDTH):
            shifted = jnp.where(row < sft, pltpu.roll(prev, sft, 0), pltpu.roll(xm, sft, 0))
            xc = xc + shifted * convw[CONV_WIDTH - 1 - sft:CONV_WIDTH - sft, :]
        prev_ref[bb] = xm
        xc = xc + convb_ref[...]
        xc = xc * lax.logistic(xc)
        xcb = xc.astype(bf16)
        wide = lambda col: jnp.broadcast_to(col, (L, LANES))
        twice = lambda a: jnp.concatenate([a, a], axis=1)
        lf_wide = [wide(logsig(gcol[:, N_ML + hd:N_ML + hd + 1])) for hd in heads]
        ig_wide = [wide(gcol[:, hd:hd + 1]) for hd in heads]
        lf_split = [_split_bf16(x) for x in lf_wide]
        rh, rl = _split_bf16(logsig(grow))
        yield
        q = _dot(xcb, wq_ref[...])
        k = _dot(xcb, wk_ref[...])
        v = _dot(xm.astype(bf16), wv_ref[...])
        cum_wide = [_dot(ltri, hi) + _dot(ltri, lo) for hi, lo in lf_split]
        cum_row = _dot(rh, utri) + _dot(rl, utri)
        kb = k.astype(bf16)
        vb = v.astype(bf16)
        c_bd = c_ref[bb]
        n_st = n_ref[bb]
        m_st = m_ref[bb]
        c_bdb = c_bd.astype(bf16)
        qhs = [jnp.where(mhs[hd], q, 0.0) for hd in heads]
        qhb = [qh.astype(bf16) for qh in qhs]
        qn_split = [_split_bf16(qh * n_st) for qh in qhs]
        ones_ml = jnp.ones((D_ML, LANES), bf16)
        ones_l = jnp.ones((L, LANES), bf16)
        yield
        qk = [_dot_nt(qhb[hd], kb) for hd in heads]
        qc = [_dot(qhb[hd], c_bdb) for hd in heads]
        qn = [_dot(hi, ones_ml) + _dot(lo, ones_ml) for hi, lo in qn_split]
        yield
        m_ts, inters, sws = [], [], []
        wfull = jnp.zeros((L, D_ML), f32)
        decay_l = jnp.zeros((1, D_ML), f32)
        mnew_l = jnp.zeros((1, D_ML), f32)
        for hd in heads:
            cumc = cum_wide[hd]
            cumr = cum_row[N_ML + hd:N_ML + hd + 1, :]
            igr = grow[hd:hd + 1, :]
            m_prev = m_st[:, hd * HEAD_DIM:hd * HEAD_DIM + 1]
            log_d = jnp.where(causal, cumc - cumr + igr, -jnp.inf)
            log_inter = cumc + m_prev
            m_t = jnp.maximum(log_inter, jnp.max(log_d, axis=-1, keepdims=True))
            m_ts.append(m_t)
            inters.append(jnp.exp(log_inter - m_t))
            sws.append(qk[hd] * jnp.exp(log_d - m_t))
            last = cumc[L - 1:L, :]
            log_w = last - cumc + ig_wide[hd]
            m_new = jnp.maximum(last + m_prev, jnp.max(log_w, axis=0, keepdims=True))
            wfull = jnp.where(mhs[hd], twice(jnp.exp(log_w - m_new)), wfull)
            decay_l = jnp.where(mhs[hd], twice(jnp.exp(last + m_prev - m_new)), decay_l)
            mnew_l = jnp.where(mhs[hd], twice(m_new), mnew_l)
        kw = k * wfull
        sw_split = [_split_bf16(sw) for sw in sws]
        kw16 = kw.astype(bf16)
        yield
        swv = [_dot(sw_split[hd][0], vb) for hd in heads]
        swsum = [_dot(hi, ones_l) + _dot(lo, ones_l) for hi, lo in sw_split]
        upd = lax.dot_general(kw16, vb, _TN, preferred_element_type=f32)
        yield
        hmix = jnp.zeros((L, D_ML), f32)
        for hd in heads:
            num = twice(inters[hd]) * qc[hd] + swv[hd]
            den = inters[hd] * qn[hd] + swsum[hd]
            hout = num / twice(jnp.maximum(jnp.abs(den), jnp.exp(-m_ts[hd])))
            hmix = jnp.where(mhs[hd], hout, hmix)
        c_ref[bb] = decay_l * c_bd + jnp.where(same_head, upd, 0.0)
        n_ref[bb] = decay_l * n_st + jnp.sum(kw, axis=0, keepdims=True)
        m_ref[bb] = mnew_l
        y_ref[bb] = lax.logistic(og) * hmix
        yield

    chains = [chain(bb) for bb in range(u_ref.shape[0])]
    for _ in range(MLSTM_STAGES):
        for ch_ in chains:
            next(ch_)


def _mlstm(u_ml, g_row, conv_w, conv_b, wq_bd, wk_bd, wv_bd, gb_col, gb_row):
    b, s, _ = u_ml.shape
    nb = MLSTM_ROWS if b % MLSTM_ROWS == 0 else 1
    full = lambda a: pl.BlockSpec(a.shape, lambda i, j: (0,) * a.ndim)
    return pl.pallas_call(
        _mlstm_kernel,
        out_shape=jax.ShapeDtypeStruct((b, s, D_ML), f32),
        grid=(b // nb, s // CHUNK),
        in_specs=[pl.BlockSpec((nb, CHUNK, ML_COLS), lambda i, j: (i, j, 0)),
                  pl.BlockSpec((nb, 2 * N_ML, CHUNK), lambda i, j: (i, 0, j)),
                  full(conv_w), full(conv_b), full(wq_bd), full(wk_bd), full(wv_bd), full(gb_col), full(gb_row)],
        out_specs=pl.BlockSpec((nb, CHUNK, D_ML), lambda i, j: (i, j, 0)),
        scratch_shapes=[pltpu.VMEM((nb, CHUNK, D_ML), f32), pltpu.VMEM((nb, D_ML, D_ML), f32),
                        pltpu.VMEM((nb, 1, D_ML), f32), pltpu.VMEM((nb, 1, D_ML), f32)],
        compiler_params=_cparams(("parallel", "arbitrary")),
        name="mlstm",
    )(u_ml, g_row, conv_w, conv_b, wq_bd, wk_bd, wv_bd, gb_col, gb_row)


def _rows(start, size, stride):
    return pl.ds(start, size) if stride == 1 else pl.ds(start, size, stride=stride)


def _dil_kernel(q_ref, k_ref, v_ref, slope_ref, o_ref, *scratch, seq, tile):
    npat = len(DILATIONS)
    kd = scratch[0:npat]
    vd = scratch[npat:2 * npat]
    op = scratch[2 * npat:3 * npat]
    ls = scratch[3 * npat:4 * npat]
    bias_ref = scratch[4 * npat]
    L = CHUNK
    ti = pl.program_id(2)
    lane = _iota((1, LANES), 1)
    m0 = lane < HEAD_DIM

    @pl.when(ti == 0)
    def _():
        row = _iota((L, L), 0)
        col = _iota((L, L), 1)
        for p, d in enumerate(DILATIONS):
            per = seq // d
            for r in range(d):
                kd[p][r * per:(r + 1) * per, :] = k_ref[_rows(r, per, d), :].astype(bf16)
                vd[p][r * per:(r + 1) * per, :] = v_ref[_rows(r, per, d), :].astype(bf16)
            for j in range(2):
                sl = slope_ref[j][0:1, :] * float(d)
                bias_ref[p, j, 0] = jnp.where(col >= row, -(sl * (row - col + DIL_SPAN).astype(f32)), -jnp.inf)
                bias_ref[p, j, 1] = jnp.full((L, L), -jnp.inf, f32)
                bias_ref[p, j, 2] = jnp.where(col <= row, -(sl * (row - col).astype(f32)), -jnp.inf)

    nunits = tile // L
    for p, d in enumerate(DILATIONS):
        per = seq // d
        blocks_per_res = tile // d // L

        def body(it, carry, p=p, d=d, per=per, blocks_per_res=blocks_per_res):
            units = []
            for c in range(DIL_UNITS):
                u = it * DIL_UNITS + c
                r = u // blocks_per_res
                blk = u % blocks_per_res
                start = blk * (L * d) + r
                q = q_ref[_rows(start, L, d), :].astype(bf16)
                m_first = (ti * tile) // d + blk * L
                base = r * per + m_first
                has_prev = m_first > 0
                cur0 = pl.multiple_of(base, L)
                prev0 = pl.multiple_of(jnp.where(has_prev, base - L, base), L)
                units.append(dict(start=start, q=q, pk=jnp.where(has_prev, 0, 1),
                                  kc=kd[p][pl.ds(cur0, L), :], vc=vd[p][pl.ds(cur0, L), :],
                                  kp=kd[p][pl.ds(prev0, L), :], vp=vd[p][pl.ds(prev0, L), :]))
            chains = [(un, j) for un in units for j in range(2)]
            scores = []
            for un, j in chains:
                qj = jnp.where(m0 if j == 0 else jnp.logical_not(m0), un["q"], jnp.zeros_like(un["q"]))
                scores.append((_dot_nt(qj, un["kp"]) + bias_ref[p, j, un["pk"]],
                               _dot_nt(qj, un["kc"]) + bias_ref[p, j, 2]))
            probs = []
            for sp, sc in scores:
                mx = jnp.max(jnp.maximum(sp, sc), axis=-1, keepdims=True)
                pp = jnp.exp(sp - mx)
                pc = jnp.exp(sc - mx)
                den = jnp.sum(pp + pc, axis=-1, keepdims=True)
                probs.append((pp.astype(bf16), pc.astype(bf16), den, mx + jnp.log(den)))
            outs = [(_dot(pp, un["vp"]) + _dot(pc, un["vc"])) / den
                    for (un, j), (pp, pc, den, _) in zip(chains, probs)]
            for c, un in enumerate(units):
                op[p][_rows(un["start"], L, d), :] = jnp.where(m0, outs[2 * c], outs[2 * c + 1])
                ls[p][_rows(un["start"], L, d), :] = jnp.where(m0, probs[2 * c][3], probs[2 * c + 1][3])
            return carry

        lax.fori_loop(0, nunits // DIL_UNITS, body, 0)

    l0, l1, l2 = ls[0][...], ls[1][...], ls[2][...]
    mx = jnp.maximum(jnp.maximum(l0, l1), l2)
    e0, e1, e2 = jnp.exp(l0 - mx), jnp.exp(l1 - mx), jnp.exp(l2 - mx)
    o_ref[...] = ((e0 * op[0][...] + e1 * op[1][...] + e2 * op[2][...]) / (e0 + e1 + e2)).astype(o_ref.dtype)


def _dilated(u_dil, slopes):
    b, s, _ = u_dil.shape
    npair = N_DIL // 2
    tile = DILATIONS[-1] * CHUNK
    assert s % tile == 0 and (tile // CHUNK) % DIL_UNITS == 0
    seq_buf = lambda dt: pltpu.VMEM((s, LANES), dt)
    tile_buf = pltpu.VMEM((tile, LANES), f32)
    npat = len(DILATIONS)
    return pl.pallas_call(
        functools.partial(_dil_kernel, seq=s, tile=tile),
        out_shape=jax.ShapeDtypeStruct((b, s, D_DIL), bf16),
        grid=(b, npair, s // tile),
        in_specs=[pl.BlockSpec((None, tile, LANES), lambda i, h, n: (i, n, h)),
                  pl.BlockSpec((None, s, LANES), lambda i, h, n: (i, 0, npair + h)),
                  pl.BlockSpec((None, s, LANES), lambda i, h, n: (i, 0, 2 * npair + h)),
                  pl.BlockSpec((None, 2, 8, LANES), lambda i, h, n: (h, 0, 0, 0))],
        out_specs=pl.BlockSpec((None, tile, LANES), lambda i, h, n: (i, n, h)),
        scratch_shapes=([seq_buf(bf16)] * (2 * npat) + [tile_buf] * (2 * npat)
                        + [pltpu.VMEM((npat, 2, 3, CHUNK, CHUNK), f32)]),
        compiler_params=_cparams(("parallel", "parallel", "arbitrary")),
        name="dilated",
    )(u_dil, u_dil, u_dil, slopes)


def _sb_kernel(q_ref, k_ref, v_ref, o_ref, qm_ref, r_ref, acc_ref, *, nq):
    L = CHUNK
    tile0 = pl.program_id(1) * nq
    row = _iota((L, L), 0)
    col = _iota((L, L), 1)
    strict = col < row
    lane = _iota((1, LANES), 1)
    m0 = lane < HEAD_DIM
    npair = N_SB // 2
    csum = jnp.concatenate([(_iota((L, L), 0) > _iota((L, L), 1)).astype(bf16), jnp.ones((L, L), bf16)], axis=1)

    for qb in range(nq):
        for hp in range(npair):
            q = q_ref[qb * L:(qb + 1) * L, hp * LANES:(hp + 1) * LANES]
            qm_ref[(qb * npair + hp) * 2] = jnp.where(m0, q, jnp.zeros_like(q))
            qm_ref[(qb * npair + hp) * 2 + 1] = jnp.where(m0, jnp.zeros_like(q), q)

    def sweep(it, diagonal):
        rmax = None
        for g0 in range(0, nq, SB_GROUP):
            chains = []
            for qb in range(g0, g0 + SB_GROUP):
                qi = tile0 + qb
                k0 = pl.multiple_of(jnp.maximum(qi - it, 0) * L, L)
                out_of_keys = jnp.where(qi - it <= 0, SB_NO_KEYS, 0.0)
                for hp in range(npair):
                    kblk = k_ref[pl.ds(k0, L), hp * LANES:(hp + 1) * LANES]
                    vblk = v_ref[pl.ds(k0, L), hp * LANES:(hp + 1) * LANES]
                    for j in range(2):
                        chains.append(((qb * npair + hp) * 2 + j, kblk, vblk, out_of_keys))
            zs = [_dot_nt(qm_ref[c], kblk) for c, kblk, _, _ in chains]
            zl, lkb = [], []
            for z in zs:
                ls = jnp.minimum(z, 0.0) - jnp.log(1.0 + jnp.exp(-jnp.abs(z)))
                lk = ls - z
                zl.append(ls)
                lkb.append((jnp.where(strict, lk, 0.0) if diagonal else lk).astype(bf16))
            ts = [_dot(x, csum) for x in lkb]
            avs = []
            for (c, _, _, out_of_keys), s, t in zip(chains, zl, ts):
                if diagonal:
                    a = jnp.where(strict, jnp.exp(s + t[:, :L]), 0.0)
                    rs = t[:, L:] + out_of_keys
                else:
                    a = jnp.exp(s + r_ref[c] + t[:, :L])
                    rs = r_ref[c] + t[:, L:] + out_of_keys
                avs.append(a.astype(bf16))
                r_ref[c] = rs
                rmax = rs if rmax is None else jnp.maximum(rmax, rs)
            for (c, _, vblk, _), a in zip(chains, avs):
                if diagonal:
                    acc_ref[c] = _dot(a, vblk)
                else:
                    acc_ref[c] += _dot(a, vblk)
        return jnp.max(rmax) > SB_DEAD_LOG

    alive0 = sweep(0, True)

    def body(carry):
        _, it = carry
        return sweep(it, False), it + 1

    lax.while_loop(lambda carry: carry[0], body, (alive0, jnp.int32(1)))

    for qb in range(nq):
        for hp in range(npair):
            c = (qb * npair + hp) * 2
            o_ref[qb * L:(qb + 1) * L, hp * LANES:(hp + 1) * LANES] = (
                jnp.where(m0, acc_ref[c], acc_ref[c + 1]).astype(o_ref.dtype))


def _stickbreak(u_sb):
    b, s, _ = u_sb.shape
    nq = SB_QBLOCKS
    tq = nq * CHUNK
    nchain = nq * N_SB
    return pl.pallas_call(
        functools.partial(_sb_kernel, nq=nq),
        out_shape=jax.ShapeDtypeStruct((b, s, D_SB), bf16),
        grid=(b, s // tq),
        in_specs=[pl.BlockSpec((None, tq, D_SB), lambda i, n: (i, n, 0)),
                  pl.BlockSpec((None, s, D_SB), lambda i, n: (i, 0, 1)),
                  pl.BlockSpec((None, s, D_SB), lambda i, n: (i, 0, 2))],
        out_specs=pl.BlockSpec((None, tq, D_SB), lambda i, n: (i, n, 0)),
        scratch_shapes=[pltpu.VMEM((nchain, CHUNK, LANES), bf16),
                        pltpu.VMEM((nchain, CHUNK, CHUNK), f32),
                        pltpu.VMEM((nchain, CHUNK, LANES), f32)],
        compiler_params=_cparams(("parallel", "arbitrary")),
        name="stickbreak",
    )(u_sb, u_sb, u_sb)


def _post_kernel(x_ref, yml_ref, ydil_ref, ysb_ref, mod_ref, gh_ref, wout_ref, wr_hi_ref, wr_lo_ref, rb_ref,
                 xo_ref, h_ref, selt_ref, w_ref):
    mod = mod_ref[...]
    y = jnp.concatenate([yml_ref[...], ydil_ref[...].astype(f32), ysb_ref[...].astype(f32)], axis=1)
    pair = ((_iota((LANES, LANES), 0) // HEAD_DIM) == (_iota((LANES, LANES), 1) // HEAD_DIM)).astype(bf16)
    cols = []
    for cblk in range(y.shape[1] // LANES):
        yb = y[:, cblk * LANES:(cblk + 1) * LANES]
        hi, lo = _split_bf16(yb * yb)
        ms = (_dot(hi, pair) + _dot(lo, pair)) * (1.0 / HEAD_DIM)
        cols.append(yb * lax.rsqrt(ms + EPS))
    yn = (jnp.concatenate(cols, axis=1) * gh_ref[...]).astype(bf16)
    xn = x_ref[...] + mod[2:3, :] * _dot(yn, wout_ref[...])
    xo_ref[...] = xn
    h2 = _rms(xn) * (1.0 + mod[4:5, :]) + mod[3:4, :]
    hh, hl = _split_bf16(h2)
    h_ref[...] = h2
    tm = x_ref.shape[0]
    row = _iota((SUBLANES, tm), 0)
    grow = row < N_GROUPS
    logits = _dot(hh, wr_hi_ref[...]) + _dot(hl, wr_hi_ref[...]) + _dot(hh, wr_lo_ref[...])
    logits_t = logits.T
    sc, bs = [], []
    for j in range(EPG):
        sc.append(lax.logistic(logits_t[j * SUBLANES:(j + 1) * SUBLANES, :]))
        bs.append(jnp.where(grow, sc[j] + rb_ref[j][:, 0:1], -jnp.inf))
    hi01, lo01 = jnp.maximum(bs[0], bs[1]), jnp.minimum(bs[0], bs[1])
    hi23, lo23 = jnp.maximum(bs[2], bs[3]), jnp.minimum(bs[2], bs[3])
    top1 = jnp.maximum(hi01, hi23)
    top2 = jnp.maximum(jnp.minimum(hi01, hi23), jnp.maximum(lo01, lo23))
    gscore = jnp.where(grow, top1 + top2, -jnp.inf)
    gmax = jnp.max(gscore, axis=0, keepdims=True)
    gidx = jnp.min(jnp.where(gscore == gmax, row, SUBLANES), axis=0, keepdims=True)
    gsel = row == gidx
    picks, vals = [], []
    for j in range(EPG):
        rank = jnp.zeros(bs[j].shape, jnp.int32)
        for jj in range(EPG):
            if jj != j:
                ahead = (bs[jj] >= bs[j]) if jj < j else (bs[jj] > bs[j])
                rank = rank + ahead.astype(jnp.int32)
        picks.append(gsel & (rank < 2))
        vals.append(jnp.sum(jnp.where(gsel, sc[j], 0.0), axis=0, keepdims=True))
    erow = _iota((N_EXPERTS, SUBLANES), 0)
    gcol = _iota((N_EXPERTS, SUBLANES), 1)
    sel_t = jnp.zeros((N_EXPERTS, tm), f32)
    for j in range(EPG):
        place = ((erow == gcol * EPG + j) & (gcol < N_GROUPS)).astype(bf16)
        sel_t = sel_t + _dot(place, picks[j].astype(bf16))
    selt_ref[...] = sel_t.astype(selt_ref.dtype)
    seen = jnp.zeros_like(vals[0])
    w_lo = jnp.zeros_like(vals[0])
    w_hi = jnp.zeros_like(vals[0])
    for j in range(EPG):
        picked = jnp.sum(picks[j].astype(f32), axis=0, keepdims=True) > 0.0
        w_lo = w_lo + jnp.where(picked & (seen == 0.0), vals[j], 0.0)
        w_hi = w_hi + jnp.where(picked & (seen == 1.0), vals[j], 0.0)
        seen = seen + picked.astype(f32)
    tot = w_lo + w_hi
    w_ref[...] = jnp.where(row == 0, w_lo / tot, jnp.where(row == 1, w_hi / tot, 0.0))


def _post(x, y_ml, y_dil, y_sb, mod, g_head, w_out, wr_hi, wr_lo, rb, tm):
    b, s, d = x.shape
    tok = lambda w: pl.BlockSpec((None, tm, w), lambda i, j: (i, j, 0))
    full = lambda a: pl.BlockSpec(a.shape, lambda i, j: (0,) * a.ndim)
    tok_lanes = lambda rows: pl.BlockSpec((rows, tm), lambda i, j: (0, i * (s // tm) + j))
    return pl.pallas_call(
        _post_kernel,
        out_shape=(jax.ShapeDtypeStruct((b, s, d), f32),
                   jax.ShapeDtypeStruct((b, s, d), f32),
                   jax.ShapeDtypeStruct((N_EXPERTS, b * s), bf16),
                   jax.ShapeDtypeStruct((SUBLANES, b * s), f32)),
        grid=(b, s // tm),
        in_specs=[tok(d), tok(D_ML), tok(D_DIL), tok(D_SB), pl.BlockSpec((None, 6, d), lambda i, j: (i, 0, 0)),
                  full(g_head), full(w_out), full(wr_hi), full(wr_lo), full(rb)],
        out_specs=(tok(d), tok(d), tok_lanes(N_EXPERTS), tok_lanes(SUBLANES)),
        compiler_params=_cparams(("parallel", "parallel")),
        name="merge_outproj_router",
    )(x, y_ml, y_dil, y_sb, mod, g_head, w_out, wr_hi, wr_lo, rb)


def _plan_kernel(selt_ref, lpos0_ref, lpos1_ref, base_ref, nchunk_ref, lofs_ref, tiles_ref, tails_ref,
                 before_ref, cnt_ref, lofs_sc, *, ntw, ntile_pad, nrow_tiles):
    phase = pl.program_id(0)
    step = pl.program_id(1)
    tk = MOE_TOK
    sel = selt_ref[...]
    tlane = _iota((1, ntw), 1)

    @pl.when((phase == 0) & (step == 0))
    def _():
        cnt_ref[...] = jnp.zeros_like(cnt_ref)
        before_ref[...] = (_iota((tk, tk), 0) < _iota((tk, tk), 1)).astype(bf16)

    @pl.when(phase == 0)
    def _():
        cnt = cnt_ref[...]
        for j in range(PLAN_GROUP):
            here = _dot(sel[:, j * tk:(j + 1) * tk], jnp.ones((tk, ntw), bf16))
            cnt = jnp.where(tlane == step * PLAN_GROUP + j, here, cnt)
        cnt_ref[...] = cnt
        lpos0_ref[...] = jnp.zeros_like(lpos0_ref)
        lpos1_ref[...] = jnp.zeros_like(lpos1_ref)

    @pl.when((phase == 1) & (step == 0))
    def _():
        cnt = cnt_ref[...]
        nch = jnp.floor((cnt + (MOE_CHUNK - 1.0)) * (1.0 / MOE_CHUNK))
        nchb = nch.astype(bf16)
        earlier_tiles = (_iota((ntw, ntw), 0) < _iota((ntw, ntw), 1)).astype(bf16)
        earlier_experts = (_iota((N_EXPERTS, N_EXPERTS), 1) < _iota((N_EXPERTS, N_EXPERTS), 0)).astype(bf16)
        in_expert = _dot(nchb, earlier_tiles) * MOE_CHUNK
        lofs = _dot(earlier_experts, nchb) * MOE_CHUNK
        total = _dot(nchb, jnp.ones((ntw, LANES), bf16)) * MOE_CHUNK
        padded = jnp.floor((total + (MOE_ROWS - 1.0)) * (1.0 / MOE_ROWS)) * MOE_ROWS
        rows, acc = [], jnp.zeros((1, LANES), f32)
        for e in range(N_EXPERTS):
            rows.append(acc)
            acc = acc + padded[e:e + 1, :]
        off = jnp.concatenate(rows, axis=0)
        base_ref[...] = (off[:, 0:1] + in_expert).astype(jnp.int32)
        nchunk_ref[...] = nch.astype(jnp.int32)
        lofs_ref[...] = lofs.astype(jnp.int32)
        lofs_sc[...] = lofs
        lane = _iota((1, LANES), 1)
        tails_ref[...] = jnp.where(
            lane == 0, off + total,
            jnp.where(lane == 1, (padded - total) * (1.0 / MOE_CHUNK),
                      jnp.where(lane == 2, acc, jnp.where(lane == 3, nrow_tiles - acc * (1.0 / MOE_ROWS), 0.0)))
        ).astype(jnp.int32)
        start = _iota((1, ntile_pad), 1).astype(f32) * MOE_ROWS
        ends = (off + padded)[:, 0:1]
        owner = jnp.sum((start >= ends).astype(f32), axis=0, keepdims=True)
        tiles_ref[...] = jnp.zeros_like(tiles_ref)
        tiles_ref[0:1, :] = jnp.minimum(owner, N_EXPERTS - 1.0).astype(jnp.int32)
        tiles_ref[1:2, :] = (start < acc[:, 0:1]).astype(jnp.int32)

    @pl.when(phase == 1)
    def _():
        for j in range(PLAN_GROUP):
            cols = slice(j * tk, (j + 1) * tk)
            mine = jnp.sum(jnp.where(tlane == step * PLAN_GROUP + j, lofs_sc[...], 0.0), axis=1, keepdims=True)
            pos = _dot(sel[:, cols], before_ref[...]) + mine
            picked = sel[:, cols] > 0
            lpos0_ref[:, cols] = jnp.min(jnp.where(picked, pos, 1e9), axis=0, keepdims=True).astype(jnp.int32)
            lpos1_ref[:, cols] = jnp.max(jnp.where(picked, pos, -1.0), axis=0, keepdims=True).astype(jnp.int32)


def _plan(sel_t, nrow_tiles):
    ne, t = sel_t.shape
    nt = t // MOE_TOK
    ntw = -(-nt // LANES) * LANES
    ntile_pad = -(-nrow_tiles // LANES) * LANES
    i32 = jnp.int32
    whole = lambda shape: pl.BlockSpec(shape, lambda p, i: (0, 0))
    per_tile = pl.BlockSpec((1, PLAN_GROUP * MOE_TOK), lambda p, i: (0, i * p))
    lpos0, lpos1, base, nchunk, lofs, tiles, tails = pl.pallas_call(
        functools.partial(_plan_kernel, ntw=ntw, ntile_pad=ntile_pad, nrow_tiles=nrow_tiles),
        out_shape=(jax.ShapeDtypeStruct((1, t), i32), jax.ShapeDtypeStruct((1, t), i32),
                   jax.ShapeDtypeStruct((ne, ntw), i32), jax.ShapeDtypeStruct((ne, ntw), i32),
                   jax.ShapeDtypeStruct((ne, ntw), i32), jax.ShapeDtypeStruct((8, ntile_pad), i32),
                   jax.ShapeDtypeStruct((ne, LANES), i32)),
        grid=(2, nt // PLAN_GROUP),
        in_specs=[pl.BlockSpec((ne, PLAN_GROUP * MOE_TOK), lambda p, i: (0, i))],
        out_specs=(per_tile, per_tile, whole((ne, ntw)), whole((ne, ntw)), whole((ne, ntw)),
                   whole((8, ntile_pad)), whole((ne, LANES))),
        scratch_shapes=[pltpu.VMEM((MOE_TOK, MOE_TOK), bf16), pltpu.VMEM((ne, ntw), f32),
                        pltpu.VMEM((ne, ntw), f32)],
        compiler_params=_cparams(("arbitrary", "arbitrary")),
        name="moe_plan",
    )(sel_t)
    flat = lambda a: a[:, :nt].T.reshape(nt * ne)
    return dict(lpos0=lpos0, lpos1=lpos1, base=flat(base), nchunk=flat(nchunk), lofs=flat(lofs),
                owner=tiles[0, :nrow_tiles], used=tiles[1, :nrow_tiles],
                tail_start=tails[:, 0], tail_chunks=tails[:, 1], unused=tails[0, 2:4])


def _segment_chunks(tile, base_ref, nchunk_ref, lofs_ref, make_copy, start):
    for e in range(N_EXPERTS):
        idx = tile * N_EXPERTS + e
        first_global = base_ref[idx]
        first_local = lofs_ref[idx]

        def one(c, carry, e=e, first_global=first_global, first_local=first_local):
            cp = make_copy(pl.multiple_of(first_local + c * MOE_CHUNK, MOE_CHUNK),
                           pl.multiple_of(first_global + c * MOE_CHUNK, MOE_CHUNK), e % 2)
            cp.start(priority=e % 2) if start else cp.wait()
            return carry

        lax.fori_loop(0, nchunk_ref[idx], one, 0)


def _one_hot_rows(lpos_ref):
    return (_iota((MOE_LOCAL, MOE_TOK), 0) == lpos_ref[...]).astype(bf16)


def _dispatch_kernel(base_ref, nchunk_ref, lofs_ref, tstart_ref, tchunks_ref, unused_ref, lpos0_ref, lpos1_ref,
                     h_ref, xs_hbm, loc, zeros, sem):
    tile = pl.program_id(0)
    place = _one_hot_rows(lpos0_ref) + _one_hot_rows(lpos1_ref)
    loc[...] = _dot(place, h_ref[...].astype(bf16))

    def seg_copy(local_row, global_row, s):
        return pltpu.make_async_copy(loc.at[pl.ds(local_row, MOE_CHUNK)], xs_hbm.at[pl.ds(global_row, MOE_CHUNK)],
                                     sem.at[s])

    _segment_chunks(tile, base_ref, nchunk_ref, lofs_ref, seg_copy, True)
    _segment_chunks(tile, base_ref, nchunk_ref, lofs_ref, seg_copy, False)

    @pl.when(tile == pl.num_programs(0) - 1)
    def _():
        zeros[...] = jnp.zeros_like(zeros)
        for start in (True, False):
            for e in range(N_EXPERTS):
                def one(c, carry, e=e, start=start):
                    row = pl.multiple_of(tstart_ref[e] + c * MOE_CHUNK, MOE_CHUNK)
                    cp = pltpu.make_async_copy(zeros.at[pl.ds(0, MOE_CHUNK)], xs_hbm.at[pl.ds(row, MOE_CHUNK)],
                                               sem.at[e % 2])
                    cp.start() if start else cp.wait()
                    return carry
                lax.fori_loop(0, tchunks_ref[e], one, 0)

            def whole_tile(c, carry, start=start):
                row = pl.multiple_of(unused_ref[0] + c * MOE_ROWS, MOE_ROWS)
                cp = pltpu.make_async_copy(zeros, xs_hbm.at[pl.ds(row, MOE_ROWS)], sem.at[0])
                cp.start() if start else cp.wait()
                return carry
            lax.fori_loop(0, unused_ref[1], whole_tile, 0)


def _dispatch(h_rows, plan, nrows):
    t, w = h_rows.shape
    per_tile = lambda: pl.BlockSpec((1, MOE_TOK), lambda i, *_: (0, i))
    return pl.pallas_call(
        _dispatch_kernel,
        out_shape=jax.ShapeDtypeStruct((nrows, w), h_rows.dtype),
        grid_spec=pltpu.PrefetchScalarGridSpec(
            num_scalar_prefetch=6,
            grid=(t // MOE_TOK,),
            in_specs=[per_tile(), per_tile(), pl.BlockSpec((MOE_TOK, w), lambda i, *_: (i, 0))],
            out_specs=pl.BlockSpec(memory_space=pl.ANY),
            scratch_shapes=[pltpu.VMEM((MOE_LOCAL, w), f32), pltpu.VMEM((MOE_ROWS, w), f32),
                            pltpu.SemaphoreType.DMA((2,))]),
        compiler_params=_cparams(("arbitrary",)),
        name="moe_dispatch",
    )(plan["base"], plan["nchunk"], plan["lofs"], plan["tail_start"], plan["tail_chunks"], plan["unused"],
      plan["lpos0"], plan["lpos1"], h_rows)


def _experts_kernel(owner_ref, used_ref, xs_ref, wg_ref, wu_ref, wd_ref, ys_ref):
    tile = pl.program_id(0)

    @pl.when(used_ref[tile] > 0)
    def _():
        xb = xs_ref[...].astype(bf16)
        g = _dot(xb, wg_ref[...].astype(bf16))
        he = g * lax.logistic(g) * _dot(xb, wu_ref[...].astype(bf16))
        ys_ref[...] = _dot(he.astype(bf16), wd_ref[...].astype(bf16))

    @pl.when(used_ref[tile] == 0)
    def _():
        ys_ref[...] = jnp.zeros_like(ys_ref)


def _experts(xs, owner, used, wg, wu, wd, layer):
    nrows, w = xs.shape
    _, _, d, de = wg.shape
    return pl.pallas_call(
        _experts_kernel,
        out_shape=jax.ShapeDtypeStruct((nrows, w), xs.dtype),
        grid_spec=pltpu.PrefetchScalarGridSpec(
            num_scalar_prefetch=2,
            grid=(nrows // MOE_ROWS,),
            in_specs=[pl.BlockSpec((MOE_ROWS, w), lambda g, own, use: (g * use[g], 0)),
                      pl.BlockSpec((None, None, d, de), lambda g, own, use: (layer, own[g], 0, 0)),
                      pl.BlockSpec((None, None, d, de), lambda g, own, use: (layer, own[g], 0, 0)),
                      pl.BlockSpec((None, None, de, d), lambda g, own, use: (layer, own[g], 0, 0))],
            out_specs=pl.BlockSpec((MOE_ROWS, w), lambda g, own, use: (g, 0))),
        compiler_params=_cparams(("arbitrary",)),
        name="moe_experts",
    )(owner, used, xs, wg, wu, wd)


def _combine_kernel(base_ref, nchunk_ref, lofs_ref, lpos0_ref, lpos1_ref, ys_hbm, x_ref, w_ref, mod_ref, gf_ref,
                    o_ref, loc, sem, *, final):
    tile = pl.program_id(0)
    ntile = pl.num_programs(0)
    slot = tile % 2

    def gather(tile_, slot_, start):
        def seg_copy(local_row, global_row, s):
            return pltpu.make_async_copy(ys_hbm.at[pl.ds(global_row, MOE_CHUNK)],
                                         loc.at[slot_, pl.ds(local_row, MOE_CHUNK)], sem.at[slot_, s])
        _segment_chunks(tile_, base_ref, nchunk_ref, lofs_ref, seg_copy, start)

    @pl.when(tile == 0)
    def _():
        loc[...] = jnp.zeros_like(loc)
        gather(0, 0, True)

    @pl.when(tile + 1 < ntile)
    def _():
        gather(tile + 1, 1 - slot, True)

    gather(tile, slot, False)
    ysl = loc[slot].astype(bf16)
    prow = _iota((MOE_LOCAL, MOE_TOK), 0)
    gates = (jnp.where(prow == lpos0_ref[...], w_ref[0:1, :], 0.0)
             + jnp.where(prow == lpos1_ref[...], w_ref[1:2, :], 0.0))
    g_hi, g_lo = _split_bf16(gates)
    y = (lax.dot_general(g_hi, ysl, _TN, preferred_element_type=f32)
         + lax.dot_general(g_lo, ysl, _TN, preferred_element_type=f32))
    xn = x_ref[...] + mod_ref[...][5:6, :] * y
    if final:
        xn = _rms(xn) * gf_ref[...]
    o_ref[...] = xn


def _combine(ys, plan, x, w_pair, mod, g_final, final):
    b, s, d = x.shape
    t = b * s
    per_batch = s // MOE_TOK
    per_tile = lambda: pl.BlockSpec((1, MOE_TOK), lambda i, *_: (0, i))
    out = pl.pallas_call(
        functools.partial(_combine_kernel, final=final),
        out_shape=jax.ShapeDtypeStruct((t, d), f32),
        grid_spec=pltpu.PrefetchScalarGridSpec(
            num_scalar_prefetch=3,
            grid=(t // MOE_TOK,),
            in_specs=[per_tile(), per_tile(), pl.BlockSpec(memory_space=pl.ANY),
                      pl.BlockSpec((MOE_TOK, d), lambda i, *_: (i, 0)),
                      pl.BlockSpec((SUBLANES, MOE_TOK), lambda i, *_: (0, i)),
                      pl.BlockSpec((None, 6, d), lambda i, *_: (i // per_batch, 0, 0)),
                      pl.BlockSpec((1, d), lambda i, *_: (0, 0))],
            out_specs=pl.BlockSpec((MOE_TOK, d), lambda i, *_: (i, 0)),
            scratch_shapes=[pltpu.VMEM((2, MOE_LOCAL, d), f32), pltpu.SemaphoreType.DMA((2, 2))]),
        compiler_params=_cparams(("arbitrary",)),
        name="moe_combine_final" if final else "moe_combine",
    )(plan["base"], plan["nchunk"], plan["lofs"], plan["lpos0"], plan["lpos1"], ys,
      x.reshape(t, d), w_pair, mod, g_final)
    return out.reshape(b, s, d)


def _moe(x, h_rows, sel_t, w_pair, mod, wg, wu, wd, layer, g_final, final):
    b, s, d = x.shape
    t = b * s
    worst = 2 * t + (t // MOE_TOK) * N_EXPERTS * (MOE_CHUNK - 1) + N_EXPERTS * MOE_ROWS
    nrows = -(-worst // MOE_ROWS) * MOE_ROWS
    plan = _plan(sel_t, nrows // MOE_ROWS)
    xs = _dispatch(h_rows.reshape(t, d), plan, nrows)
    ys = _experts(xs, plan["owner"], plan["used"], wg, wu, wd, layer)
    return _combine(ys, plan, x, w_pair, mod, g_final, final)


def _block_diag(w):
    n, a, c = w.shape
    out = jnp.zeros((n * a, n * c), w.dtype)
    for i in range(n):
        out = out.at[i * a:(i + 1) * a, i * c:(i + 1) * c].set(w[i])
    return out


def kernel(x, c, w_in, conv_w, conv_b, w_mq, w_mk, w_mv, gate_bias, g_head, w_out, w_ada, b_ada,
           w_router, router_bias, w_gate_e, w_up_e, w_down_e, g_final):
    b, s, d = x.shape
    depth = w_in.shape[0]
    qscale = HEAD_DIM ** -0.5
    tm = min(512, s)

    mod_all = _ada(c, w_ada, b_ada).reshape(depth, b, 6, d)

    wr = jnp.zeros((d, LANES // SUBLANES, SUBLANES), f32).at[:, :EPG, :N_GROUPS].set(
        w_router.reshape(d, N_GROUPS, EPG).transpose(0, 2, 1)).reshape(d, LANES)
    wr_hi = wr.astype(bf16)
    wr_lo = (wr - wr_hi.astype(f32)).astype(bf16)
    rb = jnp.zeros((EPG, SUBLANES, LANES), f32).at[:, :N_GROUPS, :].set(
        jnp.broadcast_to(router_bias.reshape(N_GROUPS, EPG).T[:, :, None], (EPG, N_GROUPS, LANES)))

    slope = jnp.exp2(-ALIBI_MAX_BIAS * jnp.arange(1, N_DIL + 1, dtype=f32) / N_DIL)
    slopes = jnp.broadcast_to(slope.reshape(N_DIL // 2, 2, 1, 1), (N_DIL // 2, 2, 8, LANES))
    g_final2 = g_final.reshape(1, d)

    w_ml_all, w_dil_all, w_sb_all = _wprep(w_in, min(256, d))

    for l in range(depth):
        mod = mod_all[l]

        u_ml, u_dil, u_sb, g_row = _inproj(x, mod, w_ml_all[l], w_dil_all[l], w_sb_all[l], tm)

        gb = gate_bias[l]
        gb_col = jnp.zeros((1, LANES), f32).at[0, :2 * N_ML].set(gb.reshape(-1))
        gb_row = gb.reshape(2 * N_ML, 1)
        y_ml = _mlstm(u_ml, g_row, conv_w[l], conv_b[l].reshape(1, D_ML),
                      _block_diag(w_mq[l]).astype(bf16), (_block_diag(w_mk[l]) * qscale).astype(bf16),
                      _block_diag(w_mv[l]).astype(bf16), gb_col, gb_row)
        y_dil = _dilated(u_dil, slopes)
        y_sb = _stickbreak(u_sb)

        x, h_rows, sel_t, w_pair = _post(x, y_ml, y_dil, y_sb, mod, g_head[l].reshape(1, d),
                                           w_out[l].astype(bf16), wr_hi, wr_lo, rb, tm)
        x = _moe(x, h_rows, sel_t, w_pair, mod, w_gate_e, w_up_e, w_down_e, l, g_final2,
                 final=(l == depth - 1))
    return x
```

```python
import functools

import jax
import jax.numpy as jnp
from jax import lax
from jax.experimental import pallas as pl
from jax.experimental.pallas import tpu as pltpu

HEAD_DIM = 64
N_ML = 4
N_DIL = 6
N_SB = 6
D_ML = N_ML * HEAD_DIM
D_DIL = N_DIL * HEAD_DIM
D_SB = N_SB * HEAD_DIM
CONV_WIDTH = 4
CHUNK = 128
DILATIONS = (1, 4, 16)
DIL_SPAN = 128
ALIBI_MAX_BIAS = 8.0
N_EXPERTS = 16
N_GROUPS = 4
EPG = N_EXPERTS // N_GROUPS
EPS = 1e-6
LANES = 128
ML_COLS = 2 * D_ML + LANES
SB_DEAD_LOG = -104.0
SB_NO_KEYS = -1e30
SB_QBLOCKS = 4
SB_GROUP = 4
DIL_UNITS = 8
MOE_ROWS = 512
MOE_TOK = 256
MOE_CHUNK = 8
PLAN_GROUP = 4
SUBLANES = 8
MLSTM_ROWS = 2
MLSTM_STAGES = 6
MOE_LOCAL = 2 * MOE_TOK + N_EXPERTS * MOE_CHUNK
VMEM_LIMIT = 56 * 1024 * 1024

_NT = (((1,), (1,)), ((), ()))
_TN = (((0,), (0,)), ((), ()))

f32 = jnp.float32
bf16 = jnp.bfloat16


def _dot(a, b):
    return jnp.dot(a, b, preferred_element_type=f32)


def _dot_nt(a, b):
    return lax.dot_general(a, b, _NT, preferred_element_type=f32)


def _split_bf16(x):
    hi = x.astype(bf16)
    lo = (x - hi.astype(f32)).astype(bf16)
    return hi, lo


def _iota(shape, dim):
    return lax.broadcasted_iota(jnp.int32, shape, dim)


def _rms(x):
    return x * lax.rsqrt(jnp.mean(x * x, axis=-1, keepdims=True) + EPS)


def _cparams(sem):
    return pltpu.CompilerParams(dimension_semantics=sem, vmem_limit_bytes=VMEM_LIMIT)


def _ada_kernel(c_ref, w_ref, b_ref, o_ref):
    c = c_ref[...]
    c_act = (c * lax.logistic(c)).astype(bf16)
    o_ref[...] = _dot(c_act, w_ref[...].astype(bf16)) + b_ref[...]


def _ada(c, w_ada, b_ada):
    depth, d, nd = w_ada.shape
    b = c.shape[0]
    nblk = nd // d
    return pl.pallas_call(
        _ada_kernel,
        out_shape=jax.ShapeDtypeStruct((depth, b, nd), f32),
        grid=(depth, nblk),
        in_specs=[pl.BlockSpec((b, d), lambda l, j: (0, 0)),
                  pl.BlockSpec((None, d, d), lambda l, j: (l, 0, j)),
                  pl.BlockSpec((None, 1, d), lambda l, j: (l, 0, j))],
        out_specs=pl.BlockSpec((None, b, d), lambda l, j: (l, 0, j)),
        compiler_params=_cparams(("parallel", "parallel")),
        name="ada_mod",
    )(c, w_ada, b_ada.reshape(depth, 1, nd))


def _wprep_kernel(w_ref, wml_ref, wdil_ref, wsb_ref):
    w = w_ref[...]
    off_dil = 2 * D_ML + 2 * N_ML
    off_sb = off_dil + 3 * D_DIL
    qs = HEAD_DIM ** -0.5
    ml = w[:, :ML_COLS]
    wml_ref[...] = jnp.where(_iota((1, ML_COLS), 1) < off_dil, ml, 0.0).astype(bf16)
    q_cols = jnp.where(_iota((1, 3 * D_DIL), 1) < D_DIL, qs, 1.0)
    wdil_ref[...] = (w[:, off_dil:off_sb] * q_cols).astype(bf16)
    wsb_ref[...] = (w[:, off_sb:off_sb + 3 * D_SB] * q_cols).astype(bf16)


def _wprep(w_in, rows):
    depth, d, d_in = w_in.shape
    assert d_in == 2 * D_ML + 2 * N_ML + 3 * D_DIL + 3 * D_SB and D_DIL == D_SB
    out = lambda w: pl.BlockSpec((None, rows, w), lambda l, i: (l, i, 0))
    return pl.pallas_call(
        _wprep_kernel,
        out_shape=(jax.ShapeDtypeStruct((depth, d, ML_COLS), bf16),
                   jax.ShapeDtypeStruct((depth, d, 3 * D_DIL), bf16),
                   jax.ShapeDtypeStruct((depth, d, 3 * D_SB), bf16)),
        grid=(depth, d // rows),
        in_specs=[pl.BlockSpec((None, rows, d_in), lambda l, i: (l, i, 0))],
        out_specs=(out(ML_COLS), out(3 * D_DIL), out(3 * D_SB)),
        compiler_params=_cparams(("parallel", "parallel")),
        name="inproj_weight_layout",
    )(w_in)


def _inproj_kernel(x_ref, mod_ref, wml_ref, wdil_ref, wsb_ref,
                   uml_ref, udil_ref, usb_ref, grow_ref):
    x = x_ref[...]
    mod = mod_ref[...]
    h = (_rms(x) * (1.0 + mod[1:2, :]) + mod[0:1, :]).astype(bf16)
    uml = _dot(h, wml_ref[...])
    uml_ref[...] = uml
    udil_ref[...] = _dot(h, wdil_ref[...])
    usb_ref[...] = _dot(h, wsb_ref[...]).astype(bf16)
    grow_ref[...] = uml[:, 2 * D_ML:].T[:2 * N_ML, :]


def _inproj(x, mod, w_ml, w_dil, w_sb, tm):
    b, s, d = x.shape
    full = lambda a: pl.BlockSpec(a.shape, lambda i, j: (0,) * a.ndim)
    return pl.pallas_call(
        _inproj_kernel,
        out_shape=(jax.ShapeDtypeStruct((b, s, ML_COLS), f32),
                   jax.ShapeDtypeStruct((b, s, 3 * D_DIL), f32),
                   jax.ShapeDtypeStruct((b, s, 3 * D_SB), bf16),
                   jax.ShapeDtypeStruct((b, 2 * N_ML, s), f32)),
        grid=(b, s // tm),
        in_specs=[pl.BlockSpec((None, tm, d), lambda i, j: (i, j, 0)),
                  pl.BlockSpec((None, 6, d), lambda i, j: (i, 0, 0)),
                  full(w_ml), full(w_dil), full(w_sb)],
        out_specs=(pl.BlockSpec((None, tm, ML_COLS), lambda i, j: (i, j, 0)),
                   pl.BlockSpec((None, tm, 3 * D_DIL), lambda i, j: (i, j, 0)),
                   pl.BlockSpec((None, tm, 3 * D_SB), lambda i, j: (i, j, 0)),
                   pl.BlockSpec((None, 2 * N_ML, tm), lambda i, j: (i, 0, j))),
        compiler_params=_cparams(("parallel", "parallel")),
        name="norm_inproj",
    )(x, mod, w_ml, w_dil, w_sb)


def _mlstm_kernel(u_ref, grow_ref, convw_ref, convb_ref, wq_ref, wk_ref, wv_ref, gbc_ref, gbr_ref,
                  y_ref, prev_ref, c_ref, n_ref, m_ref):
    L = CHUNK

    @pl.when(pl.program_id(1) == 0)
    def _():
        prev_ref[...] = jnp.zeros_like(prev_ref)
        c_ref[...] = jnp.zeros_like(c_ref)
        n_ref[...] = jnp.zeros_like(n_ref)
        m_ref[...] = jnp.zeros_like(m_ref)

    row = _iota((L, D_ML), 0)
    convw = convw_ref[...]
    ltri = (_iota((L, L), 1) <= _iota((L, L), 0)).astype(bf16)
    utri = (_iota((L, L), 0) <= _iota((L, L), 1)).astype(bf16)
    causal = _iota((L, L), 1) <= _iota((L, L), 0)
    lane = _iota((1, D_ML), 1)
    heads = range(N_ML)
    mhs = [(lane >= hd * HEAD_DIM) & (lane < (hd + 1) * HEAD_DIM) for hd in heads]
    same_head = (_iota((D_ML, D_ML), 0) // HEAD_DIM) == (_iota((D_ML, D_ML), 1) // HEAD_DIM)

    def logsig(t):
        return jnp.minimum(t, 0.0) - jnp.log1p(jnp.exp(-jnp.abs(t)))

    def chain(bb):
        u = u_ref[bb]
        xm = u[:, :D_ML]
        og = u[:, D_ML:2 * D_ML]
        gcol = u[:, 2 * D_ML:] + gbc_ref[...]
        grow = grow_ref[bb] + gbr_ref[...]
        prev = prev_ref[bb]
        xc = xm * convw[CONV_WIDTH - 1:CONV_WIDTH, :]
        for sft in range(1, CONV_WIDTH):
            shifted = jnp.where(row < sft, pltpu.roll(prev, sft, 0), pltpu.roll(xm, sft, 0))
            xc = xc + shifted * convw[CONV_WIDTH - 1 - sft:CONV_WIDTH - sft, :]
        prev_ref[bb] = xm
        xc = xc + convb_ref[...]
        xc = xc * lax.logistic(xc)
        xcb = xc.astype(bf16)
        wide = lambda col: jnp.broadcast_to(col, (L, LANES))
        twice = lambda a: jnp.concatenate([a, a], axis=1)
        lf_wide = [wide(logsig(gcol[:, N_ML + hd:N_ML + hd + 1])) for hd in heads]
        ig_wide = [wide(gcol[:, hd:hd + 1]) for hd in heads]
        lf_split = [_split_bf16(x) for x in lf_wide]
        rh, rl = _split_bf16(logsig(grow))
        yield
        q = _dot(xcb, wq_ref[...])
        k = _dot(xcb, wk_ref[...])
        v = _dot(xm.astype(bf16), wv_ref[...])
        cum_wide = [_dot(ltri, hi) + _dot(ltri, lo) for hi, lo in lf_split]
        cum_row = _dot(rh, utri) + _dot(rl, utri)
        kb = k.astype(bf16)
        vb = v.astype(bf16)
        c_bd = c_ref[bb]
        n_st = n_ref[bb]
        m_st = m_ref[bb]
        c_bdb = c_bd.astype(bf16)
        qhs = [jnp.where(mhs[hd], q, 0.0) for hd in heads]
        qhb = [qh.astype(bf16) for qh in qhs]
        qn_split = [_split_bf16(qh * n_st) for qh in qhs]
        ones_ml = jnp.ones((D_ML, LANES), bf16)
        ones_l = jnp.ones((L, LANES), bf16)
        yield
        qk = [_dot_nt(qhb[hd], kb) for hd in heads]
        qc = [_dot(qhb[hd], c_bdb) for hd in heads]
        qn = [_dot(hi, ones_ml) + _dot(lo, ones_ml) for hi, lo in qn_split]
        yield
        m_ts, inters, sws = [], [], []
        wfull = jnp.zeros((L, D_ML), f32)
        decay_l = jnp.zeros((1, D_ML), f32)
        mnew_l = jnp.zeros((1, D_ML), f32)
        for hd in heads:
            cumc = cum_wide[hd]
            cumr = cum_row[N_ML + hd:N_ML + hd + 1, :]
            igr = grow[hd:hd + 1, :]
            m_prev = m_st[:, hd * HEAD_DIM:hd * HEAD_DIM + 1]
            log_d = jnp.where(causal, cumc - cumr + igr, -jnp.inf)
            log_inter = cumc + m_prev
            m_t = jnp.maximum(log_inter, jnp.max(log_d, axis=-1, keepdims=True))
            m_ts.append(m_t)
            inters.append(jnp.exp(log_inter - m_t))
            sws.append(qk[hd] * jnp.exp(log_d - m_t))
            last = cumc[L - 1:L, :]
            log_w = last - cumc + ig_wide[hd]
            m_new = jnp.maximum(last + m_prev, jnp.max(log_w, axis=0, keepdims=True))
            wfull = jnp.where(mhs[hd], twice(jnp.exp(log_w - m_new)), wfull)
            decay_l = jnp.where(mhs[hd], twice(jnp.exp(last + m_prev - m_new)), decay_l)
            mnew_l = jnp.where(mhs[hd], twice(m_new), mnew_l)
        kw = k * wfull
        sw_split = [_split_bf16(sw) for sw in sws]
        kw16 = kw.astype(bf16)
        yield
        swv = [_dot(sw_split[hd][0], vb) for hd in heads]
        swsum = [_dot(hi, ones_l) + _dot(lo, ones_l) for hi, lo in sw_split]
        upd = lax.dot_general(kw16, vb, _TN, preferred_element_type=f32)
        yield
        hmix = jnp.zeros((L, D_ML), f32)
        for hd in heads:
            num = twice(inters[hd]) * qc[hd] + swv[hd]
            den = inters[hd] * qn[hd] + swsum[hd]
            hout = num / twice(jnp.maximum(jnp.abs(den), jnp.exp(-m_ts[hd])))
            hmix = jnp.where(mhs[hd], hout, hmix)
        c_ref[bb] = decay_l * c_bd + jnp.where(same_head, upd, 0.0)
        n_ref[bb] = decay_l * n_st + jnp.sum(kw, axis=0, keepdims=True)
        m_ref[bb] = mnew_l
        y_ref[bb] = lax.logistic(og) * hmix
        yield

    chains = [chain(bb) for bb in range(u_ref.shape[0])]
    for _ in range(MLSTM_STAGES):
        for ch_ in chains:
            next(ch_)


def _mlstm(u_ml, g_row, conv_w, conv_b, wq_bd, wk_bd, wv_bd, gb_col, gb_row):
    b, s, _ = u_ml.shape
    nb = MLSTM_ROWS if b % MLSTM_ROWS == 0 else 1
    full = lambda a: pl.BlockSpec(a.shape, lambda i, j: (0,) * a.ndim)
    return pl.pallas_call(
        _mlstm_kernel,
        out_shape=jax.ShapeDtypeStruct((b, s, D_ML), f32),
        grid=(b // nb, s // CHUNK),
        in_specs=[pl.BlockSpec((nb, CHUNK, ML_COLS), lambda i, j: (i, j, 0)),
                  pl.BlockSpec((nb, 2 * N_ML, CHUNK), lambda i, j: (i, 0, j)),
                  full(conv_w), full(conv_b), full(wq_bd), full(wk_bd), full(wv_bd), full(gb_col), full(gb_row)],
        out_specs=pl.BlockSpec((nb, CHUNK, D_ML), lambda i, j: (i, j, 0)),
        scratch_shapes=[pltpu.VMEM((nb, CHUNK, D_ML), f32), pltpu.VMEM((nb, D_ML, D_ML), f32),
                        pltpu.VMEM((nb, 1, D_ML), f32), pltpu.VMEM((nb, 1, D_ML), f32)],
        compiler_params=_cparams(("parallel", "arbitrary")),
        name="mlstm",
    )(u_ml, g_row, conv_w, conv_b, wq_bd, wk_bd, wv_bd, gb_col, gb_row)


def _rows(start, size, stride):
    return pl.ds(start, size) if stride == 1 else pl.ds(start, size, stride=stride)


def _dil_kernel(q_ref, k_ref, v_ref, slope_ref, o_ref, *scratch, seq, tile):
    npat = len(DILATIONS)
    kd = scratch[0:npat]
    vd = scratch[npat:2 * npat]
    op = scratch[2 * npat:3 * npat]
    ls = scratch[3 * npat:4 * npat]
    bias_ref = scratch[4 * npat]
    L = CHUNK
    ti = pl.program_id(2)
    lane = _iota((1, LANES), 1)
    m0 = lane < HEAD_DIM

    @pl.when(ti == 0)
    def _():
        row = _iota((L, L), 0)
        col = _iota((L, L), 1)
        for p, d in enumerate(DILATIONS):
            per = seq // d
            for r in range(d):
                kd[p][r * per:(r + 1) * per, :] = k_ref[_rows(r, per, d), :].astype(bf16)
                vd[p][r * per:(r + 1) * per, :] = v_ref[_rows(r, per, d), :].astype(bf16)
            for j in range(2):
                sl = slope_ref[j][0:1, :] * float(d)
                bias_ref[p, j, 0] = jnp.where(col >= row, -(sl * (row - col + DIL_SPAN).astype(f32)), -jnp.inf)
                bias_ref[p, j, 1] = jnp.full((L, L), -jnp.inf, f32)
                bias_ref[p, j, 2] = jnp.where(col <= row, -(sl * (row - col).astype(f32)), -jnp.inf)

    nunits = tile // L
    for p, d in enumerate(DILATIONS):
        per = seq // d
        blocks_per_res = tile // d // L

        def body(it, carry, p=p, d=d, per=per, blocks_per_res=blocks_per_res):
            units = []
            for c in range(DIL_UNITS):
                u = it * DIL_UNITS + c
                r = u // blocks_per_res
                blk = u % blocks_per_res
                start = blk * (L * d) + r
                q = q_ref[_rows(start, L, d), :].astype(bf16)
                m_first = (ti * tile) // d + blk * L
                base = r * per + m_first
                has_prev = m_first > 0
                cur0 = pl.multiple_of(base, L)
                prev0 = pl.multiple_of(jnp.where(has_prev, base - L, base), L)
                units.append(dict(start=start, q=q, pk=jnp.where(has_prev, 0, 1),
                                  kc=kd[p][pl.ds(cur0, L), :], vc=vd[p][pl.ds(cur0, L), :],
                                  kp=kd[p][pl.ds(prev0, L), :], vp=vd[p][pl.ds(prev0, L), :]))
            chains = [(un, j) for un in units for j in range(2)]
            scores = []
            for un, j in chains:
                qj = jnp.where(m0 if j == 0 else jnp.logical_not(m0), un["q"], jnp.zeros_like(un["q"]))
                scores.append((_dot_nt(qj, un["kp"]) + bias_ref[p, j, un["pk"]],
                               _dot_nt(qj, un["kc"]) + bias_ref[p, j, 2]))
            probs = []
            for sp, sc in scores:
                mx = jnp.max(jnp.maximum(sp, sc), axis=-1, keepdims=True)
                pp = jnp.exp(sp - mx)
                pc = jnp.exp(sc - mx)
                den = jnp.sum(pp + pc, axis=-1, keepdims=True)
                probs.append((pp.astype(bf16), pc.astype(bf16), den, mx + jnp.log(den)))
            outs = [(_dot(pp, un["vp"]) + _dot(pc, un["vc"])) / den
                    for (un, j), (pp, pc, den, _) in zip(chains, probs)]
            for c, un in enumerate(units):
                op[p][_rows(un["start"], L, d), :] = jnp.where(m0, outs[2 * c], outs[2 * c + 1])
                ls[p][_rows(un["start"], L, d), :] = jnp.where(m0, probs[2 * c][3], probs[2 * c + 1][3])
            return carry

        lax.fori_loop(0, nunits // DIL_UNITS, body, 0)

    l0, l1, l2 = ls[0][...], ls[1][...], ls[2][...]
    mx = jnp.maximum(jnp.maximum(l0, l1), l2)
    e0, e1, e2 = jnp.exp(l0 - mx), jnp.exp(l1 - mx), jnp.exp(l2 - mx)
    o_ref[...] = ((e0 * op[0][...] + e1 * op[1][...] + e2 * op[2][...]) / (e0 + e1 + e2)).astype(o_ref.dtype)


def _dilated(u_dil, slopes):
    b, s, _ = u_dil.shape
    npair = N_DIL // 2
    tile = DILATIONS[-1] * CHUNK
    assert s % tile == 0 and (tile // CHUNK) % DIL_UNITS == 0
    seq_buf = lambda dt: pltpu.VMEM((s, LANES), dt)
    tile_buf = pltpu.VMEM((tile, LANES), f32)
    npat = len(DILATIONS)
    return pl.pallas_call(
        functools.partial(_dil_kernel, seq=s, tile=tile),
        out_shape=jax.ShapeDtypeStruct((b, s, D_DIL), bf16),
        grid=(b, npair, s // tile),
        in_specs=[pl.BlockSpec((None, tile, LANES), lambda i, h, n: (i, n, h)),
                  pl.BlockSpec((None, s, LANES), lambda i, h, n: (i, 0, npair + h)),
                  pl.BlockSpec((None, s, LANES), lambda i, h, n: (i, 0, 2 * npair + h)),
                  pl.BlockSpec((None, 2, 8, LANES), lambda i, h, n: (h, 0, 0, 0))],
        out_specs=pl.BlockSpec((None, tile, LANES), lambda i, h, n: (i, n, h)),
        scratch_shapes=([seq_buf(bf16)] * (2 * npat) + [tile_buf] * (2 * npat)
                        + [pltpu.VMEM((npat, 2, 3, CHUNK, CHUNK), f32)]),
        compiler_params=_cparams(("parallel", "parallel", "arbitrary")),
        name="dilated",
    )(u_dil, u_dil, u_dil, slopes)


def _sb_kernel(q_ref, k_ref, v_ref, o_ref, qm_ref, r_ref, acc_ref, *, nq):
    L = CHUNK
    tile0 = pl.program_id(1) * nq
    row = _iota((L, L), 0)
    col = _iota((L, L), 1)
    strict = col < row
    lane = _iota((1, LANES), 1)
    m0 = lane < HEAD_DIM
    npair = N_SB // 2
    csum = jnp.concatenate([(_iota((L, L), 0) > _iota((L, L), 1)).astype(bf16), jnp.ones((L, L), bf16)], axis=1)

    for qb in range(nq):
        for hp in range(npair):
            q = q_ref[qb * L:(qb + 1) * L, hp * LANES:(hp + 1) * LANES]
            qm_ref[(qb * npair + hp) * 2] = jnp.where(m0, q, jnp.zeros_like(q))
            qm_ref[(qb * npair + hp) * 2 + 1] = jnp.where(m0, jnp.zeros_like(q), q)

    def sweep(it, diagonal):
        rmax = None
        for g0 in range(0, nq, SB_GROUP):
            chains = []
            for qb in range(g0, g0 + SB_GROUP):
                qi = tile0 + qb
                k0 = pl.multiple_of(jnp.maximum(qi - it, 0) * L, L)
                out_of_keys = jnp.where(qi - it <= 0, SB_NO_KEYS, 0.0)
                for hp in range(npair):
                    kblk = k_ref[pl.ds(k0, L), hp * LANES:(hp + 1) * LANES]
                    vblk = v_ref[pl.ds(k0, L), hp * LANES:(hp + 1) * LANES]
                    for j in range(2):
                        chains.append(((qb * npair + hp) * 2 + j, kblk, vblk, out_of_keys))
            zs = [_dot_nt(qm_ref[c], kblk) for c, kblk, _, _ in chains]
            zl, lkb = [], []
            for z in zs:
                ls = jnp.minimum(z, 0.0) - jnp.log(1.0 + jnp.exp(-jnp.abs(z)))
                lk = ls - z
                zl.append(ls)
                lkb.append((jnp.where(strict, lk, 0.0) if diagonal else lk).astype(bf16))
            ts = [_dot(x, csum) for x in lkb]
            avs = []
            for (c, _, _, out_of_keys), s, t in zip(chains, zl, ts):
                if diagonal:
                    a = jnp.where(strict, jnp.exp(s + t[:, :L]), 0.0)
                    rs = t[:, L:] + out_of_keys
                else:
                    a = jnp.exp(s + r_ref[c] + t[:, :L])
                    rs = r_ref[c] + t[:, L:] + out_of_keys
                avs.append(a.astype(bf16))
                r_ref[c] = rs
                rmax = rs if rmax is None else jnp.maximum(rmax, rs)
            for (c, _, vblk, _), a in zip(chains, avs):
                if diagonal:
                    acc_ref[c] = _dot(a, vblk)
                else:
                    acc_ref[c] += _dot(a, vblk)
        return jnp.max(rmax) > SB_DEAD_LOG

    alive0 = sweep(0, True)

    def body(carry):
        _, it = carry
        return sweep(it, False), it + 1

    lax.while_loop(lambda carry: carry[0], body, (alive0, jnp.int32(1)))

    for qb in range(nq):
        for hp in range(npair):
            c = (qb * npair + hp) * 2
            o_ref[qb * L:(qb + 1) * L, hp * LANES:(hp + 1) * LANES] = (
                jnp.where(m0, acc_ref[c], acc_ref[c + 1]).astype(o_ref.dtype))


def _stickbreak(u_sb):
    b, s, _ = u_sb.shape
    nq = SB_QBLOCKS
    tq = nq * CHUNK
    nchain = nq * N_SB
    return pl.pallas_call(
        functools.partial(_sb_kernel, nq=nq),
        out_shape=jax.ShapeDtypeStruct((b, s, D_SB), bf16),
        grid=(b, s // tq),
        in_specs=[pl.BlockSpec((None, tq, D_SB), lambda i, n: (i, n, 0)),
                  pl.BlockSpec((None, s, D_SB), lambda i, n: (i, 0, 1)),
                  pl.BlockSpec((None, s, D_SB), lambda i, n: (i, 0, 2))],
        out_specs=pl.BlockSpec((None, tq, D_SB), lambda i, n: (i, n, 0)),
        scratch_shapes=[pltpu.VMEM((nchain, CHUNK, LANES), bf16),
                        pltpu.VMEM((nchain, CHUNK, CHUNK), f32),
                        pltpu.VMEM((nchain, CHUNK, LANES), f32)],
        compiler_params=_cparams(("parallel", "arbitrary")),
        name="stickbreak",
    )(u_sb, u_sb, u_sb)


def _post_kernel(x_ref, yml_ref, ydil_ref, ysb_ref, mod_ref, gh_ref, wout_ref, wr_hi_ref, wr_lo_ref, rb_ref,
                 xo_ref, h_ref, selt_ref, w_ref):
    mod = mod_ref[...]
    y = jnp.concatenate([yml_ref[...], ydil_ref[...].astype(f32), ysb_ref[...].astype(f32)], axis=1)
    pair = ((_iota((LANES, LANES), 0) // HEAD_DIM) == (_iota((LANES, LANES), 1) // HEAD_DIM)).astype(bf16)
    cols = []
    for cblk in range(y.shape[1] // LANES):
        yb = y[:, cblk * LANES:(cblk + 1) * LANES]
        hi, lo = _split_bf16(yb * yb)
        ms = (_dot(hi, pair) + _dot(lo, pair)) * (1.0 / HEAD_DIM)
        cols.append(yb * lax.rsqrt(ms + EPS))
    yn = (jnp.concatenate(cols, axis=1) * gh_ref[...]).astype(bf16)
    xn = x_ref[...] + mod[2:3, :] * _dot(yn, wout_ref[...])
    xo_ref[...] = xn
    h2 = _rms(xn) * (1.0 + mod[4:5, :]) + mod[3:4, :]
    hh, hl = _split_bf16(h2)
    h_ref[...] = h2
    tm = x_ref.shape[0]
    row = _iota((SUBLANES, tm), 0)
    grow = row < N_GROUPS
    logits = _dot(hh, wr_hi_ref[...]) + _dot(hl, wr_hi_ref[...]) + _dot(hh, wr_lo_ref[...])
    logits_t = logits.T
    sc, bs = [], []
    for j in range(EPG):
        sc.append(lax.logistic(logits_t[j * SUBLANES:(j + 1) * SUBLANES, :]))
        bs.append(jnp.where(grow, sc[j] + rb_ref[j][:, 0:1], -jnp.inf))
    hi01, lo01 = jnp.maximum(bs[0], bs[1]), jnp.minimum(bs[0], bs[1])
    hi23, lo23 = jnp.maximum(bs[2], bs[3]), jnp.minimum(bs[2], bs[3])
    top1 = jnp.maximum(hi01, hi23)
    top2 = jnp.maximum(jnp.minimum(hi01, hi23), jnp.maximum(lo01, lo23))
    gscore = jnp.where(grow, top1 + top2, -jnp.inf)
    gmax = jnp.max(gscore, axis=0, keepdims=True)
    gidx = jnp.min(jnp.where(gscore == gmax, row, SUBLANES), axis=0, keepdims=True)
    gsel = row == gidx
    picks, vals = [], []
    for j in range(EPG):
        rank = jnp.zeros(bs[j].shape, jnp.int32)
        for jj in range(EPG):
            if jj != j:
                ahead = (bs[jj] >= bs[j]) if jj < j else (bs[jj] > bs[j])
                rank = rank + ahead.astype(jnp.int32)
        picks.append(gsel & (rank < 2))
        vals.append(jnp.sum(jnp.where(gsel, sc[j], 0.0), axis=0, keepdims=True))
    erow = _iota((N_EXPERTS, SUBLANES), 0)
    gcol = _iota((N_EXPERTS, SUBLANES), 1)
    sel_t = jnp.zeros((N_EXPERTS, tm), f32)
    for j in range(EPG):
        place = ((erow == gcol * EPG + j) & (gcol < N_GROUPS)).astype(bf16)
        sel_t = sel_t + _dot(place, picks[j].astype(bf16))
    selt_ref[...] = sel_t.astype(selt_ref.dtype)
    seen = jnp.zeros_like(vals[0])
    w_lo = jnp.zeros_like(vals[0])
    w_hi = jnp.zeros_like(vals[0])
    for j in range(EPG):
        picked = jnp.sum(picks[j].astype(f32), axis=0, keepdims=True) > 0.0
        w_lo = w_lo + jnp.where(picked & (seen == 0.0), vals[j], 0.0)
        w_hi = w_hi + jnp.where(picked & (seen == 1.0), vals[j], 0.0)
        seen = seen + picked.astype(f32)
    tot = w_lo + w_hi
    w_ref[...] = jnp.where(row == 0, w_lo / tot, jnp.where(row == 1, w_hi / tot, 0.0))


def _post(x, y_ml, y_dil, y_sb, mod, g_head, w_out, wr_hi, wr_lo, rb, tm):
    b, s, d = x.shape
    tok = lambda w: pl.BlockSpec((None, tm, w), lambda i, j: (i, j, 0))
    full = lambda a: pl.BlockSpec(a.shape, lambda i, j: (0,) * a.ndim)
    tok_lanes = lambda rows: pl.BlockSpec((rows, tm), lambda i, j: (0, i * (s // tm) + j))
    return pl.pallas_call(
        _post_kernel,
        out_shape=(jax.ShapeDtypeStruct((b, s, d), f32),
                   jax.ShapeDtypeStruct((b, s, d), f32),
                   jax.ShapeDtypeStruct((N_EXPERTS, b * s), bf16),
                   jax.ShapeDtypeStruct((SUBLANES, b * s), f32)),
        grid=(b, s // tm),
        in_specs=[tok(d), tok(D_ML), tok(D_DIL), tok(D_SB), pl.BlockSpec((None, 6, d), lambda i, j: (i, 0, 0)),
                  full(g_head), full(w_out), full(wr_hi), full(wr_lo), full(rb)],
        out_specs=(tok(d), tok(d), tok_lanes(N_EXPERTS), tok_lanes(SUBLANES)),
        compiler_params=_cparams(("parallel", "parallel")),
        name="merge_outproj_router",
    )(x, y_ml, y_dil, y_sb, mod, g_head, w_out, wr_hi, wr_lo, rb)


def _plan_kernel(selt_ref, lpos0_ref, lpos1_ref, base_ref, nchunk_ref, lofs_ref, tiles_ref, tails_ref,
                 before_ref, cnt_ref, lofs_sc, *, ntw, ntile_pad, nrow_tiles):
    phase = pl.program_id(0)
    step = pl.program_id(1)
    tk = MOE_TOK
    sel = selt_ref[...]
    tlane = _iota((1, ntw), 1)

    @pl.when((phase == 0) & (step == 0))
    def _():
        cnt_ref[...] = jnp.zeros_like(cnt_ref)
        before_ref[...] = (_iota((tk, tk), 0) < _iota((tk, tk), 1)).astype(bf16)

    @pl.when(phase == 0)
    def _():
        cnt = cnt_ref[...]
        for j in range(PLAN_GROUP):
            here = _dot(sel[:, j * tk:(j + 1) * tk], jnp.ones((tk, ntw), bf16))
            cnt = jnp.where(tlane == step * PLAN_GROUP + j, here, cnt)
        cnt_ref[...] = cnt
        lpos0_ref[...] = jnp.zeros_like(lpos0_ref)
        lpos1_ref[...] = jnp.zeros_like(lpos1_ref)

    @pl.when((phase == 1) & (step == 0))
    def _():
        cnt = cnt_ref[...]
        nch = jnp.floor((cnt + (MOE_CHUNK - 1.0)) * (1.0 / MOE_CHUNK))
        nchb = nch.astype(bf16)
        earlier_tiles = (_iota((ntw, ntw), 0) < _iota((ntw, ntw), 1)).astype(bf16)
        earlier_experts = (_iota((N_EXPERTS, N_EXPERTS), 1) < _iota((N_EXPERTS, N_EXPERTS), 0)).astype(bf16)
        in_expert = _dot(nchb, earlier_tiles) * MOE_CHUNK
        lofs = _dot(earlier_experts, nchb) * MOE_CHUNK
        total = _dot(nchb, jnp.ones((ntw, LANES), bf16)) * MOE_CHUNK
        padded = jnp.floor((total + (MOE_ROWS - 1.0)) * (1.0 / MOE_ROWS)) * MOE_ROWS
        rows, acc = [], jnp.zeros((1, LANES), f32)
        for e in range(N_EXPERTS):
            rows.append(acc)
            acc = acc + padded[e:e + 1, :]
        off = jnp.concatenate(rows, axis=0)
        base_ref[...] = (off[:, 0:1] + in_expert).astype(jnp.int32)
        nchunk_ref[...] = nch.astype(jnp.int32)
        lofs_ref[...] = lofs.astype(jnp.int32)
        lofs_sc[...] = lofs
        lane = _iota((1, LANES), 1)
        tails_ref[...] = jnp.where(
            lane == 0, off + total,
            jnp.where(lane == 1, (padded - total) * (1.0 / MOE_CHUNK),
                      jnp.where(lane == 2, acc, jnp.where(lane == 3, nrow_tiles - acc * (1.0 / MOE_ROWS), 0.0)))
        ).astype(jnp.int32)
        start = _iota((1, ntile_pad), 1).astype(f32) * MOE_ROWS
        ends = (off + padded)[:, 0:1]
        owner = jnp.sum((start >= ends).astype(f32), axis=0, keepdims=True)
        tiles_ref[...] = jnp.zeros_like(tiles_ref)
        tiles_ref[0:1, :] = jnp.minimum(owner, N_EXPERTS - 1.0).astype(jnp.int32)
        tiles_ref[1:2, :] = (start < acc[:, 0:1]).astype(jnp.int32)

    @pl.when(phase == 1)
    def _():
        for j in range(PLAN_GROUP):
            cols = slice(j * tk, (j + 1) * tk)
            mine = jnp.sum(jnp.where(tlane == step * PLAN_GROUP + j, lofs_sc[...], 0.0), axis=1, keepdims=True)
            pos = _dot(sel[:, cols], before_ref[...]) + mine
            picked = sel[:, cols] > 0
            lpos0_ref[:, cols] = jnp.min(jnp.where(picked, pos, 1e9), axis=0, keepdims=True).astype(jnp.int32)
            lpos1_ref[:, cols] = jnp.max(jnp.where(picked, pos, -1.0), axis=0, keepdims=True).astype(jnp.int32)


def _plan(sel_t, nrow_tiles):
    ne, t = sel_t.shape
    nt = t // MOE_TOK
    ntw = -(-nt // LANES) * LANES
    ntile_pad = -(-nrow_tiles // LANES) * LANES
    i32 = jnp.int32
    whole = lambda shape: pl.BlockSpec(shape, lambda p, i: (0, 0))
    per_tile = pl.BlockSpec((1, PLAN_GROUP * MOE_TOK), lambda p, i: (0, i * p))
    lpos0, lpos1, base, nchunk, lofs, tiles, tails = pl.pallas_call(
        functools.partial(_plan_kernel, ntw=ntw, ntile_pad=ntile_pad, nrow_tiles=nrow_tiles),
        out_shape=(jax.ShapeDtypeStruct((1, t), i32), jax.ShapeDtypeStruct((1, t), i32),
                   jax.ShapeDtypeStruct((ne, ntw), i32), jax.ShapeDtypeStruct((ne, ntw), i32),
                   jax.ShapeDtypeStruct((ne, ntw), i32), jax.ShapeDtypeStruct((8, ntile_pad), i32),
                   jax.ShapeDtypeStruct((ne, LANES), i32)),
        grid=(2, nt // PLAN_GROUP),
        in_specs=[pl.BlockSpec((ne, PLAN_GROUP * MOE_TOK), lambda p, i: (0, i))],
        out_specs=(per_tile, per_tile, whole((ne, ntw)), whole((ne, ntw)), whole((ne, ntw)),
                   whole((8, ntile_pad)), whole((ne, LANES))),
        scratch_shapes=[pltpu.VMEM((MOE_TOK, MOE_TOK), bf16), pltpu.VMEM((ne, ntw), f32),
                        pltpu.VMEM((ne, ntw), f32)],
        compiler_params=_cparams(("arbitrary", "arbitrary")),
        name="moe_plan",
    )(sel_t)
    flat = lambda a: a[:, :nt].T.reshape(nt * ne)
    return dict(lpos0=lpos0, lpos1=lpos1, base=flat(base), nchunk=flat(nchunk), lofs=flat(lofs),
                owner=tiles[0, :nrow_tiles], used=tiles[1, :nrow_tiles],
                tail_start=tails[:, 0], tail_chunks=tails[:, 1], unused=tails[0, 2:4])


def _segment_chunks(tile, base_ref, nchunk_ref, lofs_ref, make_copy, start):
    for e in range(N_EXPERTS):
        idx = tile * N_EXPERTS + e
        first_global = base_ref[idx]
        first_local = lofs_ref[idx]

        def one(c, carry, e=e, first_global=first_global, first_local=first_local):
            cp = make_copy(pl.multiple_of(first_local + c * MOE_CHUNK, MOE_CHUNK),
                           pl.multiple_of(first_global + c * MOE_CHUNK, MOE_CHUNK), e % 2)
            cp.start(priority=e % 2) if start else cp.wait()
            return carry

        lax.fori_loop(0, nchunk_ref[idx], one, 0)


def _one_hot_rows(lpos_ref):
    return (_iota((MOE_LOCAL, MOE_TOK), 0) == lpos_ref[...]).astype(bf16)


def _dispatch_kernel(base_ref, nchunk_ref, lofs_ref, tstart_ref, tchunks_ref, unused_ref, lpos0_ref, lpos1_ref,
                     h_ref, xs_hbm, loc, zeros, sem):
    tile = pl.program_id(0)
    last = pl.num_programs(0) - 1
    slot = tile % 2
    place = _one_hot_rows(lpos0_ref) + _one_hot_rows(lpos1_ref)
    loc[slot] = _dot(place, h_ref[...].astype(bf16))

    def copier(slot_):
        def seg_copy(local_row, global_row, s):
            return pltpu.make_async_copy(loc.at[slot_, pl.ds(local_row, MOE_CHUNK)],
                                         xs_hbm.at[pl.ds(global_row, MOE_CHUNK)], sem.at[slot_, s])
        return seg_copy

    _segment_chunks(tile, base_ref, nchunk_ref, lofs_ref, copier(slot), True)

    @pl.when(tile > 0)
    def _():
        _segment_chunks(tile - 1, base_ref, nchunk_ref, lofs_ref, copier(1 - slot), False)

    @pl.when(tile == last)
    def _():
        _segment_chunks(tile, base_ref, nchunk_ref, lofs_ref, copier(slot), False)

    @pl.when(tile == last)
    def _():
        zeros[...] = jnp.zeros_like(zeros)
        for start in (True, False):
            for e in range(N_EXPERTS):
                def one(c, carry, e=e, start=start):
                    row = pl.multiple_of(tstart_ref[e] + c * MOE_CHUNK, MOE_CHUNK)
                    cp = pltpu.make_async_copy(zeros.at[pl.ds(0, MOE_CHUNK)], xs_hbm.at[pl.ds(row, MOE_CHUNK)],
                                               sem.at[0, e % 2])
                    cp.start() if start else cp.wait()
                    return carry
                lax.fori_loop(0, tchunks_ref[e], one, 0)

            def whole_tile(c, carry, start=start):
                row = pl.multiple_of(unused_ref[0] + c * MOE_ROWS, MOE_ROWS)
                cp = pltpu.make_async_copy(zeros, xs_hbm.at[pl.ds(row, MOE_ROWS)], sem.at[0, 0])
                cp.start() if start else cp.wait()
                return carry
            lax.fori_loop(0, unused_ref[1], whole_tile, 0)


def _dispatch(h_rows, plan, nrows):
    t, w = h_rows.shape
    per_tile = lambda: pl.BlockSpec((1, MOE_TOK), lambda i, *_: (0, i))
    return pl.pallas_call(
        _dispatch_kernel,
        out_shape=jax.ShapeDtypeStruct((nrows, w), h_rows.dtype),
        grid_spec=pltpu.PrefetchScalarGridSpec(
            num_scalar_prefetch=6,
            grid=(t // MOE_TOK,),
            in_specs=[per_tile(), per_tile(), pl.BlockSpec((MOE_TOK, w), lambda i, *_: (i, 0))],
            out_specs=pl.BlockSpec(memory_space=pl.ANY),
            scratch_shapes=[pltpu.VMEM((2, MOE_LOCAL, w), f32), pltpu.VMEM((MOE_ROWS, w), f32),
                            pltpu.SemaphoreType.DMA((2, 2))]),
        compiler_params=_cparams(("arbitrary",)),
        name="moe_dispatch",
    )(plan["base"], plan["nchunk"], plan["lofs"], plan["tail_start"], plan["tail_chunks"], plan["unused"],
      plan["lpos0"], plan["lpos1"], h_rows)


def _experts_kernel(owner_ref, used_ref, xs_ref, wg_ref, wu_ref, wd_ref, ys_ref):
    tile = pl.program_id(0)

    @pl.when(used_ref[tile] > 0)
    def _():
        xb = xs_ref[...].astype(bf16)
        g = _dot(xb, wg_ref[...].astype(bf16))
        he = g * lax.logistic(g) * _dot(xb, wu_ref[...].astype(bf16))
        ys_ref[...] = _dot(he.astype(bf16), wd_ref[...].astype(bf16))

    @pl.when(used_ref[tile] == 0)
    def _():
        ys_ref[...] = jnp.zeros_like(ys_ref)


def _experts(xs, owner, used, wg, wu, wd, layer):
    nrows, w = xs.shape
    _, _, d, de = wg.shape
    return pl.pallas_call(
        _experts_kernel,
        out_shape=jax.ShapeDtypeStruct((nrows, w), xs.dtype),
        grid_spec=pltpu.PrefetchScalarGridSpec(
            num_scalar_prefetch=2,
            grid=(nrows // MOE_ROWS,),
            in_specs=[pl.BlockSpec((MOE_ROWS, w), lambda g, own, use: (g * use[g], 0)),
                      pl.BlockSpec((None, None, d, de), lambda g, own, use: (layer, own[g], 0, 0)),
                      pl.BlockSpec((None, None, d, de), lambda g, own, use: (layer, own[g], 0, 0)),
                      pl.BlockSpec((None, None, de, d), lambda g, own, use: (layer, own[g], 0, 0))],
            out_specs=pl.BlockSpec((MOE_ROWS, w), lambda g, own, use: (g, 0))),
        compiler_params=_cparams(("arbitrary",)),
        name="moe_experts",
    )(owner, used, xs, wg, wu, wd)


def _combine_kernel(base_ref, nchunk_ref, lofs_ref, lpos0_ref, lpos1_ref, ys_hbm, x_ref, w_ref, mod_ref, gf_ref,
                    o_ref, loc, sem, *, final):
    tile = pl.program_id(0)
    ntile = pl.num_programs(0)
    slot = tile % 2

    def gather(tile_, slot_, start):
        def seg_copy(local_row, global_row, s):
            return pltpu.make_async_copy(ys_hbm.at[pl.ds(global_row, MOE_CHUNK)],
                                         loc.at[slot_, pl.ds(local_row, MOE_CHUNK)], sem.at[slot_, s])
        _segment_chunks(tile_, base_ref, nchunk_ref, lofs_ref, seg_copy, start)

    @pl.when(tile == 0)
    def _():
        loc[...] = jnp.zeros_like(loc)
        gather(0, 0, True)

    @pl.when(tile + 1 < ntile)
    def _():
        gather(tile + 1, 1 - slot, True)

    gather(tile, slot, False)
    ysl = loc[slot].astype(bf16)
    prow = _iota((MOE_LOCAL, MOE_TOK), 0)
    gates = (jnp.where(prow == lpos0_ref[...], w_ref[0:1, :], 0.0)
             + jnp.where(prow == lpos1_ref[...], w_ref[1:2, :], 0.0))
    g_hi, g_lo = _split_bf16(gates)
    y = (lax.dot_general(g_hi, ysl, _TN, preferred_element_type=f32)
         + lax.dot_general(g_lo, ysl, _TN, preferred_element_type=f32))
    xn = x_ref[...] + mod_ref[...][5:6, :] * y
    if final:
        xn = _rms(xn) * gf_ref[...]
    o_ref[...] = xn


def _combine(ys, plan, x, w_pair, mod, g_final, final):
    b, s, d = x.shape
    t = b * s
    per_batch = s // MOE_TOK
    per_tile = lambda: pl.BlockSpec((1, MOE_TOK), lambda i, *_: (0, i))
    out = pl.pallas_call(
        functools.partial(_combine_kernel, final=final),
        out_shape=jax.ShapeDtypeStruct((t, d), f32),
        grid_spec=pltpu.PrefetchScalarGridSpec(
            num_scalar_prefetch=3,
            grid=(t // MOE_TOK,),
            in_specs=[per_tile(), per_tile(), pl.BlockSpec(memory_space=pl.ANY),
                      pl.BlockSpec((MOE_TOK, d), lambda i, *_: (i, 0)),
                      pl.BlockSpec((SUBLANES, MOE_TOK), lambda i, *_: (0, i)),
                      pl.BlockSpec((None, 6, d), lambda i, *_: (i // per_batch, 0, 0)),
                      pl.BlockSpec((1, d), lambda i, *_: (0, 0))],
            out_specs=pl.BlockSpec((MOE_TOK, d), lambda i, *_: (i, 0)),
            scratch_shapes=[pltpu.VMEM((2, MOE_LOCAL, d), f32), pltpu.SemaphoreType.DMA((2, 2))]),
        compiler_params=_cparams(("arbitrary",)),
        name="moe_combine_final" if final else "moe_combine",
    )(plan["base"], plan["nchunk"], plan["lofs"], plan["lpos0"], plan["lpos1"], ys,
      x.reshape(t, d), w_pair, mod, g_final)
    return out.reshape(b, s, d)


def _moe(x, h_rows, sel_t, w_pair, mod, wg, wu, wd, layer, g_final, final):
    b, s, d = x.shape
    t = b * s
    worst = 2 * t + (t // MOE_TOK) * N_EXPERTS * (MOE_CHUNK - 1) + N_EXPERTS * MOE_ROWS
    nrows = -(-worst // MOE_ROWS) * MOE_ROWS
    plan = _plan(sel_t, nrows // MOE_ROWS)
    xs = _dispatch(h_rows.reshape(t, d), plan, nrows)
    ys = _experts(xs, plan["owner"], plan["used"], wg, wu, wd, layer)
    return _combine(ys, plan, x, w_pair, mod, g_final, final)


def _block_diag(w):
    n, a, c = w.shape
    out = jnp.zeros((n * a, n * c), w.dtype)
    for i in range(n):
        out = out.at[i * a:(i + 1) * a, i * c:(i + 1) * c].set(w[i])
    return out


def kernel(x, c, w_in, conv_w, conv_b, w_mq, w_mk, w_mv, gate_bias, g_head, w_out, w_ada, b_ada,
           w_router, router_bias, w_gate_e, w_up_e, w_down_e, g_final):
    b, s, d = x.shape
    depth = w_in.shape[0]
    qscale = HEAD_DIM ** -0.5
    tm = min(512, s)

    mod_all = _ada(c, w_ada, b_ada).reshape(depth, b, 6, d)

    wr = jnp.zeros((d, LANES // SUBLANES, SUBLANES), f32).at[:, :EPG, :N_GROUPS].set(
        w_router.reshape(d, N_GROUPS, EPG).transpose(0, 2, 1)).reshape(d, LANES)
    wr_hi = wr.astype(bf16)
    wr_lo = (wr - wr_hi.astype(f32)).astype(bf16)
    rb = jnp.zeros((EPG, SUBLANES, LANES), f32).at[:, :N_GROUPS, :].set(
        jnp.broadcast_to(router_bias.reshape(N_GROUPS, EPG).T[:, :, None], (EPG, N_GROUPS, LANES)))

    slope = jnp.exp2(-ALIBI_MAX_BIAS * jnp.arange(1, N_DIL + 1, dtype=f32) / N_DIL)
    slopes = jnp.broadcast_to(slope.reshape(N_DIL // 2, 2, 1, 1), (N_DIL // 2, 2, 8, LANES))
    g_final2 = g_final.reshape(1, d)

    w_ml_all, w_dil_all, w_sb_all = _wprep(w_in, min(256, d))

    for l in range(depth):
        mod = mod_all[l]

        u_ml, u_dil, u_sb, g_row = _inproj(x, mod, w_ml_all[l], w_dil_all[l], w_sb_all[l], tm)

        gb = gate_bias[l]
        gb_col = jnp.zeros((1, LANES), f32).at[0, :2 * N_ML].set(gb.reshape(-1))
        gb_row = gb.reshape(2 * N_ML, 1)
        y_ml = _mlstm(u_ml, g_row, conv_w[l], conv_b[l].reshape(1, D_ML),
                      _block_diag(w_mq[l]).astype(bf16), (_block_diag(w_mk[l]) * qscale).astype(bf16),
                      _block_diag(w_mv[l]).astype(bf16), gb_col, gb_row)
        y_dil = _dilated(u_dil, slopes)
        y_sb = _stickbreak(u_sb)

        x, h_rows, sel_t, w_pair = _post(x, y_ml, y_dil, y_sb, mod, g_head[l].reshape(1, d),
                                           w_out[l].astype(bf16), wr_hi, wr_lo, rb, tm)
        x = _moe(x, h_rows, sel_t, w_pair, mod, w_gate_e, w_up_e, w_down_e, l, g_final2,
                 final=(l == depth - 1))
    return x
```

```python
import functools

import jax
import jax.numpy as jnp
from jax import lax
from jax.experimental import pallas as pl
from jax.experimental.pallas import tpu as pltpu

HEAD_DIM = 64
N_ML = 4
N_DIL = 6
N_SB = 6
D_ML = N_ML * HEAD_DIM
D_DIL = N_DIL * HEAD_DIM
D_SB = N_SB * HEAD_DIM
CONV_WIDTH = 4
CHUNK = 128
DILATIONS = (1, 4, 16)
DIL_SPAN = 128
ALIBI_MAX_BIAS = 8.0
N_EXPERTS = 16
N_GROUPS = 4
EPG = N_EXPERTS // N_GROUPS
EPS = 1e-6
LANES = 128
ML_COLS = 2 * D_ML + LANES
SB_DEAD_LOG = -104.0
SB_NO_KEYS = -1e30
SB_QBLOCKS = 4
SB_GROUP = 4
DIL_UNITS = 8
MOE_ROWS = 512
MOE_TOK = 256
MOE_CHUNK = 8
PLAN_GROUP = 4
SUBLANES = 8
MLSTM_ROWS = 2
MLSTM_STAGES = 6
MOE_LOCAL = 2 * MOE_TOK + N_EXPERTS * MOE_CHUNK
VMEM_LIMIT = 56 * 1024 * 1024

_NT = (((1,), (1,)), ((), ()))
_TN = (((0,), (0,)), ((), ()))

f32 = jnp.float32
bf16 = jnp.bfloat16


def _dot(a, b):
    return jnp.dot(a, b, preferred_element_type=f32)


def _dot_nt(a, b):
    return lax.dot_general(a, b, _NT, preferred_element_type=f32)


def _split_bf16(x):
    hi = x.astype(bf16)
    lo = (x - hi.astype(f32)).astype(bf16)
    return hi, lo


def _iota(shape, dim):
    return lax.broadcasted_iota(jnp.int32, shape, dim)


def _rms(x):
    return x * lax.rsqrt(jnp.mean(x * x, axis=-1, keepdims=True) + EPS)


def _cparams(sem):
    return pltpu.CompilerParams(dimension_semantics=sem, vmem_limit_bytes=VMEM_LIMIT)


def _ada_kernel(c_ref, w_ref, b_ref, o_ref):
    c = c_ref[...]
    c_act = (c * lax.logistic(c)).astype(bf16)
    o_ref[...] = _dot(c_act, w_ref[...].astype(bf16)) + b_ref[...]


def _ada(c, w_ada, b_ada):
    depth, d, nd = w_ada.shape
    b = c.shape[0]
    nblk = nd // d
    return pl.pallas_call(
        _ada_kernel,
        out_shape=jax.ShapeDtypeStruct((depth, b, nd), f32),
        grid=(depth, nblk),
        in_specs=[pl.BlockSpec((b, d), lambda l, j: (0, 0)),
                  pl.BlockSpec((None, d, d), lambda l, j: (l, 0, j)),
                  pl.BlockSpec((None, 1, d), lambda l, j: (l, 0, j))],
        out_specs=pl.BlockSpec((None, b, d), lambda l, j: (l, 0, j)),
        compiler_params=_cparams(("parallel", "parallel")),
        name="ada_mod",
    )(c, w_ada, b_ada.reshape(depth, 1, nd))


def _wprep_kernel(w_ref, wml_ref, wdil_ref, wsb_ref):
    w = w_ref[...]
    off_dil = 2 * D_ML + 2 * N_ML
    off_sb = off_dil + 3 * D_DIL
    qs = HEAD_DIM ** -0.5
    ml = w[:, :ML_COLS]
    wml_ref[...] = jnp.where(_iota((1, ML_COLS), 1) < off_dil, ml, 0.0).astype(bf16)
    q_cols = jnp.where(_iota((1, 3 * D_DIL), 1) < D_DIL, qs, 1.0)
    wdil_ref[...] = (w[:, off_dil:off_sb] * q_cols).astype(bf16)
    wsb_ref[...] = (w[:, off_sb:off_sb + 3 * D_SB] * q_cols).astype(bf16)


def _wprep(w_in, rows):
    depth, d, d_in = w_in.shape
    assert d_in == 2 * D_ML + 2 * N_ML + 3 * D_DIL + 3 * D_SB and D_DIL == D_SB
    out = lambda w: pl.BlockSpec((None, rows, w), lambda l, i: (l, i, 0))
    return pl.pallas_call(
        _wprep_kernel,
        out_shape=(jax.ShapeDtypeStruct((depth, d, ML_COLS), bf16),
                   jax.ShapeDtypeStruct((depth, d, 3 * D_DIL), bf16),
                   jax.ShapeDtypeStruct((depth, d, 3 * D_SB), bf16)),
        grid=(depth, d // rows),
        in_specs=[pl.BlockSpec((None, rows, d_in), lambda l, i: (l, i, 0))],
        out_specs=(out(ML_COLS), out(3 * D_DIL), out(3 * D_SB)),
        compiler_params=_cparams(("parallel", "parallel")),
        name="inproj_weight_layout",
    )(w_in)


def _inproj_kernel(x_ref, mod_ref, wml_ref, wdil_ref, wsb_ref,
                   uml_ref, udil_ref, usb_ref, grow_ref):
    x = x_ref[...]
    mod = mod_ref[...]
    h = (_rms(x) * (1.0 + mod[1:2, :]) + mod[0:1, :]).astype(bf16)
    uml = _dot(h, wml_ref[...])
    uml_ref[...] = uml
    udil_ref[...] = _dot(h, wdil_ref[...])
    usb_ref[...] = _dot(h, wsb_ref[...]).astype(bf16)
    grow_ref[...] = uml[:, 2 * D_ML:].T[:2 * N_ML, :]


def _inproj(x, mod, w_ml, w_dil, w_sb, tm):
    b, s, d = x.shape
    full = lambda a: pl.BlockSpec(a.shape, lambda i, j: (0,) * a.ndim)
    return pl.pallas_call(
        _inproj_kernel,
        out_shape=(jax.ShapeDtypeStruct((b, s, ML_COLS), f32),
                   jax.ShapeDtypeStruct((b, s, 3 * D_DIL), f32),
                   jax.ShapeDtypeStruct((b, s, 3 * D_SB), bf16),
                   jax.ShapeDtypeStruct((b, 2 * N_ML, s), f32)),
        grid=(b, s // tm),
        in_specs=[pl.BlockSpec((None, tm, d), lambda i, j: (i, j, 0)),
                  pl.BlockSpec((None, 6, d), lambda i, j: (i, 0, 0)),
                  full(w_ml), full(w_dil), full(w_sb)],
        out_specs=(pl.BlockSpec((None, tm, ML_COLS), lambda i, j: (i, j, 0)),
                   pl.BlockSpec((None, tm, 3 * D_DIL), lambda i, j: (i, j, 0)),
                   pl.BlockSpec((None, tm, 3 * D_SB), lambda i, j: (i, j, 0)),
                   pl.BlockSpec((None, 2 * N_ML, tm), lambda i, j: (i, 0, j))),
        compiler_params=_cparams(("parallel", "parallel")),
        name="norm_inproj",
    )(x, mod, w_ml, w_dil, w_sb)


def _mlstm_kernel(u_ref, grow_ref, convw_ref, convb_ref, wq_ref, wk_ref, wv_ref, gbc_ref, gbr_ref,
                  y_ref, prev_ref, c_ref, n_ref, m_ref):
    L = CHUNK

    @pl.when(pl.program_id(1) == 0)
    def _():
        prev_ref[...] = jnp.zeros_like(prev_ref)
        c_ref[...] = jnp.zeros_like(c_ref)
        n_ref[...] = jnp.zeros_like(n_ref)
        m_ref[...] = jnp.zeros_like(m_ref)

    row = _iota((L, D_ML), 0)
    convw = convw_ref[...]
    ltri = (_iota((L, L), 1) <= _iota((L, L), 0)).astype(bf16)
    utri = (_iota((L, L), 0) <= _iota((L, L), 1)).astype(bf16)
    causal = _iota((L, L), 1) <= _iota((L, L), 0)
    lane = _iota((1, D_ML), 1)
    heads = range(N_ML)
    mhs = [(lane >= hd * HEAD_DIM) & (lane < (hd + 1) * HEAD_DIM) for hd in heads]
    same_head = (_iota((D_ML, D_ML), 0) // HEAD_DIM) == (_iota((D_ML, D_ML), 1) // HEAD_DIM)

    def logsig(t):
        return jnp.minimum(t, 0.0) - jnp.log1p(jnp.exp(-jnp.abs(t)))

    def chain(bb):
        u = u_ref[bb]
        xm = u[:, :D_ML]
        og = u[:, D_ML:2 * D_ML]
        gcol = u[:, 2 * D_ML:] + gbc_ref[...]
        grow = grow_ref[bb] + gbr_ref[...]
        prev = prev_ref[bb]
        xc = xm * convw[CONV_WIDTH - 1:CONV_WIDTH, :]
        for sft in range(1, CONV_WIDTH):
            shifted = jnp.where(row < sft, pltpu.roll(prev, sft, 0), pltpu.roll(xm, sft, 0))
            xc = xc + shifted * convw[CONV_WIDTH - 1 - sft:CONV_WIDTH - sft, :]
        prev_ref[bb] = xm
        xc = xc + convb_ref[...]
        xc = xc * lax.logistic(xc)
        xcb = xc.astype(bf16)
        wide = lambda col: jnp.broadcast_to(col, (L, LANES))
        twice = lambda a: jnp.concatenate([a, a], axis=1)
        lf_wide = [wide(logsig(gcol[:, N_ML + hd:N_ML + hd + 1])) for hd in heads]
        ig_wide = [wide(gcol[:, hd:hd + 1]) for hd in heads]
        lf_split = [_split_bf16(x) for x in lf_wide]
        rh, rl = _split_bf16(logsig(grow))
        yield
        q = _dot(xcb, wq_ref[...])
        k = _dot(xcb, wk_ref[...])
        v = _dot(xm.astype(bf16), wv_ref[...])
        cum_wide = [_dot(ltri, hi) + _dot(ltri, lo) for hi, lo in lf_split]
        cum_row = _dot(rh, utri) + _dot(rl, utri)
        kb = k.astype(bf16)
        vb = v.astype(bf16)
        c_bd = c_ref[bb]
        n_st = n_ref[bb]
        m_st = m_ref[bb]
        c_bdb = c_bd.astype(bf16)
        qhs = [jnp.where(mhs[hd], q, 0.0) for hd in heads]
        qhb = [qh.astype(bf16) for qh in qhs]
        qn_split = [_split_bf16(qh * n_st) for qh in qhs]
        ones_ml = jnp.ones((D_ML, LANES), bf16)
        ones_l = jnp.ones((L, LANES), bf16)
        yield
        qk = [_dot_nt(qhb[hd], kb) for hd in heads]
        qc = [_dot(qhb[hd], c_bdb) for hd in heads]
        qn = [_dot(hi, ones_ml) + _dot(lo, ones_ml) for hi, lo in qn_split]
        yield
        m_ts, inters, sws = [], [], []
        wfull = jnp.zeros((L, D_ML), f32)
        decay_l = jnp.zeros((1, D_ML), f32)
        mnew_l = jnp.zeros((1, D_ML), f32)
        for hd in heads:
            cumc = cum_wide[hd]
            cumr = cum_row[N_ML + hd:N_ML + hd + 1, :]
            igr = grow[hd:hd + 1, :]
            m_prev = m_st[:, hd * HEAD_DIM:hd * HEAD_DIM + 1]
            log_d = jnp.where(causal, cumc - cumr + igr, -jnp.inf)
            log_inter = cumc + m_prev
            m_t = jnp.maximum(log_inter, jnp.max(log_d, axis=-1, keepdims=True))
            m_ts.append(m_t)
            inters.append(jnp.exp(log_inter - m_t))
            sws.append(qk[hd] * jnp.exp(log_d - m_t))
            last = cumc[L - 1:L, :]
            log_w = last - cumc + ig_wide[hd]
            m_new = jnp.maximum(last + m_prev, jnp.max(log_w, axis=0, keepdims=True))
            wfull = jnp.where(mhs[hd], twice(jnp.exp(log_w - m_new)), wfull)
            decay_l = jnp.where(mhs[hd], twice(jnp.exp(last + m_prev - m_new)), decay_l)
            mnew_l = jnp.where(mhs[hd], twice(m_new), mnew_l)
        kw = k * wfull
        sw_split = [_split_bf16(sw) for sw in sws]
        kw16 = kw.astype(bf16)
        yield
        swv = [_dot(sw_split[hd][0], vb) for hd in heads]
        swsum = [_dot(hi, ones_l) + _dot(lo, ones_l) for hi, lo in sw_split]
        upd = lax.dot_general(kw16, vb, _TN, preferred_element_type=f32)
        yield
        hmix = jnp.zeros((L, D_ML), f32)
        for hd in heads:
            num = twice(inters[hd]) * qc[hd] + swv[hd]
            den = inters[hd] * qn[hd] + swsum[hd]
            hout = num / twice(jnp.maximum(jnp.abs(den), jnp.exp(-m_ts[hd])))
            hmix = jnp.where(mhs[hd], hout, hmix)
        c_ref[bb] = decay_l * c_bd + jnp.where(same_head, upd, 0.0)
        n_ref[bb] = decay_l * n_st + jnp.sum(kw, axis=0, keepdims=True)
        m_ref[bb] = mnew_l
        y_ref[bb] = lax.logistic(og) * hmix
        yield

    chains = [chain(bb) for bb in range(u_ref.shape[0])]
    for _ in range(MLSTM_STAGES):
        for ch_ in chains:
            next(ch_)


def _mlstm(u_ml, g_row, conv_w, conv_b, wq_bd, wk_bd, wv_bd, gb_col, gb_row):
    b, s, _ = u_ml.shape
    nb = MLSTM_ROWS if b % MLSTM_ROWS == 0 else 1
    full = lambda a: pl.BlockSpec(a.shape, lambda i, j: (0,) * a.ndim)
    return pl.pallas_call(
        _mlstm_kernel,
        out_shape=jax.ShapeDtypeStruct((b, s, D_ML), f32),
        grid=(b // nb, s // CHUNK),
        in_specs=[pl.BlockSpec((nb, CHUNK, ML_COLS), lambda i, j: (i, j, 0)),
                  pl.BlockSpec((nb, 2 * N_ML, CHUNK), lambda i, j: (i, 0, j)),
                  full(conv_w), full(conv_b), full(wq_bd), full(wk_bd), full(wv_bd), full(gb_col), full(gb_row)],
        out_specs=pl.BlockSpec((nb, CHUNK, D_ML), lambda i, j: (i, j, 0)),
        scratch_shapes=[pltpu.VMEM((nb, CHUNK, D_ML), f32), pltpu.VMEM((nb, D_ML, D_ML), f32),
                        pltpu.VMEM((nb, 1, D_ML), f32), pltpu.VMEM((nb, 1, D_ML), f32)],
        compiler_params=_cparams(("parallel", "arbitrary")),
        name="mlstm",
    )(u_ml, g_row, conv_w, conv_b, wq_bd, wk_bd, wv_bd, gb_col, gb_row)


def _rows(start, size, stride):
    return pl.ds(start, size) if stride == 1 else pl.ds(start, size, stride=stride)


def _dil_kernel(q_ref, k_ref, v_ref, slope_ref, o_ref, *scratch, seq, tile):
    npat = len(DILATIONS)
    kd = scratch[0:npat]
    vd = scratch[npat:2 * npat]
    op = scratch[2 * npat:3 * npat]
    ls = scratch[3 * npat:4 * npat]
    bias_ref = scratch[4 * npat]
    L = CHUNK
    ti = pl.program_id(2)
    lane = _iota((1, LANES), 1)
    m0 = lane < HEAD_DIM

    @pl.when(ti == 0)
    def _():
        row = _iota((L, L), 0)
        col = _iota((L, L), 1)
        for p, d in enumerate(DILATIONS):
            per = seq // d
            for r in range(d):
                kd[p][r * per:(r + 1) * per, :] = k_ref[_rows(r, per, d), :].astype(bf16)
                vd[p][r * per:(r + 1) * per, :] = v_ref[_rows(r, per, d), :].astype(bf16)
            for j in range(2):
                sl = slope_ref[j][0:1, :] * float(d)
                bias_ref[p, j, 0] = jnp.where(col >= row, -(sl * (row - col + DIL_SPAN).astype(f32)), -jnp.inf)
                bias_ref[p, j, 1] = jnp.full((L, L), -jnp.inf, f32)
                bias_ref[p, j, 2] = jnp.where(col <= row, -(sl * (row - col).astype(f32)), -jnp.inf)

    nunits = tile // L
    for p, d in enumerate(DILATIONS):
        per = seq // d
        blocks_per_res = tile // d // L

        def body(it, carry, p=p, d=d, per=per, blocks_per_res=blocks_per_res):
            units = []
            for c in range(DIL_UNITS):
                u = it * DIL_UNITS + c
                r = u // blocks_per_res
                blk = u % blocks_per_res
                start = blk * (L * d) + r
                q = q_ref[_rows(start, L, d), :].astype(bf16)
                m_first = (ti * tile) // d + blk * L
                base = r * per + m_first
                has_prev = m_first > 0
                cur0 = pl.multiple_of(base, L)
                prev0 = pl.multiple_of(jnp.where(has_prev, base - L, base), L)
                units.append(dict(start=start, q=q, pk=jnp.where(has_prev, 0, 1),
                                  kc=kd[p][pl.ds(cur0, L), :], vc=vd[p][pl.ds(cur0, L), :],
                                  kp=kd[p][pl.ds(prev0, L), :], vp=vd[p][pl.ds(prev0, L), :]))
            chains = [(un, j) for un in units for j in range(2)]
            scores = []
            for un, j in chains:
                qj = jnp.where(m0 if j == 0 else jnp.logical_not(m0), un["q"], jnp.zeros_like(un["q"]))
                scores.append((_dot_nt(qj, un["kp"]) + bias_ref[p, j, un["pk"]],
                               _dot_nt(qj, un["kc"]) + bias_ref[p, j, 2]))
            probs = []
            for sp, sc in scores:
                mx = jnp.max(jnp.maximum(sp, sc), axis=-1, keepdims=True)
                pp = jnp.exp(sp - mx)
                pc = jnp.exp(sc - mx)
                den = jnp.sum(pp + pc, axis=-1, keepdims=True)
                probs.append((pp.astype(bf16), pc.astype(bf16), den, mx + jnp.log(den)))
            outs = [(_dot(pp, un["vp"]) + _dot(pc, un["vc"])) / den
                    for (un, j), (pp, pc, den, _) in zip(chains, probs)]
            for c, un in enumerate(units):
                op[p][_rows(un["start"], L, d), :] = jnp.where(m0, outs[2 * c], outs[2 * c + 1])
                ls[p][_rows(un["start"], L, d), :] = jnp.where(m0, probs[2 * c][3], probs[2 * c + 1][3])
            return carry

        lax.fori_loop(0, nunits // DIL_UNITS, body, 0)

    l0, l1, l2 = ls[0][...], ls[1][...], ls[2][...]
    mx = jnp.maximum(jnp.maximum(l0, l1), l2)
    e0, e1, e2 = jnp.exp(l0 - mx), jnp.exp(l1 - mx), jnp.exp(l2 - mx)
    o_ref[...] = ((e0 * op[0][...] + e1 * op[1][...] + e2 * op[2][...]) / (e0 + e1 + e2)).astype(o_ref.dtype)


def _dilated(u_dil, slopes):
    b, s, _ = u_dil.shape
    npair = N_DIL // 2
    tile = DILATIONS[-1] * CHUNK
    assert s % tile == 0 and (tile // CHUNK) % DIL_UNITS == 0
    seq_buf = lambda dt: pltpu.VMEM((s, LANES), dt)
    tile_buf = pltpu.VMEM((tile, LANES), f32)
    npat = len(DILATIONS)
    return pl.pallas_call(
        functools.partial(_dil_kernel, seq=s, tile=tile),
        out_shape=jax.ShapeDtypeStruct((b, s, D_DIL), bf16),
        grid=(b, npair, s // tile),
        in_specs=[pl.BlockSpec((None, tile, LANES), lambda i, h, n: (i, n, h)),
                  pl.BlockSpec((None, s, LANES), lambda i, h, n: (i, 0, npair + h)),
                  pl.BlockSpec((None, s, LANES), lambda i, h, n: (i, 0, 2 * npair + h)),
                  pl.BlockSpec((None, 2, 8, LANES), lambda i, h, n: (h, 0, 0, 0))],
        out_specs=pl.BlockSpec((None, tile, LANES), lambda i, h, n: (i, n, h)),
        scratch_shapes=([seq_buf(bf16)] * (2 * npat) + [tile_buf] * (2 * npat)
                        + [pltpu.VMEM((npat, 2, 3, CHUNK, CHUNK), f32)]),
        compiler_params=_cparams(("parallel", "parallel", "arbitrary")),
        name="dilated",
    )(u_dil, u_dil, u_dil, slopes)


def _sb_kernel(q_ref, k_ref, v_ref, o_ref, qm_ref, r_ref, acc_ref, *, nq):
    L = CHUNK
    tile0 = pl.program_id(1) * nq
    row = _iota((L, L), 0)
    col = _iota((L, L), 1)
    strict = col < row
    lane = _iota((1, LANES), 1)
    m0 = lane < HEAD_DIM
    npair = N_SB // 2
    csum = jnp.concatenate([(_iota((L, L), 0) > _iota((L, L), 1)).astype(bf16), jnp.ones((L, L), bf16)], axis=1)

    for qb in range(nq):
        for hp in range(npair):
            q = q_ref[qb * L:(qb + 1) * L, hp * LANES:(hp + 1) * LANES]
            qm_ref[(qb * npair + hp) * 2] = jnp.where(m0, q, jnp.zeros_like(q))
            qm_ref[(qb * npair + hp) * 2 + 1] = jnp.where(m0, jnp.zeros_like(q), q)

    def sweep(it, diagonal):
        rmax = None
        for g0 in range(0, nq, SB_GROUP):
            chains = []
            for qb in range(g0, g0 + SB_GROUP):
                qi = tile0 + qb
                k0 = pl.multiple_of(jnp.maximum(qi - it, 0) * L, L)
                out_of_keys = jnp.where(qi - it <= 0, SB_NO_KEYS, 0.0)
                for hp in range(npair):
                    kblk = k_ref[pl.ds(k0, L), hp * LANES:(hp + 1) * LANES]
                    vblk = v_ref[pl.ds(k0, L), hp * LANES:(hp + 1) * LANES]
                    for j in range(2):
                        chains.append(((qb * npair + hp) * 2 + j, kblk, vblk, out_of_keys))
            zs = [_dot_nt(qm_ref[c], kblk) for c, kblk, _, _ in chains]
            zl, lkb = [], []
            for z in zs:
                ls = jnp.minimum(z, 0.0) - jnp.log(1.0 + jnp.exp(-jnp.abs(z)))
                lk = ls - z
                zl.append(ls)
                lkb.append((jnp.where(strict, lk, 0.0) if diagonal else lk).astype(bf16))
            ts = [_dot(x, csum) for x in lkb]
            avs = []
            for (c, _, _, out_of_keys), s, t in zip(chains, zl, ts):
                if diagonal:
                    a = jnp.where(strict, jnp.exp(s + t[:, :L]), 0.0)
                    rs = t[:, L:] + out_of_keys
                else:
                    a = jnp.exp(s + r_ref[c] + t[:, :L])
                    rs = r_ref[c] + t[:, L:] + out_of_keys
                avs.append(a.astype(bf16))
                r_ref[c] = rs
                rmax = rs if rmax is None else jnp.maximum(rmax, rs)
            for (c, _, vblk, _), a in zip(chains, avs):
                if diagonal:
                    acc_ref[c] = _dot(a, vblk)
                else:
                    acc_ref[c] += _dot(a, vblk)
        return jnp.max(rmax) > SB_DEAD_LOG

    alive0 = sweep(0, True)

    def body(carry):
        _, it = carry
        return sweep(it, False), it + 1

    lax.while_loop(lambda carry: carry[0], body, (alive0, jnp.int32(1)))

    for qb in range(nq):
        for hp in range(npair):
            c = (qb * npair + hp) * 2
            o_ref[qb * L:(qb + 1) * L, hp * LANES:(hp + 1) * LANES] = (
                jnp.where(m0, acc_ref[c], acc_ref[c + 1]).astype(o_ref.dtype))


def _stickbreak(u_sb):
    b, s, _ = u_sb.shape
    nq = SB_QBLOCKS
    tq = nq * CHUNK
    nchain = nq * N_SB
    return pl.pallas_call(
        functools.partial(_sb_kernel, nq=nq),
        out_shape=jax.ShapeDtypeStruct((b, s, D_SB), bf16),
        grid=(b, s // tq),
        in_specs=[pl.BlockSpec((None, tq, D_SB), lambda i, n: (i, n, 0)),
                  pl.BlockSpec((None, s, D_SB), lambda i, n: (i, 0, 1)),
                  pl.BlockSpec((None, s, D_SB), lambda i, n: (i, 0, 2))],
        out_specs=pl.BlockSpec((None, tq, D_SB), lambda i, n: (i, n, 0)),
        scratch_shapes=[pltpu.VMEM((nchain, CHUNK, LANES), bf16),
                        pltpu.VMEM((nchain, CHUNK, CHUNK), f32),
                        pltpu.VMEM((nchain, CHUNK, LANES), f32)],
        compiler_params=_cparams(("parallel", "arbitrary")),
        name="stickbreak",
    )(u_sb, u_sb, u_sb)


def _post_kernel(x_ref, yml_ref, ydil_ref, ysb_ref, mod_ref, gh_ref, wout_ref, wr_hi_ref, wr_lo_ref, rb_ref,
                 xo_ref, h_ref, selt_ref, w_ref):
    mod = mod_ref[...]
    y = jnp.concatenate([yml_ref[...], ydil_ref[...].astype(f32), ysb_ref[...].astype(f32)], axis=1)
    pair = ((_iota((LANES, LANES), 0) // HEAD_DIM) == (_iota((LANES, LANES), 1) // HEAD_DIM)).astype(bf16)
    cols = []
    for cblk in range(y.shape[1] // LANES):
        yb = y[:, cblk * LANES:(cblk + 1) * LANES]
        hi, lo = _split_bf16(yb * yb)
        ms = (_dot(hi, pair) + _dot(lo, pair)) * (1.0 / HEAD_DIM)
        cols.append(yb * lax.rsqrt(ms + EPS))
    yn = (jnp.concatenate(cols, axis=1) * gh_ref[...]).astype(bf16)
    xn = x_ref[...] + mod[2:3, :] * _dot(yn, wout_ref[...])
    xo_ref[...] = xn
    h2 = _rms(xn) * (1.0 + mod[4:5, :]) + mod[3:4, :]
    hh, hl = _split_bf16(h2)
    h_ref[...] = h2
    tm = x_ref.shape[0]
    row = _iota((SUBLANES, tm), 0)
    grow = row < N_GROUPS
    logits = _dot(hh, wr_hi_ref[...]) + _dot(hl, wr_hi_ref[...]) + _dot(hh, wr_lo_ref[...])
    logits_t = logits.T
    sc, bs = [], []
    for j in range(EPG):
        sc.append(lax.logistic(logits_t[j * SUBLANES:(j + 1) * SUBLANES, :]))
        bs.append(jnp.where(grow, sc[j] + rb_ref[j][:, 0:1], -jnp.inf))
    hi01, lo01 = jnp.maximum(bs[0], bs[1]), jnp.minimum(bs[0], bs[1])
    hi23, lo23 = jnp.maximum(bs[2], bs[3]), jnp.minimum(bs[2], bs[3])
    top1 = jnp.maximum(hi01, hi23)
    top2 = jnp.maximum(jnp.minimum(hi01, hi23), jnp.maximum(lo01, lo23))
    gscore = jnp.where(grow, top1 + top2, -jnp.inf)
    gmax = jnp.max(gscore, axis=0, keepdims=True)
    gidx = jnp.min(jnp.where(gscore == gmax, row, SUBLANES), axis=0, keepdims=True)
    gsel = row == gidx
    picks, vals = [], []
    for j in range(EPG):
        rank = jnp.zeros(bs[j].shape, jnp.int32)
        for jj in range(EPG):
            if jj != j:
                ahead = (bs[jj] >= bs[j]) if jj < j else (bs[jj] > bs[j])
                rank = rank + ahead.astype(jnp.int32)
        picks.append(gsel & (rank < 2))
        vals.append(jnp.sum(jnp.where(gsel, sc[j], 0.0), axis=0, keepdims=True))
    erow = _iota((N_EXPERTS, SUBLANES), 0)
    gcol = _iota((N_EXPERTS, SUBLANES), 1)
    sel_t = jnp.zeros((N_EXPERTS, tm), f32)
    for j in range(EPG):
        place = ((erow == gcol * EPG + j) & (gcol < N_GROUPS)).astype(bf16)
        sel_t = sel_t + _dot(place, picks[j].astype(bf16))
    selt_ref[...] = sel_t.astype(selt_ref.dtype)
    seen = jnp.zeros_like(vals[0])
    w_lo = jnp.zeros_like(vals[0])
    w_hi = jnp.zeros_like(vals[0])
    for j in range(EPG):
        picked = jnp.sum(picks[j].astype(f32), axis=0, keepdims=True) > 0.0
        w_lo = w_lo + jnp.where(picked & (seen == 0.0), vals[j], 0.0)
        w_hi = w_hi + jnp.where(picked & (seen == 1.0), vals[j], 0.0)
        seen = seen + picked.astype(f32)
    tot = w_lo + w_hi
    w_ref[...] = jnp.where(row == 0, w_lo / tot, jnp.where(row == 1, w_hi / tot, 0.0))


def _post(x, y_ml, y_dil, y_sb, mod, g_head, w_out, wr_hi, wr_lo, rb, tm):
    b, s, d = x.shape
    tok = lambda w: pl.BlockSpec((None, tm, w), lambda i, j: (i, j, 0))
    full = lambda a: pl.BlockSpec(a.shape, lambda i, j: (0,) * a.ndim)
    tok_lanes = lambda rows: pl.BlockSpec((rows, tm), lambda i, j: (0, i * (s // tm) + j))
    return pl.pallas_call(
        _post_kernel,
        out_shape=(jax.ShapeDtypeStruct((b, s, d), f32),
                   jax.ShapeDtypeStruct((b, s, d), f32),
                   jax.ShapeDtypeStruct((N_EXPERTS, b * s), bf16),
                   jax.ShapeDtypeStruct((SUBLANES, b * s), f32)),
        grid=(b, s // tm),
        in_specs=[tok(d), tok(D_ML), tok(D_DIL), tok(D_SB), pl.BlockSpec((None, 6, d), lambda i, j: (i, 0, 0)),
                  full(g_head), full(w_out), full(wr_hi), full(wr_lo), full(rb)],
        out_specs=(tok(d), tok(d), tok_lanes(N_EXPERTS), tok_lanes(SUBLANES)),
        compiler_params=_cparams(("parallel", "parallel")),
        name="merge_outproj_router",
    )(x, y_ml, y_dil, y_sb, mod, g_head, w_out, wr_hi, wr_lo, rb)


def _plan_kernel(selt_ref, lpos0_ref, lpos1_ref, base_ref, nchunk_ref, lofs_ref, tiles_ref, tails_ref,
                 before_ref, cnt_ref, lofs_sc, *, ntw, ntile_pad, nrow_tiles):
    phase = pl.program_id(0)
    step = pl.program_id(1)
    tk = MOE_TOK
    sel = selt_ref[...]
    tlane = _iota((1, ntw), 1)

    @pl.when((phase == 0) & (step == 0))
    def _():
        cnt_ref[...] = jnp.zeros_like(cnt_ref)
        before_ref[...] = (_iota((tk, tk), 0) < _iota((tk, tk), 1)).astype(bf16)

    @pl.when(phase == 0)
    def _():
        cnt = cnt_ref[...]
        for j in range(PLAN_GROUP):
            here = _dot(sel[:, j * tk:(j + 1) * tk], jnp.ones((tk, ntw), bf16))
            cnt = jnp.where(tlane == step * PLAN_GROUP + j, here, cnt)
        cnt_ref[...] = cnt
        lpos0_ref[...] = jnp.zeros_like(lpos0_ref)
        lpos1_ref[...] = jnp.zeros_like(lpos1_ref)

    @pl.when((phase == 1) & (step == 0))
    def _():
        cnt = cnt_ref[...]
        nch = jnp.floor((cnt + (MOE_CHUNK - 1.0)) * (1.0 / MOE_CHUNK))
        nchb = nch.astype(bf16)
        earlier_tiles = (_iota((ntw, ntw), 0) < _iota((ntw, ntw), 1)).astype(bf16)
        earlier_experts = (_iota((N_EXPERTS, N_EXPERTS), 1) < _iota((N_EXPERTS, N_EXPERTS), 0)).astype(bf16)
        in_expert = _dot(nchb, earlier_tiles) * MOE_CHUNK
        lofs = _dot(earlier_experts, nchb) * MOE_CHUNK
        total = _dot(nchb, jnp.ones((ntw, LANES), bf16)) * MOE_CHUNK
        padded = jnp.floor((total + (MOE_ROWS - 1.0)) * (1.0 / MOE_ROWS)) * MOE_ROWS
        rows, acc = [], jnp.zeros((1, LANES), f32)
        for e in range(N_EXPERTS):
            rows.append(acc)
            acc = acc + padded[e:e + 1, :]
        off = jnp.concatenate(rows, axis=0)
        base_ref[...] = (off[:, 0:1] + in_expert).astype(jnp.int32)
        nchunk_ref[...] = nch.astype(jnp.int32)
        lofs_ref[...] = lofs.astype(jnp.int32)
        lofs_sc[...] = lofs
        lane = _iota((1, LANES), 1)
        tails_ref[...] = jnp.where(
            lane == 0, off + total,
            jnp.where(lane == 1, (padded - total) * (1.0 / MOE_CHUNK),
                      jnp.where(lane == 2, acc, jnp.where(lane == 3, nrow_tiles - acc * (1.0 / MOE_ROWS), 0.0)))
        ).astype(jnp.int32)
        start = _iota((1, ntile_pad), 1).astype(f32) * MOE_ROWS
        ends = (off + padded)[:, 0:1]
        owner = jnp.sum((start >= ends).astype(f32), axis=0, keepdims=True)
        tiles_ref[...] = jnp.zeros_like(tiles_ref)
        tiles_ref[0:1, :] = jnp.minimum(owner, N_EXPERTS - 1.0).astype(jnp.int32)
        tiles_ref[1:2, :] = (start < acc[:, 0:1]).astype(jnp.int32)

    @pl.when(phase == 1)
    def _():
        for j in range(PLAN_GROUP):
            cols = slice(j * tk, (j + 1) * tk)
            mine = jnp.sum(jnp.where(tlane == step * PLAN_GROUP + j, lofs_sc[...], 0.0), axis=1, keepdims=True)
            pos = _dot(sel[:, cols], before_ref[...]) + mine
            picked = sel[:, cols] > 0
            lpos0_ref[:, cols] = jnp.min(jnp.where(picked, pos, 1e9), axis=0, keepdims=True).astype(jnp.int32)
            lpos1_ref[:, cols] = jnp.max(jnp.where(picked, pos, -1.0), axis=0, keepdims=True).astype(jnp.int32)


def _plan(sel_t, nrow_tiles):
    ne, t = sel_t.shape
    nt = t // MOE_TOK
    ntw = -(-nt // LANES) * LANES
    ntile_pad = -(-nrow_tiles // LANES) * LANES
    i32 = jnp.int32
    whole = lambda shape: pl.BlockSpec(shape, lambda p, i: (0, 0))
    per_tile = pl.BlockSpec((1, PLAN_GROUP * MOE_TOK), lambda p, i: (0, i * p))
    lpos0, lpos1, base, nchunk, lofs, tiles, tails = pl.pallas_call(
        functools.partial(_plan_kernel, ntw=ntw, ntile_pad=ntile_pad, nrow_tiles=nrow_tiles),
        out_shape=(jax.ShapeDtypeStruct((1, t), i32), jax.ShapeDtypeStruct((1, t), i32),
                   jax.ShapeDtypeStruct((ne, ntw), i32), jax.ShapeDtypeStruct((ne, ntw), i32),
                   jax.ShapeDtypeStruct((ne, ntw), i32), jax.ShapeDtypeStruct((8, ntile_pad), i32),
                   jax.ShapeDtypeStruct((ne, LANES), i32)),
        grid=(2, nt // PLAN_GROUP),
        in_specs=[pl.BlockSpec((ne, PLAN_GROUP * MOE_TOK), lambda p, i: (0, i))],
        out_specs=(per_tile, per_tile, whole((ne, ntw)), whole((ne, ntw)), whole((ne, ntw)),
                   whole((8, ntile_pad)), whole((ne, LANES))),
        scratch_shapes=[pltpu.VMEM((MOE_TOK, MOE_TOK), bf16), pltpu.VMEM((ne, ntw), f32),
                        pltpu.VMEM((ne, ntw), f32)],
        compiler_params=_cparams(("arbitrary", "arbitrary")),
        name="moe_plan",
    )(sel_t)
    flat = lambda a: a[:, :nt].T.reshape(nt * ne)
    return dict(lpos0=lpos0, lpos1=lpos1, base=flat(base), nchunk=flat(nchunk), lofs=flat(lofs),
                owner=tiles[0, :nrow_tiles], used=tiles[1, :nrow_tiles],
                tail_start=tails[:, 0], tail_chunks=tails[:, 1], unused=tails[0, 2:4])


def _segment_chunks(tile, base_ref, nchunk_ref, lofs_ref, make_copy, start):
    for e in range(N_EXPERTS):
        idx = tile * N_EXPERTS + e
        first_global = base_ref[idx]
        first_local = lofs_ref[idx]

        def one(c, carry, e=e, first_global=first_global, first_local=first_local):
            cp = make_copy(pl.multiple_of(first_local + c * MOE_CHUNK, MOE_CHUNK),
                           pl.multiple_of(first_global + c * MOE_CHUNK, MOE_CHUNK), e % 2)
            cp.start(priority=e % 2) if start else cp.wait()
            return carry

        lax.fori_loop(0, nchunk_ref[idx], one, 0)


def _one_hot_rows(lpos_ref):
    return (_iota((MOE_LOCAL, MOE_TOK), 0) == lpos_ref[...]).astype(bf16)


def _dispatch_kernel(base_ref, nchunk_ref, lofs_ref, tstart_ref, tchunks_ref, unused_ref, lpos0_ref, lpos1_ref,
                     h_ref, xs_hbm, loc, zeros, sem):
    tile = pl.program_id(0)
    last = pl.num_programs(0) - 1
    slot = tile % 2
    place = _one_hot_rows(lpos0_ref) + _one_hot_rows(lpos1_ref)
    loc[slot] = _dot(place, h_ref[...].astype(bf16))

    def copier(slot_):
        def seg_copy(local_row, global_row, s):
            return pltpu.make_async_copy(loc.at[slot_, pl.ds(local_row, MOE_CHUNK)],
                                         xs_hbm.at[pl.ds(global_row, MOE_CHUNK)], sem.at[slot_, s])
        return seg_copy

    _segment_chunks(tile, base_ref, nchunk_ref, lofs_ref, copier(slot), True)

    @pl.when(tile > 0)
    def _():
        _segment_chunks(tile - 1, base_ref, nchunk_ref, lofs_ref, copier(1 - slot), False)

    @pl.when(tile == last)
    def _():
        _segment_chunks(tile, base_ref, nchunk_ref, lofs_ref, copier(slot), False)

    @pl.when(tile == last)
    def _():
        zeros[...] = jnp.zeros_like(zeros)
        for start in (True, False):
            for e in range(N_EXPERTS):
                def one(c, carry, e=e, start=start):
                    row = pl.multiple_of(tstart_ref[e] + c * MOE_CHUNK, MOE_CHUNK)
                    cp = pltpu.make_async_copy(zeros.at[pl.ds(0, MOE_CHUNK)], xs_hbm.at[pl.ds(row, MOE_CHUNK)],
                                               sem.at[0, e % 2])
                    cp.start() if start else cp.wait()
                    return carry
                lax.fori_loop(0, tchunks_ref[e], one, 0)

            def whole_tile(c, carry, start=start):
                row = pl.multiple_of(unused_ref[0] + c * MOE_ROWS, MOE_ROWS)
                cp = pltpu.make_async_copy(zeros, xs_hbm.at[pl.ds(row, MOE_ROWS)], sem.at[0, 0])
                cp.start() if start else cp.wait()
                return carry
            lax.fori_loop(0, unused_ref[1], whole_tile, 0)


def _dispatch(h_rows, plan, nrows):
    t, w = h_rows.shape
    per_tile = lambda: pl.BlockSpec((1, MOE_TOK), lambda i, *_: (0, i))
    return pl.pallas_call(
        _dispatch_kernel,
        out_shape=jax.ShapeDtypeStruct((nrows, w), h_rows.dtype),
        grid_spec=pltpu.PrefetchScalarGridSpec(
            num_scalar_prefetch=6,
            grid=(t // MOE_TOK,),
            in_specs=[per_tile(), per_tile(), pl.BlockSpec((MOE_TOK, w), lambda i, *_: (i, 0))],
            out_specs=pl.BlockSpec(memory_space=pl.ANY),
            scratch_shapes=[pltpu.VMEM((2, MOE_LOCAL, w), f32), pltpu.VMEM((MOE_ROWS, w), f32),
                            pltpu.SemaphoreType.DMA((2, 2))]),
        compiler_params=_cparams(("arbitrary",)),
        name="moe_dispatch",
    )(plan["base"], plan["nchunk"], plan["lofs"], plan["tail_start"], plan["tail_chunks"], plan["unused"],
      plan["lpos0"], plan["lpos1"], h_rows)


def _experts_kernel(owner_ref, used_ref, xs_ref, wg_ref, wu_ref, wd_ref, ys_ref):
    tile = pl.program_id(0)

    @pl.when(used_ref[tile] > 0)
    def _():
        xb = xs_ref[...].astype(bf16)
        g = _dot(xb, wg_ref[...].astype(bf16))
        he = g * lax.logistic(g) * _dot(xb, wu_ref[...].astype(bf16))
        ys_ref[...] = _dot(he.astype(bf16), wd_ref[...].astype(bf16))

    @pl.when(used_ref[tile] == 0)
    def _():
        ys_ref[...] = jnp.zeros_like(ys_ref)


def _experts(xs, owner, used, wg, wu, wd, layer):
    nrows, w = xs.shape
    _, _, d, de = wg.shape
    return pl.pallas_call(
        _experts_kernel,
        out_shape=jax.ShapeDtypeStruct((nrows, w), xs.dtype),
        grid_spec=pltpu.PrefetchScalarGridSpec(
            num_scalar_prefetch=2,
            grid=(nrows // MOE_ROWS,),
            in_specs=[pl.BlockSpec((MOE_ROWS, w), lambda g, own, use: (g * use[g], 0)),
                      pl.BlockSpec((None, None, d, de), lambda g, own, use: (layer, own[g], 0, 0)),
                      pl.BlockSpec((None, None, d, de), lambda g, own, use: (layer, own[g], 0, 0)),
                      pl.BlockSpec((None, None, de, d), lambda g, own, use: (layer, own[g], 0, 0))],
            out_specs=pl.BlockSpec((MOE_ROWS, w), lambda g, own, use: (g, 0))),
        compiler_params=_cparams(("arbitrary",)),
        name="moe_experts",
    )(owner, used, xs, wg, wu, wd)


def _combine_kernel(base_ref, nchunk_ref, lofs_ref, lpos0_ref, lpos1_ref, ys_hbm, x_ref, w_ref, mod_ref, gf_ref,
                    o_ref, loc, sem, *, final):
    tile = pl.program_id(0)
    ntile = pl.num_programs(0)
    slot = tile % 2

    def gather(tile_, slot_, start):
        def seg_copy(local_row, global_row, s):
            return pltpu.make_async_copy(ys_hbm.at[pl.ds(global_row, MOE_CHUNK)],
                                         loc.at[slot_, pl.ds(local_row, MOE_CHUNK)], sem.at[slot_, s])
        _segment_chunks(tile_, base_ref, nchunk_ref, lofs_ref, seg_copy, start)

    @pl.when(tile == 0)
    def _():
        loc[...] = jnp.zeros_like(loc)
        gather(0, 0, True)

    @pl.when(tile + 1 < ntile)
    def _():
        gather(tile + 1, 1 - slot, True)

    gather(tile, slot, False)
    ysl = loc[slot].astype(bf16)
    prow = _iota((MOE_LOCAL, MOE_TOK), 0)
    gates = (jnp.where(prow == lpos0_ref[...], w_ref[0:1, :], 0.0)
             + jnp.where(prow == lpos1_ref[...], w_ref[1:2, :], 0.0))
    y = lax.dot_general(gates.astype(bf16), ysl, _TN, preferred_element_type=f32)
    xn = x_ref[...] + mod_ref[...][5:6, :] * y
    if final:
        xn = _rms(xn) * gf_ref[...]
    o_ref[...] = xn


def _combine(ys, plan, x, w_pair, mod, g_final, final):
    b, s, d = x.shape
    t = b * s
    per_batch = s // MOE_TOK
    per_tile = lambda: pl.BlockSpec((1, MOE_TOK), lambda i, *_: (0, i))
    out = pl.pallas_call(
        functools.partial(_combine_kernel, final=final),
        out_shape=jax.ShapeDtypeStruct((t, d), f32),
        grid_spec=pltpu.PrefetchScalarGridSpec(
            num_scalar_prefetch=3,
            grid=(t // MOE_TOK,),
            in_specs=[per_tile(), per_tile(), pl.BlockSpec(memory_space=pl.ANY),
                      pl.BlockSpec((MOE_TOK, d), lambda i, *_: (i, 0)),
                      pl.BlockSpec((SUBLANES, MOE_TOK), lambda i, *_: (0, i)),
                      pl.BlockSpec((None, 6, d), lambda i, *_: (i // per_batch, 0, 0)),
                      pl.BlockSpec((1, d), lambda i, *_: (0, 0))],
            out_specs=pl.BlockSpec((MOE_TOK, d), lambda i, *_: (i, 0)),
            scratch_shapes=[pltpu.VMEM((2, MOE_LOCAL, d), f32), pltpu.SemaphoreType.DMA((2, 2))]),
        compiler_params=_cparams(("arbitrary",)),
        name="moe_combine_final" if final else "moe_combine",
    )(plan["base"], plan["nchunk"], plan["lofs"], plan["lpos0"], plan["lpos1"], ys,
      x.reshape(t, d), w_pair, mod, g_final)
    return out.reshape(b, s, d)


def _moe(x, h_rows, sel_t, w_pair, mod, wg, wu, wd, layer, g_final, final):
    b, s, d = x.shape
    t = b * s
    worst = 2 * t + (t // MOE_TOK) * N_EXPERTS * (MOE_CHUNK - 1) + N_EXPERTS * MOE_ROWS
    nrows = -(-worst // MOE_ROWS) * MOE_ROWS
    plan = _plan(sel_t, nrows // MOE_ROWS)
    xs = _dispatch(h_rows.reshape(t, d), plan, nrows)
    ys = _experts(xs, plan["owner"], plan["used"], wg, wu, wd, layer)
    return _combine(ys, plan, x, w_pair, mod, g_final, final)


def _block_diag(w):
    n, a, c = w.shape
    out = jnp.zeros((n * a, n * c), w.dtype)
    for i in range(n):
        out = out.at[i * a:(i + 1) * a, i * c:(i + 1) * c].set(w[i])
    return out


def kernel(x, c, w_in, conv_w, conv_b, w_mq, w_mk, w_mv, gate_bias, g_head, w_out, w_ada, b_ada,
           w_router, router_bias, w_gate_e, w_up_e, w_down_e, g_final):
    b, s, d = x.shape
    depth = w_in.shape[0]
    qscale = HEAD_DIM ** -0.5
    tm = min(512, s)

    mod_all = _ada(c, w_ada, b_ada).reshape(depth, b, 6, d)

    wr = jnp.zeros((d, LANES // SUBLANES, SUBLANES), f32).at[:, :EPG, :N_GROUPS].set(
        w_router.reshape(d, N_GROUPS, EPG).transpose(0, 2, 1)).reshape(d, LANES)
    wr_hi = wr.astype(bf16)
    wr_lo = (wr - wr_hi.astype(f32)).astype(bf16)
    rb = jnp.zeros((EPG, SUBLANES, LANES), f32).at[:, :N_GROUPS, :].set(
        jnp.broadcast_to(router_bias.reshape(N_GROUPS, EPG).T[:, :, None], (EPG, N_GROUPS, LANES)))

    slope = jnp.exp2(-ALIBI_MAX_BIAS * jnp.arange(1, N_DIL + 1, dtype=f32) / N_DIL)
    slopes = jnp.broadcast_to(slope.reshape(N_DIL // 2, 2, 1, 1), (N_DIL // 2, 2, 8, LANES))
    g_final2 = g_final.reshape(1, d)

    w_ml_all, w_dil_all, w_sb_all = _wprep(w_in, min(256, d))

    for l in range(depth):
        mod = mod_all[l]

        u_ml, u_dil, u_sb, g_row = _inproj(x, mod, w_ml_all[l], w_dil_all[l], w_sb_all[l], tm)

        gb = gate_bias[l]
        gb_col = jnp.zeros((1, LANES), f32).at[0, :2 * N_ML].set(gb.reshape(-1))
        gb_row = gb.reshape(2 * N_ML, 1)
        y_ml = _mlstm(u_ml, g_row, conv_w[l], conv_b[l].reshape(1, D_ML),
                      _block_diag(w_mq[l]).astype(bf16), (_block_diag(w_mk[l]) * qscale).astype(bf16),
                      _block_diag(w_mv[l]).astype(bf16), gb_col, gb_row)
        y_dil = _dilated(u_dil, slopes)
        y_sb = _stickbreak(u_sb)

        x, h_rows, sel_t, w_pair = _post(x, y_ml, y_dil, y_sb, mod, g_head[l].reshape(1, d),
                                           w_out[l].astype(bf16), wr_hi, wr_lo, rb, tm)
        x = _moe(x, h_rows, sel_t, w_pair, mod, w_gate_e, w_up_e, w_down_e, l, g_final2,
                 final=(l == depth - 1))
    return x
```

```python
import functools

import jax
import jax.numpy as jnp
from jax import lax
from jax.experimental import pallas as pl
from jax.experimental.pallas import tpu as pltpu

HEAD_DIM = 64
N_ML = 4
N_DIL = 6
N_SB = 6
D_ML = N_ML * HEAD_DIM
D_DIL = N_DIL * HEAD_DIM
D_SB = N_SB * HEAD_DIM
CONV_WIDTH = 4
CHUNK = 128
DILATIONS = (1, 4, 16)
DIL_SPAN = 128
ALIBI_MAX_BIAS = 8.0
N_EXPERTS = 16
N_GROUPS = 4
EPG = N_EXPERTS // N_GROUPS
EPS = 1e-6
LANES = 128
ML_COLS = 2 * D_ML + LANES
SB_DEAD_LOG = -104.0
SB_NO_KEYS = -1e30
SB_QBLOCKS = 4
SB_GROUP = 2
DIL_UNITS = 8
MOE_ROWS = 512
MOE_TOK = 256
MOE_CHUNK = 8
PLAN_GROUP = 4
SUBLANES = 8
MLSTM_ROWS = 2
MLSTM_STAGES = 6
MOE_LOCAL = 2 * MOE_TOK + N_EXPERTS * MOE_CHUNK
VMEM_LIMIT = 56 * 1024 * 1024

_NT = (((1,), (1,)), ((), ()))
_TN = (((0,), (0,)), ((), ()))

f32 = jnp.float32
bf16 = jnp.bfloat16


def _dot(a, b):
    return jnp.dot(a, b, preferred_element_type=f32)


def _dot_nt(a, b):
    return lax.dot_general(a, b, _NT, preferred_element_type=f32)


def _split_bf16(x):
    hi = x.astype(bf16)
    lo = (x - hi.astype(f32)).astype(bf16)
    return hi, lo


def _iota(shape, dim):
    return lax.broadcasted_iota(jnp.int32, shape, dim)


def _rms(x):
    return x * lax.rsqrt(jnp.mean(x * x, axis=-1, keepdims=True) + EPS)


def _cparams(sem):
    return pltpu.CompilerParams(dimension_semantics=sem, vmem_limit_bytes=VMEM_LIMIT)


def _ada_kernel(c_ref, w_ref, b_ref, o_ref):
    c = c_ref[...]
    c_act = (c * lax.logistic(c)).astype(bf16)
    o_ref[...] = _dot(c_act, w_ref[...].astype(bf16)) + b_ref[...]


def _ada(c, w_ada, b_ada):
    depth, d, nd = w_ada.shape
    b = c.shape[0]
    nblk = nd // d
    return pl.pallas_call(
        _ada_kernel,
        out_shape=jax.ShapeDtypeStruct((depth, b, nd), f32),
        grid=(depth, nblk),
        in_specs=[pl.BlockSpec((b, d), lambda l, j: (0, 0)),
                  pl.BlockSpec((None, d, d), lambda l, j: (l, 0, j)),
                  pl.BlockSpec((None, 1, d), lambda l, j: (l, 0, j))],
        out_specs=pl.BlockSpec((None, b, d), lambda l, j: (l, 0, j)),
        compiler_params=_cparams(("parallel", "parallel")),
        name="ada_mod",
    )(c, w_ada, b_ada.reshape(depth, 1, nd))


def _wprep_kernel(w_ref, wml_ref, wdil_ref, wsb_ref):
    w = w_ref[...]
    off_dil = 2 * D_ML + 2 * N_ML
    off_sb = off_dil + 3 * D_DIL
    qs = HEAD_DIM ** -0.5
    ml = w[:, :ML_COLS]
    wml_ref[...] = jnp.where(_iota((1, ML_COLS), 1) < off_dil, ml, 0.0).astype(bf16)
    q_cols = jnp.where(_iota((1, 3 * D_DIL), 1) < D_DIL, qs, 1.0)
    wdil_ref[...] = (w[:, off_dil:off_sb] * q_cols).astype(bf16)
    wsb_ref[...] = (w[:, off_sb:off_sb + 3 * D_SB] * q_cols).astype(bf16)


def _wprep(w_in, rows):
    depth, d, d_in = w_in.shape
    assert d_in == 2 * D_ML + 2 * N_ML + 3 * D_DIL + 3 * D_SB and D_DIL == D_SB
    out = lambda w: pl.BlockSpec((None, rows, w), lambda l, i: (l, i, 0))
    return pl.pallas_call(
        _wprep_kernel,
        out_shape=(jax.ShapeDtypeStruct((depth, d, ML_COLS), bf16),
                   jax.ShapeDtypeStruct((depth, d, 3 * D_DIL), bf16),
                   jax.ShapeDtypeStruct((depth, d, 3 * D_SB), bf16)),
        grid=(depth, d // rows),
        in_specs=[pl.BlockSpec((None, rows, d_in), lambda l, i: (l, i, 0))],
        out_specs=(out(ML_COLS), out(3 * D_DIL), out(3 * D_SB)),
        compiler_params=_cparams(("parallel", "parallel")),
        name="inproj_weight_layout",
    )(w_in)


def _inproj_kernel(x_ref, mod_ref, wml_ref, wdil_ref, wsb_ref,
                   uml_ref, udil_ref, usb_ref, grow_ref):
    x = x_ref[...]
    mod = mod_ref[...]
    h = (_rms(x) * (1.0 + mod[1:2, :]) + mod[0:1, :]).astype(bf16)
    uml = _dot(h, wml_ref[...])
    uml_ref[...] = uml
    udil_ref[...] = _dot(h, wdil_ref[...])
    usb_ref[...] = _dot(h, wsb_ref[...]).astype(bf16)
    grow_ref[...] = uml[:, 2 * D_ML:].T[:2 * N_ML, :]


def _inproj(x, mod, w_ml, w_dil, w_sb, tm):
    b, s, d = x.shape
    full = lambda a: pl.BlockSpec(a.shape, lambda i, j: (0,) * a.ndim)
    return pl.pallas_call(
        _inproj_kernel,
        out_shape=(jax.ShapeDtypeStruct((b, s, ML_COLS), f32),
                   jax.ShapeDtypeStruct((b, s, 3 * D_DIL), f32),
                   jax.ShapeDtypeStruct((b, s, 3 * D_SB), bf16),
                   jax.ShapeDtypeStruct((b, 2 * N_ML, s), f32)),
        grid=(b, s // tm),
        in_specs=[pl.BlockSpec((None, tm, d), lambda i, j: (i, j, 0)),
                  pl.BlockSpec((None, 6, d), lambda i, j: (i, 0, 0)),
                  full(w_ml), full(w_dil), full(w_sb)],
        out_specs=(pl.BlockSpec((None, tm, ML_COLS), lambda i, j: (i, j, 0)),
                   pl.BlockSpec((None, tm, 3 * D_DIL), lambda i, j: (i, j, 0)),
                   pl.BlockSpec((None, tm, 3 * D_SB), lambda i, j: (i, j, 0)),
                   pl.BlockSpec((None, 2 * N_ML, tm), lambda i, j: (i, 0, j))),
        compiler_params=_cparams(("parallel", "parallel")),
        name="norm_inproj",
    )(x, mod, w_ml, w_dil, w_sb)


def _mlstm_kernel(u_ref, grow_ref, convw_ref, convb_ref, wq_ref, wk_ref, wv_ref, gbc_ref, gbr_ref,
                  y_ref, prev_ref, c_ref, n_ref, m_ref):
    L = CHUNK

    @pl.when(pl.program_id(1) == 0)
    def _():
        prev_ref[...] = jnp.zeros_like(prev_ref)
        c_ref[...] = jnp.zeros_like(c_ref)
        n_ref[...] = jnp.zeros_like(n_ref)
        m_ref[...] = jnp.zeros_like(m_ref)

    row = _iota((L, D_ML), 0)
    convw = convw_ref[...]
    ltri = (_iota((L, L), 1) <= _iota((L, L), 0)).astype(bf16)
    utri = (_iota((L, L), 0) <= _iota((L, L), 1)).astype(bf16)
    causal = _iota((L, L), 1) <= _iota((L, L), 0)
    lane = _iota((1, D_ML), 1)
    heads = range(N_ML)
    mhs = [(lane >= hd * HEAD_DIM) & (lane < (hd + 1) * HEAD_DIM) for hd in heads]
    same_head = (_iota((D_ML, D_ML), 0) // HEAD_DIM) == (_iota((D_ML, D_ML), 1) // HEAD_DIM)

    def logsig(t):
        return jnp.minimum(t, 0.0) - jnp.log1p(jnp.exp(-jnp.abs(t)))

    def chain(bb):
        u = u_ref[bb]
        xm = u[:, :D_ML]
        og = u[:, D_ML:2 * D_ML]
        gcol = u[:, 2 * D_ML:] + gbc_ref[...]
        grow = grow_ref[bb] + gbr_ref[...]
        prev = prev_ref[bb]
        xc = xm * convw[CONV_WIDTH - 1:CONV_WIDTH, :]
        for sft in range(1, CONV_WIDTH):
            shifted = jnp.where(row < sft, pltpu.roll(prev, sft, 0), pltpu.roll(xm, sft, 0))
            xc = xc + shifted * convw[CONV_WIDTH - 1 - sft:CONV_WIDTH - sft, :]
        prev_ref[bb] = xm
        xc = xc + convb_ref[...]
        xc = xc * lax.logistic(xc)
        xcb = xc.astype(bf16)
        wide = lambda col: jnp.broadcast_to(col, (L, LANES))
        twice = lambda a: jnp.concatenate([a, a], axis=1)
        lf_wide = [wide(logsig(gcol[:, N_ML + hd:N_ML + hd + 1])) for hd in heads]
        ig_wide = [wide(gcol[:, hd:hd + 1]) for hd in heads]
        lf_split = [_split_bf16(x) for x in lf_wide]
        rh, rl = _split_bf16(logsig(grow))
        yield
        q = _dot(xcb, wq_ref[...])
        k = _dot(xcb, wk_ref[...])
        v = _dot(xm.astype(bf16), wv_ref[...])
        cum_wide = [_dot(ltri, hi) + _dot(ltri, lo) for hi, lo in lf_split]
        cum_row = _dot(rh, utri) + _dot(rl, utri)
        kb = k.astype(bf16)
        vb = v.astype(bf16)
        c_bd = c_ref[bb]
        n_st = n_ref[bb]
        m_st = m_ref[bb]
        c_bdb = c_bd.astype(bf16)
        qhs = [jnp.where(mhs[hd], q, 0.0) for hd in heads]
        qhb = [qh.astype(bf16) for qh in qhs]
        qn_split = [_split_bf16(qh * n_st) for qh in qhs]
        ones_ml = jnp.ones((D_ML, LANES), bf16)
        ones_l = jnp.ones((L, LANES), bf16)
        yield
        qk = [_dot_nt(qhb[hd], kb) for hd in heads]
        qc = [_dot(qhb[hd], c_bdb) for hd in heads]
        qn = [_dot(hi, ones_ml) + _dot(lo, ones_ml) for hi, lo in qn_split]
        yield
        m_ts, inters, sws = [], [], []
        wfull = jnp.zeros((L, D_ML), f32)
        decay_l = jnp.zeros((1, D_ML), f32)
        mnew_l = jnp.zeros((1, D_ML), f32)
        for hd in heads:
            cumc = cum_wide[hd]
            cumr = cum_row[N_ML + hd:N_ML + hd + 1, :]
            igr = grow[hd:hd + 1, :]
            m_prev = m_st[:, hd * HEAD_DIM:hd * HEAD_DIM + 1]
            log_d = jnp.where(causal, cumc - cumr + igr, -jnp.inf)
            log_inter = cumc + m_prev
            m_t = jnp.maximum(log_inter, jnp.max(log_d, axis=-1, keepdims=True))
            m_ts.append(m_t)
            inters.append(jnp.exp(log_inter - m_t))
            sws.append(qk[hd] * jnp.exp(log_d - m_t))
            last = cumc[L - 1:L, :]
            log_w = last - cumc + ig_wide[hd]
            m_new = jnp.maximum(last + m_prev, jnp.max(log_w, axis=0, keepdims=True))
            wfull = jnp.where(mhs[hd], twice(jnp.exp(log_w - m_new)), wfull)
            decay_l = jnp.where(mhs[hd], twice(jnp.exp(last + m_prev - m_new)), decay_l)
            mnew_l = jnp.where(mhs[hd], twice(m_new), mnew_l)
        kw = k * wfull
        sw_split = [_split_bf16(sw) for sw in sws]
        kw16 = kw.astype(bf16)
        yield
        swv = [_dot(sw_split[hd][0], vb) for hd in heads]
        swsum = [_dot(hi, ones_l) + _dot(lo, ones_l) for hi, lo in sw_split]
        upd = lax.dot_general(kw16, vb, _TN, preferred_element_type=f32)
        yield
        hmix = jnp.zeros((L, D_ML), f32)
        for hd in heads:
            num = twice(inters[hd]) * qc[hd] + swv[hd]
            den = inters[hd] * qn[hd] + swsum[hd]
            hout = num / twice(jnp.maximum(jnp.abs(den), jnp.exp(-m_ts[hd])))
            hmix = jnp.where(mhs[hd], hout, hmix)
        c_ref[bb] = decay_l * c_bd + jnp.where(same_head, upd, 0.0)
        n_ref[bb] = decay_l * n_st + jnp.sum(kw, axis=0, keepdims=True)
        m_ref[bb] = mnew_l
        y_ref[bb] = lax.logistic(og) * hmix
        yield

    chains = [chain(bb) for bb in range(u_ref.shape[0])]
    for _ in range(MLSTM_STAGES):
        for ch_ in chains:
            next(ch_)


def _mlstm(u_ml, g_row, conv_w, conv_b, wq_bd, wk_bd, wv_bd, gb_col, gb_row):
    b, s, _ = u_ml.shape
    nb = MLSTM_ROWS if b % MLSTM_ROWS == 0 else 1
    full = lambda a: pl.BlockSpec(a.shape, lambda i, j: (0,) * a.ndim)
    return pl.pallas_call(
        _mlstm_kernel,
        out_shape=jax.ShapeDtypeStruct((b, s, D_ML), f32),
        grid=(b // nb, s // CHUNK),
        in_specs=[pl.BlockSpec((nb, CHUNK, ML_COLS), lambda i, j: (i, j, 0)),
                  pl.BlockSpec((nb, 2 * N_ML, CHUNK), lambda i, j: (i, 0, j)),
                  full(conv_w), full(conv_b), full(wq_bd), full(wk_bd), full(wv_bd), full(gb_col), full(gb_row)],
        out_specs=pl.BlockSpec((nb, CHUNK, D_ML), lambda i, j: (i, j, 0)),
        scratch_shapes=[pltpu.VMEM((nb, CHUNK, D_ML), f32), pltpu.VMEM((nb, D_ML, D_ML), f32),
                        pltpu.VMEM((nb, 1, D_ML), f32), pltpu.VMEM((nb, 1, D_ML), f32)],
        compiler_params=_cparams(("parallel", "arbitrary")),
        name="mlstm",
    )(u_ml, g_row, conv_w, conv_b, wq_bd, wk_bd, wv_bd, gb_col, gb_row)


def _rows(start, size, stride):
    return pl.ds(start, size) if stride == 1 else pl.ds(start, size, stride=stride)


def _dil_kernel(q_ref, k_ref, v_ref, slope_ref, o_ref, *scratch, seq, tile):
    npat = len(DILATIONS)
    kd = scratch[0:npat]
    vd = scratch[npat:2 * npat]
    op = scratch[2 * npat:3 * npat]
    ls = scratch[3 * npat:4 * npat]
    bias_ref = scratch[4 * npat]
    L = CHUNK
    ti = pl.program_id(2)
    lane = _iota((1, LANES), 1)
    m0 = lane < HEAD_DIM

    @pl.when(ti == 0)
    def _():
        row = _iota((L, L), 0)
        col = _iota((L, L), 1)
        for p, d in enumerate(DILATIONS):
            per = seq // d
            for r in range(d):
                kd[p][r * per:(r + 1) * per, :] = k_ref[_rows(r, per, d), :].astype(bf16)
                vd[p][r * per:(r + 1) * per, :] = v_ref[_rows(r, per, d), :].astype(bf16)
            for j in range(2):
                sl = slope_ref[j][0:1, :] * float(d)
                bias_ref[p, j, 0] = jnp.where(col >= row, -(sl * (row - col + DIL_SPAN).astype(f32)), -jnp.inf)
                bias_ref[p, j, 1] = jnp.full((L, L), -jnp.inf, f32)
                bias_ref[p, j, 2] = jnp.where(col <= row, -(sl * (row - col).astype(f32)), -jnp.inf)

    nunits = tile // L
    for p, d in enumerate(DILATIONS):
        per = seq // d
        blocks_per_res = tile // d // L

        def body(it, carry, p=p, d=d, per=per, blocks_per_res=blocks_per_res):
            units = []
            for c in range(DIL_UNITS):
                u = it * DIL_UNITS + c
                r = u // blocks_per_res
                blk = u % blocks_per_res
                start = blk * (L * d) + r
                q = q_ref[_rows(start, L, d), :].astype(bf16)
                m_first = (ti * tile) // d + blk * L
                base = r * per + m_first
                has_prev = m_first > 0
                cur0 = pl.multiple_of(base, L)
                prev0 = pl.multiple_of(jnp.where(has_prev, base - L, base), L)
                units.append(dict(start=start, q=q, pk=jnp.where(has_prev, 0, 1),
                                  kc=kd[p][pl.ds(cur0, L), :], vc=vd[p][pl.ds(cur0, L), :],
                                  kp=kd[p][pl.ds(prev0, L), :], vp=vd[p][pl.ds(prev0, L), :]))
            chains = [(un, j) for un in units for j in range(2)]
            scores = []
            for un, j in chains:
                qj = jnp.where(m0 if j == 0 else jnp.logical_not(m0), un["q"], jnp.zeros_like(un["q"]))
                scores.append((_dot_nt(qj, un["kp"]) + bias_ref[p, j, un["pk"]],
                               _dot_nt(qj, un["kc"]) + bias_ref[p, j, 2]))
            probs = []
            for sp, sc in scores:
                mx = jnp.max(jnp.maximum(sp, sc), axis=-1, keepdims=True)
                pp = jnp.exp(sp - mx)
                pc = jnp.exp(sc - mx)
                den = jnp.sum(pp + pc, axis=-1, keepdims=True)
                probs.append((pp.astype(bf16), pc.astype(bf16), den, mx + jnp.log(den)))
            outs = [(_dot(pp, un["vp"]) + _dot(pc, un["vc"])) / den
                    for (un, j), (pp, pc, den, _) in zip(chains, probs)]
            for c, un in enumerate(units):
                op[p][_rows(un["start"], L, d), :] = jnp.where(m0, outs[2 * c], outs[2 * c + 1])
                ls[p][_rows(un["start"], L, d), :] = jnp.where(m0, probs[2 * c][3], probs[2 * c + 1][3])
            return carry

        lax.fori_loop(0, nunits // DIL_UNITS, body, 0)

    l0, l1, l2 = ls[0][...], ls[1][...], ls[2][...]
    mx = jnp.maximum(jnp.maximum(l0, l1), l2)
    e0, e1, e2 = jnp.exp(l0 - mx), jnp.exp(l1 - mx), jnp.exp(l2 - mx)
    o_ref[...] = ((e0 * op[0][...] + e1 * op[1][...] + e2 * op[2][...]) / (e0 + e1 + e2)).astype(o_ref.dtype)


def _dilated(u_dil, slopes):
    b, s, _ = u_dil.shape
    npair = N_DIL // 2
    tile = DILATIONS[-1] * CHUNK
    assert s % tile == 0 and (tile // CHUNK) % DIL_UNITS == 0
    seq_buf = lambda dt: pltpu.VMEM((s, LANES), dt)
    tile_buf = pltpu.VMEM((tile, LANES), f32)
    npat = len(DILATIONS)
    return pl.pallas_call(
        functools.partial(_dil_kernel, seq=s, tile=tile),
        out_shape=jax.ShapeDtypeStruct((b, s, D_DIL), bf16),
        grid=(b, npair, s // tile),
        in_specs=[pl.BlockSpec((None, tile, LANES), lambda i, h, n: (i, n, h)),
                  pl.BlockSpec((None, s, LANES), lambda i, h, n: (i, 0, npair + h)),
                  pl.BlockSpec((None, s, LANES), lambda i, h, n: (i, 0, 2 * npair + h)),
                  pl.BlockSpec((None, 2, 8, LANES), lambda i, h, n: (h, 0, 0, 0))],
        out_specs=pl.BlockSpec((None, tile, LANES), lambda i, h, n: (i, n, h)),
        scratch_shapes=([seq_buf(bf16)] * (2 * npat) + [tile_buf] * (2 * npat)
                        + [pltpu.VMEM((npat, 2, 3, CHUNK, CHUNK), f32)]),
        compiler_params=_cparams(("parallel", "parallel", "arbitrary")),
        name="dilated",
    )(u_dil, u_dil, u_dil, slopes)


def _sb_kernel(q_ref, k_ref, v_ref, o_ref, qm_ref, r_ref, acc_ref, *, nq):
    L = CHUNK
    tile0 = pl.program_id(1) * nq
    row = _iota((L, L), 0)
    col = _iota((L, L), 1)
    strict = col < row
    lane = _iota((1, LANES), 1)
    m0 = lane < HEAD_DIM
    npair = N_SB // 2
    csum = jnp.concatenate([(_iota((L, L), 0) > _iota((L, L), 1)).astype(bf16), jnp.ones((L, L), bf16)], axis=1)

    for qb in range(nq):
        for hp in range(npair):
            q = q_ref[qb * L:(qb + 1) * L, hp * LANES:(hp + 1) * LANES]
            qm_ref[(qb * npair + hp) * 2] = jnp.where(m0, q, jnp.zeros_like(q))
            qm_ref[(qb * npair + hp) * 2 + 1] = jnp.where(m0, jnp.zeros_like(q), q)

    def sweep(it, diagonal, g0):
        rmax = None
        if True:
            chains = []
            for qb in range(g0, g0 + SB_GROUP):
                qi = tile0 + qb
                k0 = pl.multiple_of(jnp.maximum(qi - it, 0) * L, L)
                out_of_keys = jnp.where(qi - it <= 0, SB_NO_KEYS, 0.0)
                for hp in range(npair):
                    kblk = k_ref[pl.ds(k0, L), hp * LANES:(hp + 1) * LANES]
                    vblk = v_ref[pl.ds(k0, L), hp * LANES:(hp + 1) * LANES]
                    for j in range(2):
                        chains.append(((qb * npair + hp) * 2 + j, kblk, vblk, out_of_keys))
            zs = [_dot_nt(qm_ref[c], kblk) for c, kblk, _, _ in chains]
            zl, lkb = [], []
            for z in zs:
                ls = jnp.minimum(z, 0.0) - jnp.log(1.0 + jnp.exp(-jnp.abs(z)))
                lk = ls - z
                zl.append(ls)
                lkb.append((jnp.where(strict, lk, 0.0) if diagonal else lk).astype(bf16))
            ts = [_dot(x, csum) for x in lkb]
            avs = []
            for (c, _, _, out_of_keys), s, t in zip(chains, zl, ts):
                if diagonal:
                    a = jnp.where(strict, jnp.exp(s + t[:, :L]), 0.0)
                    rs = t[:, L:] + out_of_keys
                else:
                    a = jnp.exp(s + r_ref[c] + t[:, :L])
                    rs = r_ref[c] + t[:, L:] + out_of_keys
                avs.append(a.astype(bf16))
                r_ref[c] = rs
                rmax = rs if rmax is None else jnp.maximum(rmax, rs)
            for (c, _, vblk, _), a in zip(chains, avs):
                if diagonal:
                    acc_ref[c] = _dot(a, vblk)
                else:
                    acc_ref[c] += _dot(a, vblk)
        return jnp.max(rmax) > SB_DEAD_LOG

    for g0 in range(0, nq, SB_GROUP):
        alive0 = sweep(0, True, g0)

        def body(carry, g0=g0):
            _, it = carry
            return sweep(it, False, g0), it + 1

        lax.while_loop(lambda carry: carry[0], body, (alive0, jnp.int32(1)))

    for qb in range(nq):
        for hp in range(npair):
            c = (qb * npair + hp) * 2
            o_ref[qb * L:(qb + 1) * L, hp * LANES:(hp + 1) * LANES] = (
                jnp.where(m0, acc_ref[c], acc_ref[c + 1]).astype(o_ref.dtype))


def _stickbreak(u_sb):
    b, s, _ = u_sb.shape
    nq = SB_QBLOCKS
    tq = nq * CHUNK
    nchain = nq * N_SB
    return pl.pallas_call(
        functools.partial(_sb_kernel, nq=nq),
        out_shape=jax.ShapeDtypeStruct((b, s, D_SB), bf16),
        grid=(b, s // tq),
        in_specs=[pl.BlockSpec((None, tq, D_SB), lambda i, n: (i, n, 0)),
                  pl.BlockSpec((None, s, D_SB), lambda i, n: (i, 0, 1)),
                  pl.BlockSpec((None, s, D_SB), lambda i, n: (i, 0, 2))],
        out_specs=pl.BlockSpec((None, tq, D_SB), lambda i, n: (i, n, 0)),
        scratch_shapes=[pltpu.VMEM((nchain, CHUNK, LANES), bf16),
                        pltpu.VMEM((nchain, CHUNK, CHUNK), f32),
                        pltpu.VMEM((nchain, CHUNK, LANES), f32)],
        compiler_params=_cparams(("parallel", "arbitrary")),
        name="stickbreak",
    )(u_sb, u_sb, u_sb)


def _post_kernel(x_ref, yml_ref, ydil_ref, ysb_ref, mod_ref, gh_ref, wout_ref, wr_hi_ref, wr_lo_ref, rb_ref,
                 xo_ref, h_ref, selt_ref, w_ref):
    mod = mod_ref[...]
    y = jnp.concatenate([yml_ref[...], ydil_ref[...].astype(f32), ysb_ref[...].astype(f32)], axis=1)
    pair = ((_iota((LANES, LANES), 0) // HEAD_DIM) == (_iota((LANES, LANES), 1) // HEAD_DIM)).astype(bf16)
    cols = []
    for cblk in range(y.shape[1] // LANES):
        yb = y[:, cblk * LANES:(cblk + 1) * LANES]
        hi, lo = _split_bf16(yb * yb)
        ms = (_dot(hi, pair) + _dot(lo, pair)) * (1.0 / HEAD_DIM)
        cols.append(yb * lax.rsqrt(ms + EPS))
    yn = (jnp.concatenate(cols, axis=1) * gh_ref[...]).astype(bf16)
    xn = x_ref[...] + mod[2:3, :] * _dot(yn, wout_ref[...])
    xo_ref[...] = xn
    h2 = _rms(xn) * (1.0 + mod[4:5, :]) + mod[3:4, :]
    hh, hl = _split_bf16(h2)
    h_ref[...] = h2
    tm = x_ref.shape[0]
    row = _iota((SUBLANES, tm), 0)
    grow = row < N_GROUPS
    logits = _dot(hh, wr_hi_ref[...]) + _dot(hl, wr_hi_ref[...]) + _dot(hh, wr_lo_ref[...])
    logits_t = logits.T
    sc, bs = [], []
    for j in range(EPG):
        sc.append(lax.logistic(logits_t[j * SUBLANES:(j + 1) * SUBLANES, :]))
        bs.append(jnp.where(grow, sc[j] + rb_ref[j][:, 0:1], -jnp.inf))
    hi01, lo01 = jnp.maximum(bs[0], bs[1]), jnp.minimum(bs[0], bs[1])
    hi23, lo23 = jnp.maximum(bs[2], bs[3]), jnp.minimum(bs[2], bs[3])
    top1 = jnp.maximum(hi01, hi23)
    top2 = jnp.maximum(jnp.minimum(hi01, hi23), jnp.maximum(lo01, lo23))
    gscore = jnp.where(grow, top1 + top2, -jnp.inf)
    gmax = jnp.max(gscore, axis=0, keepdims=True)
    gidx = jnp.min(jnp.where(gscore == gmax, row, SUBLANES), axis=0, keepdims=True)
    gsel = row == gidx
    picks, vals = [], []
    for j in range(EPG):
        rank = jnp.zeros(bs[j].shape, jnp.int32)
        for jj in range(EPG):
            if jj != j:
                ahead = (bs[jj] >= bs[j]) if jj < j else (bs[jj] > bs[j])
                rank = rank + ahead.astype(jnp.int32)
        picks.append(gsel & (rank < 2))
        vals.append(jnp.sum(jnp.where(gsel, sc[j], 0.0), axis=0, keepdims=True))
    erow = _iota((N_EXPERTS, SUBLANES), 0)
    gcol = _iota((N_EXPERTS, SUBLANES), 1)
    sel_t = jnp.zeros((N_EXPERTS, tm), f32)
    for j in range(EPG):
        place = ((erow == gcol * EPG + j) & (gcol < N_GROUPS)).astype(bf16)
        sel_t = sel_t + _dot(place, picks[j].astype(bf16))
    selt_ref[...] = sel_t.astype(selt_ref.dtype)
    seen = jnp.zeros_like(vals[0])
    w_lo = jnp.zeros_like(vals[0])
    w_hi = jnp.zeros_like(vals[0])
    for j in range(EPG):
        picked = jnp.sum(picks[j].astype(f32), axis=0, keepdims=True) > 0.0
        w_lo = w_lo + jnp.where(picked & (seen == 0.0), vals[j], 0.0)
        w_hi = w_hi + jnp.where(picked & (seen == 1.0), vals[j], 0.0)
        seen = seen + picked.astype(f32)
    tot = w_lo + w_hi
    w_ref[...] = jnp.where(row == 0, w_lo / tot, jnp.where(row == 1, w_hi / tot, 0.0))


def _post(x, y_ml, y_dil, y_sb, mod, g_head, w_out, wr_hi, wr_lo, rb, tm):
    b, s, d = x.shape
    tok = lambda w: pl.BlockSpec((None, tm, w), lambda i, j: (i, j, 0))
    full = lambda a: pl.BlockSpec(a.shape, lambda i, j: (0,) * a.ndim)
    tok_lanes = lambda rows: pl.BlockSpec((rows, tm), lambda i, j: (0, i * (s // tm) + j))
    return pl.pallas_call(
        _post_kernel,
        out_shape=(jax.ShapeDtypeStruct((b, s, d), f32),
                   jax.ShapeDtypeStruct((b, s, d), f32),
                   jax.ShapeDtypeStruct((N_EXPERTS, b * s), bf16),
                   jax.ShapeDtypeStruct((SUBLANES, b * s), f32)),
        grid=(b, s // tm),
        in_specs=[tok(d), tok(D_ML), tok(D_DIL), tok(D_SB), pl.BlockSpec((None, 6, d), lambda i, j: (i, 0, 0)),
                  full(g_head), full(w_out), full(wr_hi), full(wr_lo), full(rb)],
        out_specs=(tok(d), tok(d), tok_lanes(N_EXPERTS), tok_lanes(SUBLANES)),
        compiler_params=_cparams(("parallel", "parallel")),
        name="merge_outproj_router",
    )(x, y_ml, y_dil, y_sb, mod, g_head, w_out, wr_hi, wr_lo, rb)


def _plan_kernel(selt_ref, lpos0_ref, lpos1_ref, base_ref, nchunk_ref, lofs_ref, tiles_ref, tails_ref,
                 before_ref, cnt_ref, lofs_sc, *, ntw, ntile_pad, nrow_tiles):
    phase = pl.program_id(0)
    step = pl.program_id(1)
    tk = MOE_TOK
    sel = selt_ref[...]
    tlane = _iota((1, ntw), 1)

    @pl.when((phase == 0) & (step == 0))
    def _():
        cnt_ref[...] = jnp.zeros_like(cnt_ref)
        before_ref[...] = (_iota((tk, tk), 0) < _iota((tk, tk), 1)).astype(bf16)

    @pl.when(phase == 0)
    def _():
        cnt = cnt_ref[...]
        for j in range(PLAN_GROUP):
            here = _dot(sel[:, j * tk:(j + 1) * tk], jnp.ones((tk, ntw), bf16))
            cnt = jnp.where(tlane == step * PLAN_GROUP + j, here, cnt)
        cnt_ref[...] = cnt
        lpos0_ref[...] = jnp.zeros_like(lpos0_ref)
        lpos1_ref[...] = jnp.zeros_like(lpos1_ref)

    @pl.when((phase == 1) & (step == 0))
    def _():
        cnt = cnt_ref[...]
        nch = jnp.floor((cnt + (MOE_CHUNK - 1.0)) * (1.0 / MOE_CHUNK))
        nchb = nch.astype(bf16)
        earlier_tiles = (_iota((ntw, ntw), 0) < _iota((ntw, ntw), 1)).astype(bf16)
        earlier_experts = (_iota((N_EXPERTS, N_EXPERTS), 1) < _iota((N_EXPERTS, N_EXPERTS), 0)).astype(bf16)
        in_expert = _dot(nchb, earlier_tiles) * MOE_CHUNK
        lofs = _dot(earlier_experts, nchb) * MOE_CHUNK
        total = _dot(nchb, jnp.ones((ntw, LANES), bf16)) * MOE_CHUNK
        padded = jnp.floor((total + (MOE_ROWS - 1.0)) * (1.0 / MOE_ROWS)) * MOE_ROWS
        rows, acc = [], jnp.zeros((1, LANES), f32)
        for e in range(N_EXPERTS):
            rows.append(acc)
            acc = acc + padded[e:e + 1, :]
        off = jnp.concatenate(rows, axis=0)
        base_ref[...] = (off[:, 0:1] + in_expert).astype(jnp.int32)
        nchunk_ref[...] = nch.astype(jnp.int32)
        lofs_ref[...] = lofs.astype(jnp.int32)
        lofs_sc[...] = lofs
        lane = _iota((1, LANES), 1)
        tails_ref[...] = jnp.where(
            lane == 0, off + total,
            jnp.where(lane == 1, (padded - total) * (1.0 / MOE_CHUNK),
                      jnp.where(lane == 2, acc, jnp.where(lane == 3, nrow_tiles - acc * (1.0 / MOE_ROWS), 0.0)))
        ).astype(jnp.int32)
        start = _iota((1, ntile_pad), 1).astype(f32) * MOE_ROWS
        ends = (off + padded)[:, 0:1]
        owner = jnp.sum((start >= ends).astype(f32), axis=0, keepdims=True)
        tiles_ref[...] = jnp.zeros_like(tiles_ref)
        tiles_ref[0:1, :] = jnp.minimum(owner, N_EXPERTS - 1.0).astype(jnp.int32)
        tiles_ref[1:2, :] = (start < acc[:, 0:1]).astype(jnp.int32)

    @pl.when(phase == 1)
    def _():
        for j in range(PLAN_GROUP):
            cols = slice(j * tk, (j + 1) * tk)
            mine = jnp.sum(jnp.where(tlane == step * PLAN_GROUP + j, lofs_sc[...], 0.0), axis=1, keepdims=True)
            pos = _dot(sel[:, cols], before_ref[...]) + mine
            picked = sel[:, cols] > 0
            lpos0_ref[:, cols] = jnp.min(jnp.where(picked, pos, 1e9), axis=0, keepdims=True).astype(jnp.int32)
            lpos1_ref[:, cols] = jnp.max(jnp.where(picked, pos, -1.0), axis=0, keepdims=True).astype(jnp.int32)


def _plan(sel_t, nrow_tiles):
    ne, t = sel_t.shape
    nt = t // MOE_TOK
    ntw = -(-nt // LANES) * LANES
    ntile_pad = -(-nrow_tiles // LANES) * LANES
    i32 = jnp.int32
    whole = lambda shape: pl.BlockSpec(shape, lambda p, i: (0, 0))
    per_tile = pl.BlockSpec((1, PLAN_GROUP * MOE_TOK), lambda p, i: (0, i * p))
    lpos0, lpos1, base, nchunk, lofs, tiles, tails = pl.pallas_call(
        functools.partial(_plan_kernel, ntw=ntw, ntile_pad=ntile_pad, nrow_tiles=nrow_tiles),
        out_shape=(jax.ShapeDtypeStruct((1, t), i32), jax.ShapeDtypeStruct((1, t), i32),
                   jax.ShapeDtypeStruct((ne, ntw), i32), jax.ShapeDtypeStruct((ne, ntw), i32),
                   jax.ShapeDtypeStruct((ne, ntw), i32), jax.ShapeDtypeStruct((8, ntile_pad), i32),
                   jax.ShapeDtypeStruct((ne, LANES), i32)),
        grid=(2, nt // PLAN_GROUP),
        in_specs=[pl.BlockSpec((ne, PLAN_GROUP * MOE_TOK), lambda p, i: (0, i))],
        out_specs=(per_tile, per_tile, whole((ne, ntw)), whole((ne, ntw)), whole((ne, ntw)),
                   whole((8, ntile_pad)), whole((ne, LANES))),
        scratch_shapes=[pltpu.VMEM((MOE_TOK, MOE_TOK), bf16), pltpu.VMEM((ne, ntw), f32),
                        pltpu.VMEM((ne, ntw), f32)],
        compiler_params=_cparams(("arbitrary", "arbitrary")),
        name="moe_plan",
    )(sel_t)
    flat = lambda a: a[:, :nt].T.reshape(nt * ne)
    return dict(lpos0=lpos0, lpos1=lpos1, base=flat(base), nchunk=flat(nchunk), lofs=flat(lofs),
                owner=tiles[0, :nrow_tiles], used=tiles[1, :nrow_tiles],
                tail_start=tails[:, 0], tail_chunks=tails[:, 1], unused=tails[0, 2:4])


def _segment_chunks(tile, base_ref, nchunk_ref, lofs_ref, make_copy, start):
    for e in range(N_EXPERTS):
        idx = tile * N_EXPERTS + e
        first_global = base_ref[idx]
        first_local = lofs_ref[idx]

        def one(c, carry, e=e, first_global=first_global, first_local=first_local):
            cp = make_copy(pl.multiple_of(first_local + c * MOE_CHUNK, MOE_CHUNK),
                           pl.multiple_of(first_global + c * MOE_CHUNK, MOE_CHUNK), e % 2)
            cp.start(priority=e % 2) if start else cp.wait()
            return carry

        lax.fori_loop(0, nchunk_ref[idx], one, 0)


def _one_hot_rows(lpos_ref):
    return (_iota((MOE_LOCAL, MOE_TOK), 0) == lpos_ref[...]).astype(bf16)


def _dispatch_kernel(base_ref, nchunk_ref, lofs_ref, tstart_ref, tchunks_ref, unused_ref, lpos0_ref, lpos1_ref,
                     h_ref, xs_hbm, loc, zeros, sem):
    tile = pl.program_id(0)
    last = pl.num_programs(0) - 1
    slot = tile % 2
    place = _one_hot_rows(lpos0_ref) + _one_hot_rows(lpos1_ref)
    loc[slot] = _dot(place, h_ref[...].astype(bf16))

    def copier(slot_):
        def seg_copy(local_row, global_row, s):
            return pltpu.make_async_copy(loc.at[slot_, pl.ds(local_row, MOE_CHUNK)],
                                         xs_hbm.at[pl.ds(global_row, MOE_CHUNK)], sem.at[slot_, s])
        return seg_copy

    _segment_chunks(tile, base_ref, nchunk_ref, lofs_ref, copier(slot), True)

    @pl.when(tile > 0)
    def _():
        _segment_chunks(tile - 1, base_ref, nchunk_ref, lofs_ref, copier(1 - slot), False)

    @pl.when(tile == last)
    def _():
        _segment_chunks(tile, base_ref, nchunk_ref, lofs_ref, copier(slot), False)

    @pl.when(tile == last)
    def _():
        zeros[...] = jnp.zeros_like(zeros)
        for start in (True, False):
            for e in range(N_EXPERTS):
                def one(c, carry, e=e, start=start):
                    row = pl.multiple_of(tstart_ref[e] + c * MOE_CHUNK, MOE_CHUNK)
                    cp = pltpu.make_async_copy(zeros.at[pl.ds(0, MOE_CHUNK)], xs_hbm.at[pl.ds(row, MOE_CHUNK)],
                                               sem.at[0, e % 2])
                    cp.start() if start else cp.wait()
                    return carry
                lax.fori_loop(0, tchunks_ref[e], one, 0)

            def whole_tile(c, carry, start=start):
                row = pl.multiple_of(unused_ref[0] + c * MOE_ROWS, MOE_ROWS)
                cp = pltpu.make_async_copy(zeros, xs_hbm.at[pl.ds(row, MOE_ROWS)], sem.at[0, 0])
                cp.start() if start else cp.wait()
                return carry
            lax.fori_loop(0, unused_ref[1], whole_tile, 0)


def _dispatch(h_rows, plan, nrows):
    t, w = h_rows.shape
    per_tile = lambda: pl.BlockSpec((1, MOE_TOK), lambda i, *_: (0, i))
    return pl.pallas_call(
        _dispatch_kernel,
        out_shape=jax.ShapeDtypeStruct((nrows, w), h_rows.dtype),
        grid_spec=pltpu.PrefetchScalarGridSpec(
            num_scalar_prefetch=6,
            grid=(t // MOE_TOK,),
            in_specs=[per_tile(), per_tile(), pl.BlockSpec((MOE_TOK, w), lambda i, *_: (i, 0))],
            out_specs=pl.BlockSpec(memory_space=pl.ANY),
            scratch_shapes=[pltpu.VMEM((2, MOE_LOCAL, w), f32), pltpu.VMEM((MOE_ROWS, w), f32),
                            pltpu.SemaphoreType.DMA((2, 2))]),
        compiler_params=_cparams(("arbitrary",)),
        name="moe_dispatch",
    )(plan["base"], plan["nchunk"], plan["lofs"], plan["tail_start"], plan["tail_chunks"], plan["unused"],
      plan["lpos0"], plan["lpos1"], h_rows)


def _experts_kernel(owner_ref, used_ref, xs_ref, wg_ref, wu_ref, wd_ref, ys_ref):
    tile = pl.program_id(0)

    @pl.when(used_ref[tile] > 0)
    def _():
        xb = xs_ref[...].astype(bf16)
        g = _dot(xb, wg_ref[...].astype(bf16))
        he = g * lax.logistic(g) * _dot(xb, wu_ref[...].astype(bf16))
        ys_ref[...] = _dot(he.astype(bf16), wd_ref[...].astype(bf16))

    @pl.when(used_ref[tile] == 0)
    def _():
        ys_ref[...] = jnp.zeros_like(ys_ref)


def _experts(xs, owner, used, wg, wu, wd, layer):
    nrows, w = xs.shape
    _, _, d, de = wg.shape
    return pl.pallas_call(
        _experts_kernel,
        out_shape=jax.ShapeDtypeStruct((nrows, w), xs.dtype),
        grid_spec=pltpu.PrefetchScalarGridSpec(
            num_scalar_prefetch=2,
            grid=(nrows // MOE_ROWS,),
            in_specs=[pl.BlockSpec((MOE_ROWS, w), lambda g, own, use: (g * use[g], 0)),
                      pl.BlockSpec((None, None, d, de), lambda g, own, use: (layer, own[g], 0, 0)),
                      pl.BlockSpec((None, None, d, de), lambda g, own, use: (layer, own[g], 0, 0)),
                      pl.BlockSpec((None, None, de, d), lambda g, own, use: (layer, own[g], 0, 0))],
            out_specs=pl.BlockSpec((MOE_ROWS, w), lambda g, own, use: (g, 0))),
        compiler_params=_cparams(("arbitrary",)),
        name="moe_experts",
    )(owner, used, xs, wg, wu, wd)


def _combine_kernel(base_ref, nchunk_ref, lofs_ref, lpos0_ref, lpos1_ref, ys_hbm, x_ref, w_ref, mod_ref, gf_ref,
                    o_ref, loc, sem, *, final):
    tile = pl.program_id(0)
    ntile = pl.num_programs(0)
    slot = tile % 2

    def gather(tile_, slot_, start):
        def seg_copy(local_row, global_row, s):
            return pltpu.make_async_copy(ys_hbm.at[pl.ds(global_row, MOE_CHUNK)],
                                         loc.at[slot_, pl.ds(local_row, MOE_CHUNK)], sem.at[slot_, s])
        _segment_chunks(tile_, base_ref, nchunk_ref, lofs_ref, seg_copy, start)

    @pl.when(tile == 0)
    def _():
        loc[...] = jnp.zeros_like(loc)
        gather(0, 0, True)

    @pl.when(tile + 1 < ntile)
    def _():
        gather(tile + 1, 1 - slot, True)

    gather(tile, slot, False)
    ysl = loc[slot].astype(bf16)
    prow = _iota((MOE_LOCAL, MOE_TOK), 0)
    gates = (jnp.where(prow == lpos0_ref[...], w_ref[0:1, :], 0.0)
             + jnp.where(prow == lpos1_ref[...], w_ref[1:2, :], 0.0))
    y = lax.dot_general(gates.astype(bf16), ysl, _TN, preferred_element_type=f32)
    xn = x_ref[...] + mod_ref[...][5:6, :] * y
    if final:
        xn = _rms(xn) * gf_ref[...]
    o_ref[...] = xn


def _combine(ys, plan, x, w_pair, mod, g_final, final):
    b, s, d = x.shape
    t = b * s
    per_batch = s // MOE_TOK
    per_tile = lambda: pl.BlockSpec((1, MOE_TOK), lambda i, *_: (0, i))
    out = pl.pallas_call(
        functools.partial(_combine_kernel, final=final),
        out_shape=jax.ShapeDtypeStruct((t, d), f32),
        grid_spec=pltpu.PrefetchScalarGridSpec(
            num_scalar_prefetch=3,
            grid=(t // MOE_TOK,),
            in_specs=[per_tile(), per_tile(), pl.BlockSpec(memory_space=pl.ANY),
                      pl.BlockSpec((MOE_TOK, d), lambda i, *_: (i, 0)),
                      pl.BlockSpec((SUBLANES, MOE_TOK), lambda i, *_: (0, i)),
                      pl.BlockSpec((None, 6, d), lambda i, *_: (i // per_batch, 0, 0)),
                      pl.BlockSpec((1, d), lambda i, *_: (0, 0))],
            out_specs=pl.BlockSpec((MOE_TOK, d), lambda i, *_: (i, 0)),
            scratch_shapes=[pltpu.VMEM((2, MOE_LOCAL, d), f32), pltpu.SemaphoreType.DMA((2, 2))]),
        compiler_params=_cparams(("arbitrary",)),
        name="moe_combine_final" if final else "moe_combine",
    )(plan["base"], plan["nchunk"], plan["lofs"], plan["lpos0"], plan["lpos1"], ys,
      x.reshape(t, d), w_pair, mod, g_final)
    return out.reshape(b, s, d)


def _moe(x, h_rows, sel_t, w_pair, mod, wg, wu, wd, layer, g_final, final):
    b, s, d = x.shape
    t = b * s
    worst = 2 * t + (t // MOE_TOK) * N_EXPERTS * (MOE_CHUNK - 1) + N_EXPERTS * MOE_ROWS
    nrows = -(-worst // MOE_ROWS) * MOE_ROWS
    plan = _plan(sel_t, nrows // MOE_ROWS)
    xs = _dispatch(h_rows.reshape(t, d), plan, nrows)
    ys = _experts(xs, plan["owner"], plan["used"], wg, wu, wd, layer)
    return _combine(ys, plan, x, w_pair, mod, g_final, final)


def _block_diag(w):
    n, a, c = w.shape
    out = jnp.zeros((n * a, n * c), w.dtype)
    for i in range(n):
        out = out.at[i * a:(i + 1) * a, i * c:(i + 1) * c].set(w[i])
    return out


def kernel(x, c, w_in, conv_w, conv_b, w_mq, w_mk, w_mv, gate_bias, g_head, w_out, w_ada, b_ada,
           w_router, router_bias, w_gate_e, w_up_e, w_down_e, g_final):
    b, s, d = x.shape
    depth = w_in.shape[0]
    qscale = HEAD_DIM ** -0.5
    tm = min(512, s)

    mod_all = _ada(c, w_ada, b_ada).reshape(depth, b, 6, d)

    wr = jnp.zeros((d, LANES // SUBLANES, SUBLANES), f32).at[:, :EPG, :N_GROUPS].set(
        w_router.reshape(d, N_GROUPS, EPG).transpose(0, 2, 1)).reshape(d, LANES)
    wr_hi = wr.astype(bf16)
    wr_lo = (wr - wr_hi.astype(f32)).astype(bf16)
    rb = jnp.zeros((EPG, SUBLANES, LANES), f32).at[:, :N_GROUPS, :].set(
        jnp.broadcast_to(router_bias.reshape(N_GROUPS, EPG).T[:, :, None], (EPG, N_GROUPS, LANES)))

    slope = jnp.exp2(-ALIBI_MAX_BIAS * jnp.arange(1, N_DIL + 1, dtype=f32) / N_DIL)
    slopes = jnp.broadcast_to(slope.reshape(N_DIL // 2, 2, 1, 1), (N_DIL // 2, 2, 8, LANES))
    g_final2 = g_final.reshape(1, d)

    w_ml_all, w_dil_all, w_sb_all = _wprep(w_in, min(256, d))

    for l in range(depth):
        mod = mod_all[l]

        u_ml, u_dil, u_sb, g_row = _inproj(x, mod, w_ml_all[l], w_dil_all[l], w_sb_all[l], tm)

        gb = gate_bias[l]
        gb_col = jnp.zeros((1, LANES), f32).at[0, :2 * N_ML].set(gb.reshape(-1))
        gb_row = gb.reshape(2 * N_ML, 1)
        y_ml = _mlstm(u_ml, g_row, conv_w[l], conv_b[l].reshape(1, D_ML),
                      _block_diag(w_mq[l]).astype(bf16), (_block_diag(w_mk[l]) * qscale).astype(bf16),
                      _block_diag(w_mv[l]).astype(bf16), gb_col, gb_row)
        y_dil = _dilated(u_dil, slopes)
        y_sb = _stickbreak(u_sb)

        x, h_rows, sel_t, w_pair = _post(x, y_ml, y_dil, y_sb, mod, g_head[l].reshape(1, d),
                                           w_out[l].astype(bf16), wr_hi, wr_lo, rb, tm)
        x = _moe(x, h_rows, sel_t, w_pair, mod, w_gate_e, w_up_e, w_down_e, l, g_final2,
                 final=(l == depth - 1))
    return x
```
